```python
import math
import jax, jax.numpy as jnp
from jax import lax
import numpy as np


D_MODEL = 1024
BATCH = 8
SEQ = 4096
DEPTH = 4
DEC_BATCH = 16
DEC_SEQ = 4096
PAST_LEN = 128

HEAD_DIM = 64
N_HEADS_TOTAL = D_MODEL // HEAD_DIM
N_MEM_HEADS = 4
N_MIX_HEADS = N_HEADS_TOTAL - N_MEM_HEADS
MIX_WIDTH = N_MIX_HEADS * HEAD_DIM
MEM_WIDTH = N_MEM_HEADS * HEAD_DIM
ATTN_WIDTH = MIX_WIDTH + MEM_WIDTH
N_MEM_TOKENS = 256
N_MIXERS = 3
DIFF_QK_DIM = HEAD_DIM // 2
GRID_W = 64
NA_ROWS = 8
NA_COLS = 16
N_KV_HEADS = 4
GQA_GROUP = N_MIX_HEADS // N_KV_HEADS
ROPE_THETA = 10000.0
ROPE_AXIS_DIM = HEAD_DIM // 2
N_EXPERTS = 64
TOP_K = 8
N_GROUPS = 8
TOPK_GROUPS = 4
D_EXPERT = 256
D_SHARED = 256
ROUTED_SCALE = 2.5
Q_BLOCK = 128
TOKEN_BLOCK = 128
DN_ALPHA = (2.0 * DEPTH) ** 0.25
DN_BETA = (8.0 * DEPTH) ** -0.25
LN_EPS = 1e-5
RMS_EPS = 1e-6
N_LAYERS_A = (DEPTH + 2) // 3
N_LAYERS_B = (DEPTH + 1) // 3
N_LAYERS_C = DEPTH // 3
W_IN_A = 3 * MIX_WIDTH + MEM_WIDTH
W_IN_B = 3 * MIX_WIDTH + MEM_WIDTH
W_IN_C = MIX_WIDTH + 2 * N_KV_HEADS * HEAD_DIM + MEM_WIDTH

kernel_name = "hybrid_diff_natten_gqa_moe_encoder"


def alibi_slopes(n):
    def pow2_slopes(m):
        start = 2.0 ** (-8.0 / m)
        return [start ** (i + 1) for i in range(m)]
    if math.log2(n).is_integer():
        s = pow2_slopes(n)
    else:
        c = 2 ** math.floor(math.log2(n))
        s = pow2_slopes(c) + pow2_slopes(2 * c)[0::2][: n - c]
    return np.array(s, np.float32)


def diff_lambda_init(layer_idx):
    return 0.8 - 0.6 * math.exp(-0.3 * layer_idx)


def layer_norm(x, g, b):
    xf = x.astype(jnp.float32)
    mu = jnp.mean(xf, -1, keepdims=True)
    var = jnp.mean(jnp.square(xf - mu), -1, keepdims=True)
    y = (xf - mu) * lax.rsqrt(var + LN_EPS) * g.astype(jnp.float32) + b.astype(jnp.float32)
    return y.astype(x.dtype)


def rms_norm(x, g):
    xf = x.astype(jnp.float32)
    y = xf * lax.rsqrt(jnp.mean(xf * xf, -1, keepdims=True) + RMS_EPS) * g.astype(jnp.float32)
    return y.astype(x.dtype)


def diff_attention(cols, lam, subln_g, lambda_init):
    B, S, _ = cols.shape
    H, dq = N_MIX_HEADS, DIFF_QK_DIM
    q, k, v = jnp.split(cols, 3, axis=-1)
    q = q.reshape(B, S, H, 2, dq)
    k = k.reshape(B, S, H, 2, dq)
    v = v.reshape(B, S, H, HEAD_DIM)
    lf = lam.astype(jnp.float32)
    lam_full = jnp.exp(jnp.sum(lf[0] * lf[1])) - jnp.exp(jnp.sum(lf[2] * lf[3])) + lambda_init
    slopes = jnp.asarray(alibi_slopes(H))
    key_pos = jnp.arange(S, dtype=jnp.float32)
    scale = dq ** -0.5
    nb = S // Q_BLOCK
    q_blocks = q.reshape(B, nb, Q_BLOCK, H, 2, dq).transpose(1, 0, 2, 3, 4, 5)
    starts = jnp.arange(nb, dtype=jnp.float32) * Q_BLOCK

    def block(args):
        qb, start = args
        s = jnp.einsum('bqhcd,bkhcd->bhcqk', qb, k, preferred_element_type=jnp.float32) * scale
        qpos = start + jnp.arange(Q_BLOCK, dtype=jnp.float32)
        dist = jnp.abs(qpos[:, None] - key_pos[None, :])
        s = s - slopes[None, :, None, None, None] * dist[None, None, None]
        p = jax.nn.softmax(s, axis=-1)
        a = p[:, :, 0] - lam_full * p[:, :, 1]
        return jnp.einsum('bhqk,bkhd->bqhd', a.astype(v.dtype), v)

    o = lax.map(block, (q_blocks, starts))
    o = o.transpose(1, 0, 2, 3, 4).reshape(B, S, H, HEAD_DIM)
    o = rms_norm(o, subln_g) * (1.0 - lambda_init)
    return o.reshape(B, S, MIX_WIDTH)


def neighbourhood_attention(cols, rpb):
    B, S, _ = cols.shape
    rows = S // GRID_W
    wr = min(NA_ROWS, rows)
    H = N_MIX_HEADS
    q, k, v = jnp.split(cols, 3, axis=-1)
    shp = (B, rows, GRID_W, H, HEAD_DIM)
    q, k, v = q.reshape(shp), k.reshape(shp), v.reshape(shp)
    scale = HEAD_DIM ** -0.5
    col = jnp.arange(GRID_W)
    col_start = jnp.clip(col - NA_COLS // 2, 0, GRID_W - NA_COLS)
    col_mask = (col[None, :] >= col_start[:, None]) & (col[None, :] < col_start[:, None] + NA_COLS)
    col_bias_idx = jnp.clip(col[None, :] - col[:, None] + NA_COLS - 1, 0, 2 * NA_COLS - 2)
    q_rows = jnp.moveaxis(q, 1, 0)

    def row_block(args):
        qr, r = args
        rs = jnp.clip(r - wr // 2, 0, rows - wr)
        kr = lax.dynamic_slice_in_dim(k, rs, wr, axis=1)
        vr = lax.dynamic_slice_in_dim(v, rs, wr, axis=1)
        s = jnp.einsum('bchd,bjkhd->bhcjk', qr, kr, preferred_element_type=jnp.float32) * scale
        row_off = rs + jnp.arange(wr) - r
        bias = rpb[:, row_off + NA_ROWS - 1][:, :, col_bias_idx]
        s = s + bias.transpose(0, 2, 1, 3)[None].astype(jnp.float32)
        s = jnp.where(col_mask[None, None, :, None, :], s, -jnp.inf)
        p = jax.nn.softmax(s.reshape(B, H, GRID_W, wr * GRID_W), axis=-1)
        p = p.reshape(B, H, GRID_W, wr, GRID_W)
        return jnp.einsum('bhcjk,bjkhd->bchd', p.astype(vr.dtype), vr)

    o = lax.map(row_block, (q_rows, jnp.arange(rows)))
    return jnp.moveaxis(o, 0, 1).reshape(B, S, MIX_WIDTH)


def axial_rope_tables(S):
    t = jnp.arange(S)
    row = (t // GRID_W).astype(jnp.float32)
    colp = (t % GRID_W).astype(jnp.float32)
    inv = ROPE_THETA ** (-jnp.arange(0, ROPE_AXIS_DIM, 2, dtype=jnp.float32) / ROPE_AXIS_DIM)
    ang = jnp.concatenate([row[:, None] * inv[None], colp[:, None] * inv[None]], axis=-1)
    return jnp.cos(ang), jnp.sin(ang)


def apply_axial_rope(x, cos, sin):
    S = x.shape[1]
    bshape = (1, S) + (1,) * (x.ndim - 3) + (HEAD_DIM // 2,)
    c, s = cos.reshape(bshape), sin.reshape(bshape)
    xf = x.astype(jnp.float32).reshape(x.shape[:-1] + (HEAD_DIM // 2, 2))
    x1, x2 = xf[..., 0], xf[..., 1]
    out = jnp.stack([x1 * c - x2 * s, x1 * s + x2 * c], axis=-1).reshape(x.shape)
    return out.astype(x.dtype)


def gqa_axial_attention(cols, q_norm_g, k_norm_g):
    B, S, _ = cols.shape
    kvw = N_KV_HEADS * HEAD_DIM
    q = cols[..., :MIX_WIDTH].reshape(B, S, N_KV_HEADS, GQA_GROUP, HEAD_DIM)
    k = cols[..., MIX_WIDTH:MIX_WIDTH + kvw].reshape(B, S, N_KV_HEADS, HEAD_DIM)
    v = cols[..., MIX_WIDTH + kvw:].reshape(B, S, N_KV_HEADS, HEAD_DIM)
    cos, sin = axial_rope_tables(S)
    q = apply_axial_rope(rms_norm(q, q_norm_g), cos, sin)
    k = apply_axial_rope(rms_norm(k, k_norm_g), cos, sin)
    scale = HEAD_DIM ** -0.5
    nb = S // Q_BLOCK
    q_blocks = q.reshape(B, nb, Q_BLOCK, N_KV_HEADS, GQA_GROUP, HEAD_DIM).transpose(1, 0, 2, 3, 4, 5)

    def block(qb):
        s = jnp.einsum('bqkgd,bskd->bkgqs', qb, k, preferred_element_type=jnp.float32) * scale
        p = jax.nn.softmax(s, axis=-1)
        return jnp.einsum('bkgqs,bskd->bqkgd', p.astype(v.dtype), v)

    o = lax.map(block, q_blocks)
    return o.transpose(1, 0, 2, 3, 4, 5).reshape(B, S, MIX_WIDTH)


def memory_attention(q_mem, mem, w_mem_kv):
    B, S, _ = q_mem.shape
    M = mem.shape[1]
    q = q_mem.reshape(B, S, N_MEM_HEADS, HEAD_DIM)
    kv = jnp.einsum('bmd,de->bme', mem, w_mem_kv).reshape(B, M, 2, N_MEM_HEADS, HEAD_DIM)
    k, v = kv[:, :, 0], kv[:, :, 1]
    s = jnp.einsum('bshd,bmhd->bhsm', q, k, preferred_element_type=jnp.float32) * (HEAD_DIM ** -0.5)
    p = jax.nn.softmax(s, axis=-1)
    o = jnp.einsum('bhsm,bmhd->bshd', p.astype(v.dtype), v)
    return o.reshape(B, S, MEM_WIDTH)


def moe_ffn(x, router, router_bias, e_gate, e_up, e_down, s_gate, s_up, s_down):
    B, S, D = x.shape
    tokens = x.reshape((B * S) // TOKEN_BLOCK, TOKEN_BLOCK, D)
    per_group = N_EXPERTS // N_GROUPS

    def block(xb):
        scores = jax.nn.sigmoid(jnp.einsum('td,de->te', xb, router, preferred_element_type=jnp.float32))
        choice = scores + router_bias.astype(jnp.float32)
        grp = choice.reshape(TOKEN_BLOCK, N_GROUPS, per_group)
        grp_score = jnp.sum(lax.top_k(grp, 2)[0], axis=-1)
        _, gidx = lax.top_k(grp_score, TOPK_GROUPS)
        gmask = jnp.sum(jax.nn.one_hot(gidx, N_GROUPS, dtype=jnp.float32), axis=-2)
        emask = jnp.repeat(gmask, per_group, axis=-1)
        masked = jnp.where(emask > 0, choice, -jnp.inf)
        _, eidx = lax.top_k(masked, TOP_K)
        w = jnp.take_along_axis(scores, eidx, axis=-1)
        w = w / jnp.sum(w, axis=-1, keepdims=True) * ROUTED_SCALE
        gates = jnp.sum(jax.nn.one_hot(eidx, N_EXPERTS, dtype=jnp.float32) * w[..., None], axis=-2)
        h = jax.nn.silu(jnp.einsum('td,edf->tef', xb, e_gate)) * jnp.einsum('td,edf->tef', xb, e_up)
        h = h * gates[..., None].astype(h.dtype)
        routed = jnp.einsum('tef,efd->td', h, e_down)
        shared = jnp.einsum('tf,fd->td', jax.nn.silu(xb @ s_gate) * (xb @ s_up), s_down)
        return routed + shared

    return lax.map(block, tokens).reshape(B, S, D)


def setup_inputs(seed: int = 0) -> dict:
    key = jax.random.key(seed)
    ks = jax.random.split(key, 32)

    def nrm(k, shape, scale):
        return jax.random.normal(k, shape, jnp.float32) * scale

    D = D_MODEL
    return {
        'x_prompt': nrm(ks[0], (BATCH, SEQ, D), 1.0),
        'x_sample': nrm(ks[1], (DEC_BATCH, DEC_SEQ, D), 1.0),
        'mem_prompt': nrm(ks[2], (BATCH, N_MEM_TOKENS, D), 1.0),
        'mem_sample': nrm(ks[3], (DEC_BATCH, N_MEM_TOKENS, D), 1.0),
        'a_w_in': nrm(ks[4], (N_LAYERS_A, D, W_IN_A), D ** -0.5),
        'a_lambda': nrm(ks[5], (N_LAYERS_A, 4, DIFF_QK_DIM), 0.1),
        'a_subln': 1.0 + nrm(ks[6], (N_LAYERS_A, HEAD_DIM), 0.02),
        'b_w_in': nrm(ks[7], (N_LAYERS_B, D, W_IN_B), D ** -0.5),
        'b_rpb': nrm(ks[8], (N_LAYERS_B, N_MIX_HEADS, 2 * NA_ROWS - 1, 2 * NA_COLS - 1), 0.1),
        'c_w_in': nrm(ks[9], (N_LAYERS_C, D, W_IN_C), D ** -0.5),
        'c_q_norm': 1.0 + nrm(ks[10], (N_LAYERS_C, HEAD_DIM), 0.02),
        'c_k_norm': 1.0 + nrm(ks[11], (N_LAYERS_C, HEAD_DIM), 0.02),
        'w_mem_kv': nrm(ks[12], (DEPTH, D, 2 * MEM_WIDTH), D ** -0.5),
        'w_out': nrm(ks[13], (DEPTH, ATTN_WIDTH, D), DN_BETA * ATTN_WIDTH ** -0.5),
        'ln1_g': 1.0 + nrm(ks[14], (DEPTH, D), 0.02),
        'ln1_b': nrm(ks[15], (DEPTH, D), 0.02),
        'router': nrm(ks[16], (DEPTH, D, N_EXPERTS), D ** -0.5),
        'router_bias': nrm(ks[17], (DEPTH, N_EXPERTS), 0.01),
        'e_gate': nrm(ks[18], (DEPTH, N_EXPERTS, D, D_EXPERT), D ** -0.5),
        'e_up': nrm(ks[19], (DEPTH, N_EXPERTS, D, D_EXPERT), D ** -0.5),
        'e_down': nrm(ks[20], (DEPTH, N_EXPERTS, D_EXPERT, D), DN_BETA * D_EXPERT ** -0.5),
        's_gate': nrm(ks[21], (DEPTH, D, D_SHARED), D ** -0.5),
        's_up': nrm(ks[22], (DEPTH, D, D_SHARED), D ** -0.5),
        's_down': nrm(ks[23], (DEPTH, D_SHARED, D), DN_BETA * D_SHARED ** -0.5),
        'ln2_g': 1.0 + nrm(ks[24], (DEPTH, D), 0.02),
        'ln2_b': nrm(ks[25], (DEPTH, D), 0.02),
    }


def reference(x_prompt, x_sample, mem_prompt, mem_sample, a_w_in, a_lambda, a_subln, b_w_in, b_rpb,
              c_w_in, c_q_norm, c_k_norm, w_mem_kv, w_out, ln1_g, ln1_b, router, router_bias,
              e_gate, e_up, e_down, s_gate, s_up, s_down, ln2_g, ln2_b):
    w_in_by_mixer = (a_w_in, b_w_in, c_w_in)

    def trunk(x, mem):
        for i in range(DEPTH):
            mixer_id, occ = i % N_MIXERS, i // N_MIXERS
            proj = jnp.einsum('bsd,de->bse', x, w_in_by_mixer[mixer_id][occ])
            cols, q_mem = proj[..., :-MEM_WIDTH], proj[..., -MEM_WIDTH:]
            if mixer_id == 0:
                mix = diff_attention(cols, a_lambda[occ], a_subln[occ], diff_lambda_init(i))
            elif mixer_id == 1:
                mix = neighbourhood_attention(cols, b_rpb[occ])
            else:
                mix = gqa_axial_attention(cols, c_q_norm[occ], c_k_norm[occ])
            mem_out = memory_attention(q_mem, mem, w_mem_kv[i])
            h = jnp.einsum('bse,ed->bsd', jnp.concatenate([mix, mem_out], axis=-1), w_out[i])
            x = layer_norm(DN_ALPHA * x + h, ln1_g[i], ln1_b[i])
            h = moe_ffn(x, router[i], router_bias[i], e_gate[i], e_up[i], e_down[i],
                        s_gate[i], s_up[i], s_down[i])
            x = layer_norm(DN_ALPHA * x + h, ln2_g[i], ln2_b[i])
        return x

    y_prompt = trunk(x_prompt, mem_prompt)
    y_sample = trunk(x_sample, mem_sample)
    return (y_prompt, y_sample)
```

```python
import functools
import math

import numpy as np
import jax
import jax.numpy as jnp
from jax import lax
from jax.experimental import pallas as pl
from jax.experimental.pallas import tpu as pltpu

F32 = jnp.float32
BF16 = jnp.bfloat16

D_MODEL = 1024
HEAD_DIM = 64
LANES = 128
N_MIX_HEADS = 12
N_MIX_PAIRS = N_MIX_HEADS // 2
N_MEM_HEADS = 4
N_MEM_PAIRS = N_MEM_HEADS // 2
MIX_WIDTH = N_MIX_HEADS * HEAD_DIM
MEM_WIDTH = N_MEM_HEADS * HEAD_DIM
N_MIXERS = 3
DIFF_QK_DIM = HEAD_DIM // 2
GRID_W = 64
NA_ROWS = 8
NA_COLS = 16
N_KV_HEADS = 4
GQA_GROUP = N_MIX_HEADS // N_KV_HEADS
ROPE_THETA = 10000.0
ROPE_AXIS_DIM = HEAD_DIM // 2
N_EXPERTS = 64
TOP_K = 8
N_GROUPS = 8
TOPK_GROUPS = 4
PER_GROUP = N_EXPERTS // N_GROUPS
D_EXPERT = 256
ROUTED_SCALE = 2.5
LN_EPS = 1e-5
RMS_EPS = 1e-6
NEG_BIG = -1e30

VMEM_LIMIT = 48 * 1024 * 1024

NT_DIMS = (((1,), (1,)), ((), ()))


def _cparams(*sem):
    return pltpu.CompilerParams(dimension_semantics=sem, vmem_limit_bytes=VMEM_LIMIT)


def _alibi_slopes(n):
    def pow2_slopes(m):
        start = 2.0 ** (-8.0 / m)
        return [start ** (i + 1) for i in range(m)]
    if math.log2(n).is_integer():
        s = pow2_slopes(n)
    else:
        c = 2 ** math.floor(math.log2(n))
        s = pow2_slopes(c) + pow2_slopes(2 * c)[0::2][: n - c]
    return np.array(s, np.float32)


def _diff_lambda_init(layer_idx):
    return 0.8 - 0.6 * math.exp(-0.3 * layer_idx)


def _lane_iota(shape):
    return lax.broadcasted_iota(jnp.int32, shape, len(shape) - 1)


def _layer_norm(y, g, b):
    mu = jnp.mean(y, axis=-1, keepdims=True)
    yc = y - mu
    var = jnp.mean(yc * yc, axis=-1, keepdims=True)
    return yc * lax.rsqrt(var + LN_EPS) * g + b


def _head_rms_scale(x):
    lo = _lane_iota(x.shape) < HEAD_DIM
    xx = x * x
    ss_lo = jnp.sum(jnp.where(lo, xx, 0.0), axis=-1, keepdims=True)
    ss_hi = jnp.sum(jnp.where(lo, 0.0, xx), axis=-1, keepdims=True)
    inv = 1.0 / HEAD_DIM
    return jnp.where(lo, lax.rsqrt(ss_lo * inv + RMS_EPS), lax.rsqrt(ss_hi * inv + RMS_EPS))


def _inproj_kernel(x_ref, w_ref, cs_ref, o_ref):
    r = jnp.dot(x_ref[...].astype(BF16), w_ref[...], preferred_element_type=F32)
    r = r * cs_ref[...]
    for p in range(o_ref.shape[0]):
        o_ref[p] = r[:, p * LANES:(p + 1) * LANES].astype(BF16)


def _inproj_rope_kernel(x_ref, w_ref, cs_ref, g_ref, cos_ref, sin_ref, o_ref, *, n_rope_pairs):
    r = jnp.dot(x_ref[...].astype(BF16), w_ref[...], preferred_element_type=F32)
    cs = cs_ref[...]
    cos = cos_ref[...]
    sin = sin_ref[...]
    even = (_lane_iota(cos.shape) % 2) == 0
    for p in range(o_ref.shape[0]):
        xp = r[:, p * LANES:(p + 1) * LANES]
        if p < n_rope_pairs:
            xn = xp * _head_rms_scale(xp) * g_ref[p]
            partner = jnp.where(even, pltpu.roll(xn, LANES - 1, 1), pltpu.roll(xn, 1, 1))
            xp = xn * cos + partner * sin
        o_ref[p] = (xp * cs[:, p * LANES:(p + 1) * LANES]).astype(BF16)


def _in_projection(x, w, colscale, tm, seq, rope=None):
    n_tok, d = x.shape
    width = w.shape[1]
    n_pairs = width // LANES
    in_specs = [
        pl.BlockSpec((tm, d), lambda i: (i, 0)),
        pl.BlockSpec((d, width), lambda i: (0, 0)),
        pl.BlockSpec((1, width), lambda i: (0, 0)),
    ]
    args = [x, w, colscale]
    if rope is None:
        body = _inproj_kernel
    else:
        gains, cos, sin = rope
        tiles_per_seq = seq // tm
        body = functools.partial(_inproj_rope_kernel, n_rope_pairs=gains.shape[0])
        in_specs += [
            pl.BlockSpec(gains.shape, lambda i: (0, 0, 0)),
            pl.BlockSpec((tm, LANES), lambda i: (i % tiles_per_seq, 0)),
            pl.BlockSpec((tm, LANES), lambda i: (i % tiles_per_seq, 0)),
        ]
        args += [gains, cos, sin]
    return pl.pallas_call(
        body,
        out_shape=jax.ShapeDtypeStruct((n_pairs, n_tok, LANES), BF16),
        grid=(n_tok // tm,),
        in_specs=in_specs,
        out_specs=pl.BlockSpec((n_pairs, tm, LANES), lambda i: (0, i, 0)),
        compiler_params=_cparams("parallel"),
        name="in_projection",
    )(*args)


def _matmul_kernel(x_ref, w_ref, o_ref):
    o_ref[...] = jnp.dot(x_ref[...].astype(BF16), w_ref[...],
                         preferred_element_type=F32).astype(o_ref.dtype)


def _matmul(x, w, tm):
    m, k = x.shape
    n = w.shape[1]
    return pl.pallas_call(
        _matmul_kernel,
        out_shape=jax.ShapeDtypeStruct((m, n), BF16),
        grid=(m // tm,),
        in_specs=[pl.BlockSpec((tm, k), lambda i: (i, 0)), pl.BlockSpec((k, n), lambda i: (0, 0))],
        out_specs=pl.BlockSpec((tm, n), lambda i: (i, 0)),
        compiler_params=_cparams("parallel"),
        name="mem_kv_projection",
    )(x, w)


def _online_softmax_step(s_list, v, m_ref, l_ref, acc_ref, tq):
    p_list = []
    alpha_list = []
    for a, s in enumerate(s_list):
        rows = slice(a * tq, (a + 1) * tq)
        m_prev = m_ref[rows]
        m_new = jnp.maximum(m_prev, jnp.max(s, axis=-1, keepdims=True))
        alpha = jnp.exp(m_prev - m_new)
        p = jnp.exp(s - m_new)
        l_ref[rows] = alpha * l_ref[rows] + jnp.sum(p, axis=-1, keepdims=True)
        m_ref[rows] = m_new
        p_list.append(p.astype(BF16))
        alpha_list.append(alpha)
    p_all = jnp.concatenate(p_list, axis=0)
    alpha_all = jnp.concatenate(alpha_list, axis=0)
    acc_ref[...] = alpha_all * acc_ref[...] + jnp.dot(p_all, v, preferred_element_type=F32)


def _diff_attn_kernel(q_ref, k_ref, v_ref, slope_ref, lam_ref, g_ref, o_ref,
                      qs_ref, m_ref, l_ref, acc_ref, *, tq, tk, lambda_init):
    i = pl.program_id(2)
    j = pl.program_id(3)
    n_maps = 4

    @pl.when(j == 0)
    def _():
        q = q_ref[0].astype(F32)
        quarter = _lane_iota(q.shape) // DIFF_QK_DIM
        for a in range(n_maps):
            qs_ref[a * tq:(a + 1) * tq] = jnp.where(quarter == a, q, 0.0).astype(BF16)
        m_ref[...] = jnp.full(m_ref.shape, -jnp.inf, F32)
        l_ref[...] = jnp.zeros(l_ref.shape, F32)
        acc_ref[...] = jnp.zeros(acc_ref.shape, F32)

    s = lax.dot_general(qs_ref[...], k_ref[0], NT_DIMS, preferred_element_type=F32)
    qpos = i * tq + lax.broadcasted_iota(jnp.int32, (tq, tk), 0)
    kpos = j * tk + lax.broadcasted_iota(jnp.int32, (tq, tk), 1)
    dist = jnp.abs(qpos - kpos).astype(F32)
    slopes = slope_ref[0]
    s_list = []
    for a in range(n_maps):
        bias = slopes[a // 2:a // 2 + 1, 0:1] * dist
        s_list.append(s[a * tq:(a + 1) * tq] - bias)
    _online_softmax_step(s_list, v_ref[0], m_ref, l_ref, acc_ref, tq)

    @pl.when(j == pl.num_programs(3) - 1)
    def _():
        lam = lam_ref[...]
        lam_full = (jnp.exp(jnp.sum(lam[0:1] * lam[1:2], axis=-1, keepdims=True))
                    - jnp.exp(jnp.sum(lam[2:3] * lam[3:4], axis=-1, keepdims=True)) + lambda_init)
        heads = []
        for h in range(2):
            r0 = slice((2 * h) * tq, (2 * h + 1) * tq)
            r1 = slice((2 * h + 1) * tq, (2 * h + 2) * tq)
            heads.append(acc_ref[r0] / l_ref[r0] - lam_full * (acc_ref[r1] / l_ref[r1]))
        o = jnp.where(_lane_iota(heads[0].shape) < HEAD_DIM, heads[0], heads[1])
        o = o * _head_rms_scale(o) * g_ref[...] * (1.0 - lambda_init)
        o_ref[0] = o.astype(BF16)


def _diff_attention(heads, slopes, lam, gain, batch, seq, tq, tk, lambda_init):
    n_tok = heads.shape[1]
    nq, nk = seq // tq, seq // tk
    body = functools.partial(_diff_attn_kernel, tq=tq, tk=tk, lambda_init=lambda_init)
    return pl.pallas_call(
        body,
        out_shape=jax.ShapeDtypeStruct((N_MIX_PAIRS, n_tok, LANES), BF16),
        grid=(batch, N_MIX_PAIRS, nq, nk),
        in_specs=[
            pl.BlockSpec((1, tq, LANES), lambda b, p, i, j: (p, b * nq + i, 0)),
            pl.BlockSpec((1, tk, LANES), lambda b, p, i, j: (N_MIX_PAIRS + p, b * nk + j, 0)),
            pl.BlockSpec((1, tk, LANES), lambda b, p, i, j: (2 * N_MIX_PAIRS + p, b * nk + j, 0)),
            pl.BlockSpec((1, 8, LANES), lambda b, p, i, j: (p, 0, 0)),
            pl.BlockSpec(lam.shape, lambda b, p, i, j: (0, 0)),
            pl.BlockSpec((1, LANES), lambda b, p, i, j: (0, 0)),
        ],
        out_specs=pl.BlockSpec((1, tq, LANES), lambda b, p, i, j: (p, b * nq + i, 0)),
        scratch_shapes=[
            pltpu.VMEM((4 * tq, LANES), BF16),
            pltpu.VMEM((4 * tq, 1), F32),
            pltpu.VMEM((4 * tq, 1), F32),
            pltpu.VMEM((4 * tq, LANES), F32),
        ],
        compiler_params=_cparams("parallel", "parallel", "parallel", "arbitrary"),
        name="diff_attention",
    )(heads, heads, heads, slopes, lam, gain)


def _gqa_attn_kernel(q_ref, k_ref, v_ref, o_ref, qs_ref, m_ref, l_ref, acc_ref, *, tq, tk):
    j = pl.program_id(3)
    n_qpairs = q_ref.shape[0]
    n_maps = 2 * n_qpairs

    @pl.when(j == 0)
    def _():
        for t in range(n_qpairs):
            q = q_ref[t].astype(F32)
            lo = _lane_iota(q.shape) < HEAD_DIM
            qs_ref[(2 * t) * tq:(2 * t + 1) * tq] = jnp.where(lo, q, 0.0).astype(BF16)
            qs_ref[(2 * t + 1) * tq:(2 * t + 2) * tq] = jnp.where(lo, 0.0, q).astype(BF16)
        m_ref[...] = jnp.full(m_ref.shape, -jnp.inf, F32)
        l_ref[...] = jnp.zeros(l_ref.shape, F32)
        acc_ref[...] = jnp.zeros(acc_ref.shape, F32)

    s = lax.dot_general(qs_ref[...], k_ref[0], NT_DIMS, preferred_element_type=F32)
    s_list = [s[a * tq:(a + 1) * tq] for a in range(n_maps)]
    _online_softmax_step(s_list, v_ref[0], m_ref, l_ref, acc_ref, tq)

    @pl.when(j == pl.num_programs(3) - 1)
    def _():
        for t in range(n_qpairs):
            r0 = slice((2 * t) * tq, (2 * t + 1) * tq)
            r1 = slice((2 * t + 1) * tq, (2 * t + 2) * tq)
            o_lo = acc_ref[r0] / l_ref[r0]
            o_hi = acc_ref[r1] / l_ref[r1]
            o_ref[t] = jnp.where(_lane_iota(o_lo.shape) < HEAD_DIM, o_lo, o_hi).astype(BF16)


def _gqa_attention(heads, batch, seq, tq, tk):
    n_tok = heads.shape[1]
    nq, nk = seq // tq, seq // tk
    n_kv_pairs = N_KV_HEADS // 2
    qp = N_MIX_PAIRS // n_kv_pairs
    body = functools.partial(_gqa_attn_kernel, tq=tq, tk=tk)
    return pl.pallas_call(
        body,
        out_shape=jax.ShapeDtypeStruct((N_MIX_PAIRS, n_tok, LANES), BF16),
        grid=(batch, n_kv_pairs, nq, nk),
        in_specs=[
            pl.BlockSpec((qp, tq, LANES), lambda b, p, i, j: (p, b * nq + i, 0)),
            pl.BlockSpec((1, tk, LANES), lambda b, p, i, j: (N_MIX_PAIRS + p, b * nk + j, 0)),
            pl.BlockSpec((1, tk, LANES),
                         lambda b, p, i, j: (N_MIX_PAIRS + n_kv_pairs + p, b * nk + j, 0)),
        ],
        out_specs=pl.BlockSpec((qp, tq, LANES), lambda b, p, i, j: (p, b * nq + i, 0)),
        scratch_shapes=[
            pltpu.VMEM((2 * qp * tq, LANES), BF16),
            pltpu.VMEM((2 * qp * tq, 1), F32),
            pltpu.VMEM((2 * qp * tq, 1), F32),
            pltpu.VMEM((2 * qp * tq, LANES), F32),
        ],
        compiler_params=_cparams("parallel", "parallel", "parallel", "arbitrary"),
        name="gqa_attention",
    )(heads, heads, heads)


def _na_attn_kernel(q_ref, kp_ref, kc_ref, kn_ref, vp_ref, vc_ref, vn_ref, bias_ref, o_ref,
                    kbuf_ref, vbuf_ref, *, n_grid_rows):
    rb = pl.program_id(2)
    blk = NA_ROWS * GRID_W
    win = NA_ROWS * GRID_W
    kbuf_ref[0:blk] = kp_ref[0]
    kbuf_ref[blk:2 * blk] = kc_ref[0]
    kbuf_ref[2 * blk:3 * blk] = kn_ref[0]
    vbuf_ref[0:blk] = vp_ref[0]
    vbuf_ref[blk:2 * blk] = vc_ref[0]
    vbuf_ref[2 * blk:3 * blk] = vn_ref[0]
    lo = _lane_iota((GRID_W, LANES)) < HEAD_DIM
    for u in range(NA_ROWS):
        r = rb * NA_ROWS + u
        rs = jnp.clip(r - NA_ROWS // 2, 0, n_grid_rows - NA_ROWS)
        off = pl.multiple_of((rs - (rb - 1) * NA_ROWS) * GRID_W, GRID_W)
        ro0 = rs - r + NA_ROWS - 1
        kw = kbuf_ref[pl.ds(off, win)]
        vw = vbuf_ref[pl.ds(off, win)]
        q = q_ref[0, u * GRID_W:(u + 1) * GRID_W].astype(F32)
        qs = jnp.concatenate([jnp.where(lo, q, 0.0), jnp.where(lo, 0.0, q)], axis=0).astype(BF16)
        s = lax.dot_general(qs, kw, NT_DIMS, preferred_element_type=F32)
        bias = jnp.concatenate(
            [jnp.concatenate([bias_ref[h, ro0 + 2 * t] for t in range(NA_ROWS // 2)], axis=1)
             for h in range(2)], axis=0)
        s = s + bias
        m = jnp.max(s, axis=-1, keepdims=True)
        p = jnp.exp(s - m)
        l = jnp.sum(p, axis=-1, keepdims=True)
        o = jnp.dot(p.astype(BF16), vw, preferred_element_type=F32) / l
        o_ref[0, u * GRID_W:(u + 1) * GRID_W] = jnp.where(lo, o[0:GRID_W], o[GRID_W:]).astype(BF16)


def _na_attention(heads, bias, batch, seq):
    n_tok = heads.shape[1]
    n_grid_rows = seq // GRID_W
    nrb = n_grid_rows // NA_ROWS
    blk = NA_ROWS * GRID_W

    def kv_spec(first_pair, shift):
        return pl.BlockSpec(
            (1, blk, LANES),
            lambda b, p, rb: (first_pair + p, b * nrb + jnp.clip(rb + shift, 0, nrb - 1), 0))

    body = functools.partial(_na_attn_kernel, n_grid_rows=n_grid_rows)
    return pl.pallas_call(
        body,
        out_shape=jax.ShapeDtypeStruct((N_MIX_PAIRS, n_tok, LANES), BF16),
        grid=(batch, N_MIX_PAIRS, nrb),
        in_specs=[
            pl.BlockSpec((1, blk, LANES), lambda b, p, rb: (p, b * nrb + rb, 0)),
            kv_spec(N_MIX_PAIRS, -1), kv_spec(N_MIX_PAIRS, 0), kv_spec(N_MIX_PAIRS, 1),
            kv_spec(2 * N_MIX_PAIRS, -1), kv_spec(2 * N_MIX_PAIRS, 0), kv_spec(2 * N_MIX_PAIRS, 1),
            pl.BlockSpec((2,) + bias.shape[1:], lambda b, p, rb: (p, 0, 0, 0)),
        ],
        out_specs=pl.BlockSpec((1, blk, LANES), lambda b, p, rb: (p, b * nrb + rb, 0)),
        scratch_shapes=[pltpu.VMEM((3 * blk, LANES), BF16), pltpu.VMEM((3 * blk, LANES), BF16)],
        compiler_params=_cparams("parallel", "parallel", "arbitrary"),
        name="neighbourhood_attention",
    )(heads, heads, heads, heads, heads, heads, heads, bias)


def _na_bias_table(rpb):
    col = jnp.arange(GRID_W)
    col_start = jnp.clip(col - NA_COLS // 2, 0, GRID_W - NA_COLS)
    col_mask = (col[None, :] >= col_start[:, None]) & (col[None, :] < col_start[:, None] + NA_COLS)
    col_bias_idx = jnp.clip(col[None, :] - col[:, None] + NA_COLS - 1, 0, 2 * NA_COLS - 2)
    tiles = rpb.astype(F32)[:, :, col_bias_idx]
    tiles = jnp.where(col_mask[None, None], tiles, NEG_BIG)
    return jnp.concatenate([tiles[:, :-1], tiles[:, 1:]], axis=-1)


def _outproj_kernel(mix_ref, qm_ref, kv_ref, x_ref, w_ref, g_ref, b_ref, o_ref, *, alpha):
    tm = x_ref.shape[0]
    lo = _lane_iota((tm, LANES)) < HEAD_DIM
    parts = [mix_ref[p] for p in range(mix_ref.shape[0])]
    kv = kv_ref[0]
    for t in range(N_MEM_PAIRS):
        q = qm_ref[t].astype(F32)
        qs = jnp.concatenate([jnp.where(lo, q, 0.0), jnp.where(lo, 0.0, q)], axis=0).astype(BF16)
        kt = kv[:, t * LANES:(t + 1) * LANES]
        vt = kv[:, MEM_WIDTH + t * LANES:MEM_WIDTH + (t + 1) * LANES]
        s = lax.dot_general(qs, kt, NT_DIMS, preferred_element_type=F32)
        m = jnp.max(s, axis=-1, keepdims=True)
        p = jnp.exp(s - m)
        l = jnp.sum(p, axis=-1, keepdims=True)
        o = jnp.dot(p.astype(BF16), vt, preferred_element_type=F32) / l
        parts.append(jnp.where(lo, o[0:tm], o[tm:]).astype(BF16))
    attn = jnp.concatenate(parts, axis=1)
    h = jnp.dot(attn, w_ref[...], preferred_element_type=F32)
    o_ref[...] = _layer_norm(alpha * x_ref[...] + h, g_ref[...], b_ref[...])


def _out_projection(mix, heads, kv, x, w_out, g, b, batch, seq, tm, alpha):
    n_tok, d = x.shape
    nt = seq // tm
    n_pairs = heads.shape[0]
    n_mem = kv.shape[1]
    body = functools.partial(_outproj_kernel, alpha=alpha)
    return pl.pallas_call(
        body,
        out_shape=jax.ShapeDtypeStruct((n_tok, d), F32),
        grid=(batch, nt),
        in_specs=[
            pl.BlockSpec((N_MIX_PAIRS, tm, LANES), lambda bi, i: (0, bi * nt + i, 0)),
            pl.BlockSpec((N_MEM_PAIRS, tm, LANES),
                         lambda bi, i: (n_pairs // N_MEM_PAIRS - 1, bi * nt + i, 0)),
            pl.BlockSpec((1, n_mem, 2 * MEM_WIDTH), lambda bi, i: (bi, 0, 0)),
            pl.BlockSpec((tm, d), lambda bi, i: (bi * nt + i, 0)),
            pl.BlockSpec(w_out.shape, lambda bi, i: (0, 0)),
            pl.BlockSpec((1, d), lambda bi, i: (0, 0)),
            pl.BlockSpec((1, d), lambda bi, i: (0, 0)),
        ],
        out_specs=pl.BlockSpec((tm, d), lambda bi, i: (bi * nt + i, 0)),
        compiler_params=_cparams("parallel", "parallel"),
        name="mem_attention_out_projection",
    )(mix, heads, kv, x, w_out, g, b)


def _first_index_of_max(vals, index, sentinel, axes):
    mx = vals
    for ax in axes:
        mx = jnp.max(mx, axis=ax, keepdims=True)
    idx = jnp.where(vals == mx, index, sentinel)
    for ax in axes:
        idx = jnp.min(idx, axis=ax, keepdims=True)
    return mx, idx


def _router_gates(x, rth_ref, rtl_ref, rbias_ref):
    tm = x.shape[0]
    xh = x.astype(BF16)
    xl = (x - xh.astype(F32)).astype(BF16)
    rth = rth_ref[...]
    logits = (lax.dot_general(rth, xh, NT_DIMS, preferred_element_type=F32)
              + lax.dot_general(rth, xl, NT_DIMS, preferred_element_type=F32)
              + lax.dot_general(rtl_ref[...], xh, NT_DIMS, preferred_element_type=F32))
    scores = jax.nn.sigmoid(logits)
    choice = scores + rbias_ref[...]
    shape3 = (N_GROUPS, PER_GROUP, tm)
    c3 = choice.reshape(shape3)
    s3 = scores.reshape(shape3)
    e_in_g = lax.broadcasted_iota(jnp.int32, shape3, 1).astype(F32)
    g_idx = lax.broadcasted_iota(jnp.int32, (N_GROUPS, 1, tm), 0).astype(F32)
    e_idx = lax.broadcasted_iota(jnp.int32, shape3, 0).astype(F32) * PER_GROUP + e_in_g
    m1, i1 = _first_index_of_max(c3, e_in_g, float(PER_GROUP), (1,))
    m2 = jnp.max(jnp.where(e_in_g == i1, -jnp.inf, c3), axis=1, keepdims=True)
    cur = m1 + m2
    gsel = jnp.zeros(cur.shape, F32)
    for _ in range(TOPK_GROUPS):
        _, gi = _first_index_of_max(cur, g_idx, float(N_GROUPS), (0,))
        hit = g_idx == gi
        gsel = jnp.where(hit, 1.0, gsel)
        cur = jnp.where(hit, -jnp.inf, cur)
    cur = jnp.where(gsel > 0.0, c3, -jnp.inf)
    sel = jnp.zeros(shape3, F32)
    for _ in range(TOP_K):
        _, ei = _first_index_of_max(cur, e_idx, float(N_EXPERTS), (1, 0))
        hit = e_idx == ei
        sel = jnp.where(hit, 1.0, sel)
        cur = jnp.where(hit, -jnp.inf, cur)
    w = sel * s3
    denom = jnp.sum(jnp.sum(w, axis=1, keepdims=True), axis=0, keepdims=True)
    gates_t = (w / denom * ROUTED_SCALE).reshape(N_EXPERTS, tm)
    gates_t = jnp.concatenate([gates_t, jnp.zeros((LANES - N_EXPERTS, tm), F32)], axis=0)
    return gates_t.T, xh


def _moe_kernel(x_ref, rth_ref, rtl_ref, rbias_ref, eg_ref, eu_ref, ed_ref,
                sg_ref, su_ref, sd_ref, g_ref, b_ref, o_ref,
                xb_ref, gate_ref, acc_ref, *, alpha, chunk):
    c = pl.program_id(1)

    @pl.when(c == 0)
    def _():
        gates, xh = _router_gates(x_ref[...], rth_ref, rtl_ref, rbias_ref)
        gate_ref[...] = gates
        xb_ref[...] = xh
        hs = (jax.nn.silu(jnp.dot(xh, sg_ref[...], preferred_element_type=F32))
              * jnp.dot(xh, su_ref[...], preferred_element_type=F32))
        acc_ref[...] = jnp.dot(hs.astype(BF16), sd_ref[...], preferred_element_type=F32)

    xb = xb_ref[...]
    gates = gate_ref[...]
    lane = _lane_iota(gates.shape)
    total = acc_ref[...]
    for jj in range(chunk):
        e = c * chunk + jj
        gcol = jnp.sum(jnp.where(lane == e, gates, 0.0), axis=-1, keepdims=True)
        h = (jax.nn.silu(jnp.dot(xb, eg_ref[jj], preferred_element_type=F32))
             * jnp.dot(xb, eu_ref[jj], preferred_element_type=F32)) * gcol
        total = total + jnp.dot(h.astype(BF16), ed_ref[jj], preferred_element_type=F32)
    acc_ref[...] = total

    @pl.when(c == pl.num_programs(1) - 1)
    def _():
        o_ref[...] = _layer_norm(alpha * x_ref[...] + acc_ref[...], g_ref[...], b_ref[...])


def _moe(x, rth, rtl, rbias, eg, eu, ed, sg, su, sd, g, b, tm, chunk, alpha):
    n_tok, d = x.shape
    n_exp, _, f = eg.shape
    const2 = lambda i, c: (0, 0)
    body = functools.partial(_moe_kernel, alpha=alpha, chunk=chunk)
    return pl.pallas_call(
        body,
        out_shape=jax.ShapeDtypeStruct((n_tok, d), F32),
        grid=(n_tok // tm, n_exp // chunk),
        in_specs=[
            pl.BlockSpec((tm, d), lambda i, c: (i, 0)),
            pl.BlockSpec(rth.shape, const2),
            pl.BlockSpec(rtl.shape, const2),
            pl.BlockSpec(rbias.shape, const2),
            pl.BlockSpec((chunk, d, f), lambda i, c: (c, 0, 0)),
            pl.BlockSpec((chunk, d, f), lambda i, c: (c, 0, 0)),
            pl.BlockSpec((chunk, f, d), lambda i, c: (c, 0, 0)),
            pl.BlockSpec(sg.shape, const2),
            pl.BlockSpec(su.shape, const2),
            pl.BlockSpec(sd.shape, const2),
            pl.BlockSpec((1, d), const2),
            pl.BlockSpec((1, d), const2),
        ],
        out_specs=pl.BlockSpec((tm, d), lambda i, c: (i, 0)),
        scratch_shapes=[
            pltpu.VMEM((tm, d), BF16),
            pltpu.VMEM((tm, LANES), F32),
            pltpu.VMEM((tm, d), F32),
        ],
        compiler_params=_cparams("parallel", "arbitrary"),
        name="moe_ffn",
    )(x, rth, rtl, rbias, eg, eu, ed, sg, su, sd, g, b)


def _rope_tables(seq):
    t = jnp.arange(seq)
    row = (t // GRID_W).astype(F32)
    colp = (t % GRID_W).astype(F32)
    inv = ROPE_THETA ** (-jnp.arange(0, ROPE_AXIS_DIM, 2, dtype=F32) / ROPE_AXIS_DIM)
    ang = jnp.concatenate([row[:, None] * inv[None], colp[:, None] * inv[None]], axis=-1)
    ang = jnp.repeat(ang, 2, axis=-1)
    sign = jnp.where(jnp.arange(HEAD_DIM) % 2 == 0, -1.0, 1.0).astype(F32)
    cos = jnp.tile(jnp.cos(ang), (1, 2))
    sin = jnp.tile(jnp.sin(ang) * sign[None], (1, 2))
    return cos, sin


def _gqa_head_order():
    order = []
    for kvp in range(N_KV_HEADS // 2):
        base = 2 * GQA_GROUP * kvp
        for g in range(GQA_GROUP):
            order += [base + g, base + GQA_GROUP + g]
    return np.array(order)


def _pick_tile(n, target):
    t = min(n, target)
    while n % t:
        t //= 2
    return t


def kernel(x_prompt, x_sample, mem_prompt, mem_sample, a_w_in, a_lambda, a_subln, b_w_in, b_rpb,
           c_w_in, c_q_norm, c_k_norm, w_mem_kv, w_out, ln1_g, ln1_b, router, router_bias,
           e_gate, e_up, e_down, s_gate, s_up, s_down, ln2_g, ln2_b):
    depth = w_out.shape[0]
    alpha = (2.0 * depth) ** 0.25
    seq = x_prompt.shape[1]
    d = x_prompt.shape[2]
    assert x_sample.shape[1] == seq and seq % (NA_ROWS * GRID_W) == 0
    n_prompt = x_prompt.shape[0] * seq
    batch = x_prompt.shape[0] + x_sample.shape[0]
    x = jnp.concatenate([x_prompt.reshape(-1, d), x_sample.reshape(-1, d)], axis=0)
    mem = jnp.concatenate([mem_prompt, mem_sample], axis=0)
    n_mem = mem.shape[1]
    mem2d = mem.reshape(batch * n_mem, d)

    tm = _pick_tile(seq, 512)
    tq = _pick_tile(seq, 256)
    tk = _pick_tile(seq, 512)
    head_order = _gqa_head_order()
    col_order = (head_order[:, None] * HEAD_DIM + np.arange(HEAD_DIM)[None]).reshape(-1)
    cos, sin = _rope_tables(seq)
    slopes = _alibi_slopes(N_MIX_HEADS).reshape(N_MIX_PAIRS, 2)
    slope_tab = np.zeros((N_MIX_PAIRS, 8, LANES), np.float32)
    slope_tab[:, 0:2, :] = slopes[:, :, None]
    slope_tab = jnp.asarray(slope_tab)
    ones_row = functools.partial(jnp.ones, dtype=F32)

    for i in range(depth):
        mixer, occ = i % N_MIXERS, i // N_MIXERS
        w_o = w_out[i]
        if mixer == 0:
            w_in = a_w_in[occ]
            colscale = jnp.concatenate([
                jnp.full((MIX_WIDTH,), DIFF_QK_DIM ** -0.5, F32), ones_row((2 * MIX_WIDTH,)),
                jnp.full((MEM_WIDTH,), HEAD_DIM ** -0.5, F32)])
            heads = _in_projection(x, w_in.astype(BF16), colscale[None], tm, seq)
            gain = jnp.tile(a_subln[occ].astype(F32), 2)[None]
            mix = _diff_attention(heads, slope_tab, a_lambda[occ].astype(F32), gain, batch, seq,
                                  tq, tk, _diff_lambda_init(i))
        elif mixer == 1:
            w_in = b_w_in[occ]
            colscale = jnp.concatenate([
                jnp.full((MIX_WIDTH,), HEAD_DIM ** -0.5, F32), ones_row((2 * MIX_WIDTH,)),
                jnp.full((MEM_WIDTH,), HEAD_DIM ** -0.5, F32)])
            heads = _in_projection(x, w_in.astype(BF16), colscale[None], tm, seq)
            mix = _na_attention(heads, _na_bias_table(b_rpb[occ]), batch, seq)
        else:
            w_in = c_w_in[occ]
            kvw = N_KV_HEADS * HEAD_DIM
            w_in = jnp.concatenate([w_in[:, :MIX_WIDTH][:, col_order], w_in[:, MIX_WIDTH:]], axis=1)
            w_o = jnp.concatenate([w_o[:MIX_WIDTH][col_order], w_o[MIX_WIDTH:]], axis=0)
            colscale = jnp.concatenate([
                jnp.full((MIX_WIDTH,), HEAD_DIM ** -0.5, F32), ones_row((2 * kvw,)),
                jnp.full((MEM_WIDTH,), HEAD_DIM ** -0.5, F32)])
            qg = jnp.tile(c_q_norm[occ].astype(F32), 2)[None, None]
            kg = jnp.tile(c_k_norm[occ].astype(F32), 2)[None, None]
            gains = jnp.concatenate([jnp.tile(qg, (N_MIX_PAIRS, 1, 1)),
                                     jnp.tile(kg, (N_KV_HEADS // 2, 1, 1))], axis=0)
            heads = _in_projection(x, w_in.astype(BF16), colscale[None], tm, seq,
                                   rope=(gains, cos, sin))
            mix = _gqa_attention(heads, batch, seq, _pick_tile(seq, 128), tk)
        kv = _matmul(mem2d, w_mem_kv[i].astype(BF16), _pick_tile(batch * n_mem, 512))
        kv = kv.reshape(batch, n_mem, 2 * MEM_WIDTH)
        x = _out_projection(mix, heads, kv, x, w_o.astype(BF16), ln1_g[i][None], ln1_b[i][None],
                            batch, seq, tm, alpha)
        rt = router[i].T.astype(F32)
        rth = rt.astype(BF16)
        rtl = (rt - rth.astype(F32)).astype(BF16)
        x = _moe(x, rth, rtl, router_bias[i].astype(F32)[:, None],
                 e_gate[i].astype(BF16), e_up[i].astype(BF16), e_down[i].astype(BF16),
                 s_gate[i].astype(BF16), s_up[i].astype(BF16), s_down[i].astype(BF16),
                 ln2_g[i][None], ln2_b[i][None], tm, 4, alpha)

    y = x.reshape(batch, seq, d)
    n_pb = x_prompt.shape[0]
    return (y[:n_pb], y[n_pb:])
```

```python
import functools
import math

import numpy as np
import jax
import jax.numpy as jnp
from jax import lax
from jax.experimental import pallas as pl
from jax.experimental.pallas import tpu as pltpu

F32 = jnp.float32
BF16 = jnp.bfloat16

D_MODEL = 1024
HEAD_DIM = 64
LANES = 128
N_MIX_HEADS = 12
N_MIX_PAIRS = N_MIX_HEADS // 2
N_MEM_HEADS = 4
N_MEM_PAIRS = N_MEM_HEADS // 2
MIX_WIDTH = N_MIX_HEADS * HEAD_DIM
MEM_WIDTH = N_MEM_HEADS * HEAD_DIM
N_MIXERS = 3
DIFF_QK_DIM = HEAD_DIM // 2
GRID_W = 64
NA_ROWS = 8
NA_COLS = 16
N_KV_HEADS = 4
GQA_GROUP = N_MIX_HEADS // N_KV_HEADS
ROPE_THETA = 10000.0
ROPE_AXIS_DIM = HEAD_DIM // 2
N_EXPERTS = 64
TOP_K = 8
N_GROUPS = 8
TOPK_GROUPS = 4
PER_GROUP = N_EXPERTS // N_GROUPS
D_EXPERT = 256
ROUTED_SCALE = 2.5
LN_EPS = 1e-5
RMS_EPS = 1e-6
NEG_BIG = -1e30
LOG2E = math.log2(math.e)

VMEM_LIMIT = 48 * 1024 * 1024

NT_DIMS = (((1,), (1,)), ((), ()))
TN_DIMS = (((0,), (0,)), ((), ()))


def _cparams(*sem):
    return pltpu.CompilerParams(dimension_semantics=sem, vmem_limit_bytes=VMEM_LIMIT)


def _alibi_slopes(n):
    def pow2_slopes(m):
        start = 2.0 ** (-8.0 / m)
        return [start ** (i + 1) for i in range(m)]
    if math.log2(n).is_integer():
        s = pow2_slopes(n)
    else:
        c = 2 ** math.floor(math.log2(n))
        s = pow2_slopes(c) + pow2_slopes(2 * c)[0::2][: n - c]
    return np.array(s, np.float32)


def _diff_lambda_init(layer_idx):
    return 0.8 - 0.6 * math.exp(-0.3 * layer_idx)


def _lane_iota(shape):
    return lax.broadcasted_iota(jnp.int32, shape, len(shape) - 1)


def _layer_norm(y, g, b):
    mu = jnp.mean(y, axis=-1, keepdims=True)
    yc = y - mu
    var = jnp.mean(yc * yc, axis=-1, keepdims=True)
    return yc * lax.rsqrt(var + LN_EPS) * g + b


def _head_rms_scale(x):
    lo = _lane_iota(x.shape) < HEAD_DIM
    xx = x * x
    ss_lo = jnp.sum(jnp.where(lo, xx, 0.0), axis=-1, keepdims=True)
    ss_hi = jnp.sum(jnp.where(lo, 0.0, xx), axis=-1, keepdims=True)
    inv = 1.0 / HEAD_DIM
    return jnp.where(lo, lax.rsqrt(ss_lo * inv + RMS_EPS), lax.rsqrt(ss_hi * inv + RMS_EPS))


def _inproj_kernel(x_ref, w_ref, cs_ref, o_ref):
    r = jnp.dot(x_ref[...].astype(BF16), w_ref[...], preferred_element_type=F32)
    r = r * cs_ref[...]
    for p in range(o_ref.shape[0]):
        o_ref[p] = r[:, p * LANES:(p + 1) * LANES].astype(BF16)


def _inproj_rope_kernel(x_ref, w_ref, cs_ref, g_ref, cos_ref, sin_ref, o_ref, *, n_rope_pairs):
    r = jnp.dot(x_ref[...].astype(BF16), w_ref[...], preferred_element_type=F32)
    cs = cs_ref[...]
    cos = cos_ref[...]
    sin = sin_ref[...]
    even = (_lane_iota(cos.shape) % 2) == 0
    for p in range(o_ref.shape[0]):
        xp = r[:, p * LANES:(p + 1) * LANES]
        if p < n_rope_pairs:
            xn = xp * _head_rms_scale(xp) * g_ref[p]
            partner = jnp.where(even, pltpu.roll(xn, LANES - 1, 1), pltpu.roll(xn, 1, 1))
            xp = xn * cos + partner * sin
        o_ref[p] = (xp * cs[:, p * LANES:(p + 1) * LANES]).astype(BF16)


def _in_projection(x, w, colscale, tm, seq, rope=None):
    n_tok, d = x.shape
    width = w.shape[1]
    n_pairs = width // LANES
    in_specs = [
        pl.BlockSpec((tm, d), lambda i: (i, 0)),
        pl.BlockSpec((d, width), lambda i: (0, 0)),
        pl.BlockSpec((1, width), lambda i: (0, 0)),
    ]
    args = [x, w, colscale]
    if rope is None:
        body = _inproj_kernel
    else:
        gains, cos, sin = rope
        tiles_per_seq = seq // tm
        body = functools.partial(_inproj_rope_kernel, n_rope_pairs=gains.shape[0])
        in_specs += [
            pl.BlockSpec(gains.shape, lambda i: (0, 0, 0)),
            pl.BlockSpec((tm, LANES), lambda i: (i % tiles_per_seq, 0)),
            pl.BlockSpec((tm, LANES), lambda i: (i % tiles_per_seq, 0)),
        ]
        args += [gains, cos, sin]
    return pl.pallas_call(
        body,
        out_shape=jax.ShapeDtypeStruct((n_pairs, n_tok, LANES), BF16),
        grid=(n_tok // tm,),
        in_specs=in_specs,
        out_specs=pl.BlockSpec((n_pairs, tm, LANES), lambda i: (0, i, 0)),
        compiler_params=_cparams("parallel"),
        name="in_projection",
    )(*args)


def _matmul_kernel(x_ref, w_ref, o_ref):
    o_ref[...] = jnp.dot(x_ref[...].astype(BF16), w_ref[...],
                         preferred_element_type=F32).astype(o_ref.dtype)


def _matmul(x, w, tm):
    m, k = x.shape
    n = w.shape[1]
    return pl.pallas_call(
        _matmul_kernel,
        out_shape=jax.ShapeDtypeStruct((m, n), BF16),
        grid=(m // tm,),
        in_specs=[pl.BlockSpec((tm, k), lambda i: (i, 0)), pl.BlockSpec((k, n), lambda i: (0, 0))],
        out_specs=pl.BlockSpec((tm, n), lambda i: (i, 0)),
        compiler_params=_cparams("parallel"),
        name="mem_kv_projection",
    )(x, w)


FLASH_UNROLL = 2


def _stack_masked_q(q_ref, qs_ref, tq):
    n_qpairs = q_ref.shape[0]
    per_pair = qs_ref.shape[0] // tq // n_qpairs
    lanes_per_map = LANES // per_pair
    for t in range(n_qpairs):
        q = q_ref[t].astype(F32)
        part = _lane_iota(q.shape) // lanes_per_map
        for a in range(per_pair):
            r0 = (t * per_pair + a) * tq
            qs_ref[r0:r0 + tq] = jnp.where(part == a, q, 0.0).astype(BF16)


def _softmax_accumulate(s, v, m, l, acc_ref, col_shift=None):
    cmax = jnp.max(s, axis=0, keepdims=True)
    if col_shift is None:
        m_new = jnp.maximum(m, cmax)
        m_sub = m_new
    else:
        m_new = jnp.maximum(m, cmax + col_shift)
        m_sub = m_new - col_shift
    alpha = jnp.exp2(m - m_new)
    p = jnp.exp2(s - m_sub)
    l_new = alpha * l + jnp.sum(p, axis=0, keepdims=True)
    pv = lax.dot_general(v, p.astype(BF16), TN_DIMS, preferred_element_type=F32)
    acc_ref[...] = alpha * acc_ref[...] + pv
    return m_new, l_new


def _diff_attn_kernel(q_ref, k_ref, v_ref, slope_ref, lam_ref, g_ref, o_ref, qs_ref, acc_ref,
                      *, tq, tk, seq, lambda_init):
    i = pl.program_id(2)
    n_maps = 4
    w = n_maps * tq
    _stack_masked_q(q_ref, qs_ref, tq)
    acc_ref[...] = jnp.zeros(acc_ref.shape, F32)
    slopes = slope_ref[0]
    jd = (i * tq) // tk

    def scores(j):
        off = pl.multiple_of(j * tk, tk)
        k = k_ref[0, pl.ds(off, tk), :]
        v = v_ref[0, pl.ds(off, tk), :]
        return lax.dot_general(k, qs_ref[...], NT_DIMS, preferred_element_type=F32), v

    s, v = scores(jd)
    rel = (lax.broadcasted_iota(jnp.int32, (tk, tq), 0) + (jd * tk - i * tq)
           - lax.broadcasted_iota(jnp.int32, (tk, tq), 1))
    dist = jnp.abs(rel).astype(F32)
    s = jnp.concatenate(
        [s[:, a * tq:(a + 1) * tq] - slopes[a // 2:a // 2 + 1, 0:1] * dist for a in range(n_maps)],
        axis=1)
    carry = _softmax_accumulate(s, v, jnp.full((1, w), -jnp.inf, F32), jnp.zeros((1, w), F32),
                                acc_ref)

    key_iota = lax.broadcasted_iota(jnp.int32, (tk, LANES), 0)
    qpos = lax.broadcasted_iota(jnp.int32, (1, tq), 1).astype(F32)

    def chunk(j, carry):
        m, l = carry
        s, v = scores(j)
        sgn = jnp.where(j < jd, 1.0, -1.0)
        skip = jnp.where(j == jd, -jnp.inf, 0.0)
        kpos = (key_iota + (j * tk - i * tq)).astype(F32)
        cols, shifts = [], []
        for h in range(2):
            sl = slopes[h:h + 1, 0:1] * sgn
            key_term = jnp.tile(sl * kpos + skip, (1, tq // LANES))
            q_term = -sl * qpos
            for c in range(2):
                a = 2 * h + c
                cols.append(s[:, a * tq:(a + 1) * tq] + key_term)
                shifts.append(q_term)
        return _softmax_accumulate(jnp.concatenate(cols, axis=1), v, m, l, acc_ref,
                                   col_shift=jnp.concatenate(shifts, axis=1))

    m, l = lax.fori_loop(0, seq // tk, chunk, carry, unroll=FLASH_UNROLL)

    o = acc_ref[...] / l
    lam = lam_ref[...]
    lam_full = (jnp.exp(jnp.sum(lam[0:1] * lam[1:2], axis=-1, keepdims=True))
                - jnp.exp(jnp.sum(lam[2:3] * lam[3:4], axis=-1, keepdims=True)) + lambda_init)
    h0 = o[:, 0:tq] - lam_full * o[:, tq:2 * tq]
    h1 = o[:, 2 * tq:3 * tq] - lam_full * o[:, 3 * tq:4 * tq]
    row = lax.broadcasted_iota(jnp.int32, (LANES, tq), 0)
    o = jnp.where(row < HEAD_DIM, h0, h1).T
    o = o * _head_rms_scale(o) * g_ref[...] * (1.0 - lambda_init)
    o_ref[0] = o.astype(BF16)


def _diff_attention(heads, slopes, lam, gain, batch, seq, tq, tk, lambda_init):
    assert tk % tq == 0 and tq % LANES == 0
    n_tok = heads.shape[1]
    nq = seq // tq
    body = functools.partial(_diff_attn_kernel, tq=tq, tk=tk, seq=seq, lambda_init=lambda_init)
    return pl.pallas_call(
        body,
        out_shape=jax.ShapeDtypeStruct((N_MIX_PAIRS, n_tok, LANES), BF16),
        grid=(batch, N_MIX_PAIRS, nq),
        in_specs=[
            pl.BlockSpec((1, tq, LANES), lambda b, p, i: (p, b * nq + i, 0)),
            pl.BlockSpec((1, seq, LANES), lambda b, p, i: (N_MIX_PAIRS + p, b, 0)),
            pl.BlockSpec((1, seq, LANES), lambda b, p, i: (2 * N_MIX_PAIRS + p, b, 0)),
            pl.BlockSpec((1, 8, LANES), lambda b, p, i: (p, 0, 0)),
            pl.BlockSpec(lam.shape, lambda b, p, i: (0, 0)),
            pl.BlockSpec((1, LANES), lambda b, p, i: (0, 0)),
        ],
        out_specs=pl.BlockSpec((1, tq, LANES), lambda b, p, i: (p, b * nq + i, 0)),
        scratch_shapes=[
            pltpu.VMEM((4 * tq, LANES), BF16),
            pltpu.VMEM((LANES, 4 * tq), F32),
        ],
        compiler_params=_cparams("parallel", "parallel", "parallel"),
        name="diff_attention",
    )(heads, heads, heads, slopes, lam, gain)


def _gqa_attn_kernel(q_ref, k_ref, v_ref, o_ref, qs_ref, acc_ref, *, tq, tk, seq):
    n_qpairs = q_ref.shape[0]
    w = 2 * n_qpairs * tq
    _stack_masked_q(q_ref, qs_ref, tq)
    acc_ref[...] = jnp.zeros(acc_ref.shape, F32)

    def chunk(j, carry):
        m, l = carry
        off = pl.multiple_of(j * tk, tk)
        k = k_ref[0, pl.ds(off, tk), :]
        v = v_ref[0, pl.ds(off, tk), :]
        s = lax.dot_general(k, qs_ref[...], NT_DIMS, preferred_element_type=F32)
        return _softmax_accumulate(s, v, m, l, acc_ref)

    carry = (jnp.full((1, w), -jnp.inf, F32), jnp.zeros((1, w), F32))
    m, l = lax.fori_loop(0, seq // tk, chunk, carry, unroll=FLASH_UNROLL)
    o = acc_ref[...] / l
    row = lax.broadcasted_iota(jnp.int32, (LANES, tq), 0)
    for t in range(n_qpairs):
        lo = o[:, (2 * t) * tq:(2 * t + 1) * tq]
        hi = o[:, (2 * t + 1) * tq:(2 * t + 2) * tq]
        o_ref[t] = jnp.where(row < HEAD_DIM, lo, hi).T.astype(BF16)


def _gqa_attention(heads, batch, seq, tq, tk):
    n_tok = heads.shape[1]
    nq = seq // tq
    n_kv_pairs = N_KV_HEADS // 2
    qp = N_MIX_PAIRS // n_kv_pairs
    body = functools.partial(_gqa_attn_kernel, tq=tq, tk=tk, seq=seq)
    return pl.pallas_call(
        body,
        out_shape=jax.ShapeDtypeStruct((N_MIX_PAIRS, n_tok, LANES), BF16),
        grid=(batch, n_kv_pairs, nq),
        in_specs=[
            pl.BlockSpec((qp, tq, LANES), lambda b, p, i: (p, b * nq + i, 0)),
            pl.BlockSpec((1, seq, LANES), lambda b, p, i: (N_MIX_PAIRS + p, b, 0)),
            pl.BlockSpec((1, seq, LANES), lambda b, p, i: (N_MIX_PAIRS + n_kv_pairs + p, b, 0)),
        ],
        out_specs=pl.BlockSpec((qp, tq, LANES), lambda b, p, i: (p, b * nq + i, 0)),
        scratch_shapes=[
            pltpu.VMEM((2 * qp * tq, LANES), BF16),
            pltpu.VMEM((LANES, 2 * qp * tq), F32),
        ],
        compiler_params=_cparams("parallel", "parallel", "parallel"),
        name="gqa_attention",
    )(heads, heads, heads)


def _na_attn_kernel(q_ref, kp_ref, kc_ref, kn_ref, vp_ref, vc_ref, vn_ref, bias_ref, o_ref,
                    kbuf_ref, vbuf_ref, *, n_grid_rows):
    rb = pl.program_id(2)
    blk = NA_ROWS * GRID_W
    win = NA_ROWS * GRID_W
    kbuf_ref[0:blk] = kp_ref[0]
    kbuf_ref[blk:2 * blk] = kc_ref[0]
    kbuf_ref[2 * blk:3 * blk] = kn_ref[0]
    vbuf_ref[0:blk] = vp_ref[0]
    vbuf_ref[blk:2 * blk] = vc_ref[0]
    vbuf_ref[2 * blk:3 * blk] = vn_ref[0]
    lo = _lane_iota((GRID_W, LANES)) < HEAD_DIM
    for u in range(NA_ROWS):
        r = rb * NA_ROWS + u
        rs = jnp.clip(r - NA_ROWS // 2, 0, n_grid_rows - NA_ROWS)
        off = pl.multiple_of((rs - (rb - 1) * NA_ROWS) * GRID_W, GRID_W)
        ro0 = rs - r + NA_ROWS - 1
        kw = kbuf_ref[pl.ds(off, win)]
        vw = vbuf_ref[pl.ds(off, win)]
        q = q_ref[0, u * GRID_W:(u + 1) * GRID_W].astype(F32)
        qs = jnp.concatenate([jnp.where(lo, q, 0.0), jnp.where(lo, 0.0, q)], axis=0).astype(BF16)
        s = lax.dot_general(qs, kw, NT_DIMS, preferred_element_type=F32)
        bias = jnp.concatenate(
            [jnp.concatenate([bias_ref[h, ro0 + 2 * t] for t in range(NA_ROWS // 2)], axis=1)
             for h in range(2)], axis=0)
        s = s + bias
        m = jnp.max(s, axis=-1, keepdims=True)
        p = jnp.exp(s - m)
        l = jnp.sum(p, axis=-1, keepdims=True)
        o = jnp.dot(p.astype(BF16), vw, preferred_element_type=F32) / l
        o_ref[0, u * GRID_W:(u + 1) * GRID_W] = jnp.where(lo, o[0:GRID_W], o[GRID_W:]).astype(BF16)


def _na_attention(heads, bias, batch, seq):
    n_tok = heads.shape[1]
    n_grid_rows = seq // GRID_W
    nrb = n_grid_rows // NA_ROWS
    blk = NA_ROWS * GRID_W

    def kv_spec(first_pair, shift):
        return pl.BlockSpec(
            (1, blk, LANES),
            lambda b, p, rb: (first_pair + p, b * nrb + jnp.clip(rb + shift, 0, nrb - 1), 0))

    body = functools.partial(_na_attn_kernel, n_grid_rows=n_grid_rows)
    return pl.pallas_call(
        body,
        out_shape=jax.ShapeDtypeStruct((N_MIX_PAIRS, n_tok, LANES), BF16),
        grid=(batch, N_MIX_PAIRS, nrb),
        in_specs=[
            pl.BlockSpec((1, blk, LANES), lambda b, p, rb: (p, b * nrb + rb, 0)),
            kv_spec(N_MIX_PAIRS, -1), kv_spec(N_MIX_PAIRS, 0), kv_spec(N_MIX_PAIRS, 1),
            kv_spec(2 * N_MIX_PAIRS, -1), kv_spec(2 * N_MIX_PAIRS, 0), kv_spec(2 * N_MIX_PAIRS, 1),
            pl.BlockSpec((2,) + bias.shape[1:], lambda b, p, rb: (p, 0, 0, 0)),
        ],
        out_specs=pl.BlockSpec((1, blk, LANES), lambda b, p, rb: (p, b * nrb + rb, 0)),
        scratch_shapes=[pltpu.VMEM((3 * blk, LANES), BF16), pltpu.VMEM((3 * blk, LANES), BF16)],
        compiler_params=_cparams("parallel", "parallel", "arbitrary"),
        name="neighbourhood_attention",
    )(heads, heads, heads, heads, heads, heads, heads, bias)


def _na_bias_table(rpb):
    col = jnp.arange(GRID_W)
    col_start = jnp.clip(col - NA_COLS // 2, 0, GRID_W - NA_COLS)
    col_mask = (col[None, :] >= col_start[:, None]) & (col[None, :] < col_start[:, None] + NA_COLS)
    col_bias_idx = jnp.clip(col[None, :] - col[:, None] + NA_COLS - 1, 0, 2 * NA_COLS - 2)
    tiles = rpb.astype(F32)[:, :, col_bias_idx]
    tiles = jnp.where(col_mask[None, None], tiles, NEG_BIG)
    return jnp.concatenate([tiles[:, :-1], tiles[:, 1:]], axis=-1)


def _outproj_kernel(mix_ref, qm_ref, kv_ref, x_ref, w_ref, g_ref, b_ref, o_ref, *, alpha):
    tm = x_ref.shape[0]
    lo = _lane_iota((tm, LANES)) < HEAD_DIM
    parts = [mix_ref[p] for p in range(mix_ref.shape[0])]
    kv = kv_ref[0]
    for t in range(N_MEM_PAIRS):
        q = qm_ref[t].astype(F32)
        qs = jnp.concatenate([jnp.where(lo, q, 0.0), jnp.where(lo, 0.0, q)], axis=0).astype(BF16)
        kt = kv[:, t * LANES:(t + 1) * LANES]
        vt = kv[:, MEM_WIDTH + t * LANES:MEM_WIDTH + (t + 1) * LANES]
        s = lax.dot_general(qs, kt, NT_DIMS, preferred_element_type=F32)
        m = jnp.max(s, axis=-1, keepdims=True)
        p = jnp.exp(s - m)
        l = jnp.sum(p, axis=-1, keepdims=True)
        o = jnp.dot(p.astype(BF16), vt, preferred_element_type=F32) / l
        parts.append(jnp.where(lo, o[0:tm], o[tm:]).astype(BF16))
    attn = jnp.concatenate(parts, axis=1)
    h = jnp.dot(attn, w_ref[...], preferred_element_type=F32)
    o_ref[...] = _layer_norm(alpha * x_ref[...] + h, g_ref[...], b_ref[...])


def _out_projection(mix, heads, kv, x, w_out, g, b, batch, seq, tm, alpha):
    n_tok, d = x.shape
    nt = seq // tm
    n_pairs = heads.shape[0]
    n_mem = kv.shape[1]
    body = functools.partial(_outproj_kernel, alpha=alpha)
    return pl.pallas_call(
        body,
        out_shape=jax.ShapeDtypeStruct((n_tok, d), F32),
        grid=(batch, nt),
        in_specs=[
            pl.BlockSpec((N_MIX_PAIRS, tm, LANES), lambda bi, i: (0, bi * nt + i, 0)),
            pl.BlockSpec((N_MEM_PAIRS, tm, LANES),
                         lambda bi, i: (n_pairs // N_MEM_PAIRS - 1, bi * nt + i, 0)),
            pl.BlockSpec((1, n_mem, 2 * MEM_WIDTH), lambda bi, i: (bi, 0, 0)),
            pl.BlockSpec((tm, d), lambda bi, i: (bi * nt + i, 0)),
            pl.BlockSpec(w_out.shape, lambda bi, i: (0, 0)),
            pl.BlockSpec((1, d), lambda bi, i: (0, 0)),
            pl.BlockSpec((1, d), lambda bi, i: (0, 0)),
        ],
        out_specs=pl.BlockSpec((tm, d), lambda bi, i: (bi * nt + i, 0)),
        compiler_params=_cparams("parallel", "parallel"),
        name="mem_attention_out_projection",
    )(mix, heads, kv, x, w_out, g, b)


def _first_index_of_max(vals, index, sentinel, axes):
    mx = vals
    for ax in axes:
        mx = jnp.max(mx, axis=ax, keepdims=True)
    idx = jnp.where(vals == mx, index, sentinel)
    for ax in axes:
        idx = jnp.min(idx, axis=ax, keepdims=True)
    return mx, idx


def _router_gates(x, rth_ref, rtl_ref, rbias_ref):
    tm = x.shape[0]
    xh = x.astype(BF16)
    xl = (x - xh.astype(F32)).astype(BF16)
    rth = rth_ref[...]
    logits = (lax.dot_general(rth, xh, NT_DIMS, preferred_element_type=F32)
              + lax.dot_general(rth, xl, NT_DIMS, preferred_element_type=F32)
              + lax.dot_general(rtl_ref[...], xh, NT_DIMS, preferred_element_type=F32))
    scores = jax.nn.sigmoid(logits)
    choice = scores + rbias_ref[...]
    shape3 = (N_GROUPS, PER_GROUP, tm)
    c3 = choice.reshape(shape3)
    s3 = scores.reshape(shape3)
    e_in_g = lax.broadcasted_iota(jnp.int32, shape3, 1).astype(F32)
    g_idx = lax.broadcasted_iota(jnp.int32, (N_GROUPS, 1, tm), 0).astype(F32)
    e_idx = lax.broadcasted_iota(jnp.int32, shape3, 0).astype(F32) * PER_GROUP + e_in_g
    m1, i1 = _first_index_of_max(c3, e_in_g, float(PER_GROUP), (1,))
    m2 = jnp.max(jnp.where(e_in_g == i1, -jnp.inf, c3), axis=1, keepdims=True)
    cur = m1 + m2
    gsel = jnp.zeros(cur.shape, F32)
    for _ in range(TOPK_GROUPS):
        _, gi = _first_index_of_max(cur, g_idx, float(N_GROUPS), (0,))
        hit = g_idx == gi
        gsel = jnp.where(hit, 1.0, gsel)
        cur = jnp.where(hit, -jnp.inf, cur)
    cur = jnp.where(gsel > 0.0, c3, -jnp.inf)
    sel = jnp.zeros(shape3, F32)
    for _ in range(TOP_K):
        _, ei = _first_index_of_max(cur, e_idx, float(N_EXPERTS), (1, 0))
        hit = e_idx == ei
        sel = jnp.where(hit, 1.0, sel)
        cur = jnp.where(hit, -jnp.inf, cur)
    w = sel * s3
    denom = jnp.sum(jnp.sum(w, axis=1, keepdims=True), axis=0, keepdims=True)
    gates_t = (w / denom * ROUTED_SCALE).reshape(N_EXPERTS, tm)
    gates_t = jnp.concatenate([gates_t, jnp.zeros((LANES - N_EXPERTS, tm), F32)], axis=0)
    return gates_t.T, xh


def _moe_kernel(x_ref, rth_ref, rtl_ref, rbias_ref, eg_ref, eu_ref, ed_ref,
                sg_ref, su_ref, sd_ref, g_ref, b_ref, o_ref,
                xb_ref, gate_ref, acc_ref, *, alpha, chunk):
    c = pl.program_id(1)

    @pl.when(c == 0)
    def _():
        gates, xh = _router_gates(x_ref[...], rth_ref, rtl_ref, rbias_ref)
        gate_ref[...] = gates
        xb_ref[...] = xh
        hs = (jax.nn.silu(jnp.dot(xh, sg_ref[...], preferred_element_type=F32))
              * jnp.dot(xh, su_ref[...], preferred_element_type=F32))
        acc_ref[...] = jnp.dot(hs.astype(BF16), sd_ref[...], preferred_element_type=F32)

    xb = xb_ref[...]
    gates = gate_ref[...]
    lane = _lane_iota(gates.shape)
    total = acc_ref[...]
    for jj in range(chunk):
        e = c * chunk + jj
        gcol = jnp.sum(jnp.where(lane == e, gates, 0.0), axis=-1, keepdims=True)
        h = (jax.nn.silu(jnp.dot(xb, eg_ref[jj], preferred_element_type=F32))
             * jnp.dot(xb, eu_ref[jj], preferred_element_type=F32)) * gcol
        total = total + jnp.dot(h.astype(BF16), ed_ref[jj], preferred_element_type=F32)
    acc_ref[...] = total

    @pl.when(c == pl.num_programs(1) - 1)
    def _():
        o_ref[...] = _layer_norm(alpha * x_ref[...] + acc_ref[...], g_ref[...], b_ref[...])


def _moe(x, rth, rtl, rbias, eg, eu, ed, sg, su, sd, g, b, tm, chunk, alpha):
    n_tok, d = x.shape
    n_exp, _, f = eg.shape
    const2 = lambda i, c: (0, 0)
    body = functools.partial(_moe_kernel, alpha=alpha, chunk=chunk)
    return pl.pallas_call(
        body,
        out_shape=jax.ShapeDtypeStruct((n_tok, d), F32),
        grid=(n_tok // tm, n_exp // chunk),
        in_specs=[
            pl.BlockSpec((tm, d), lambda i, c: (i, 0)),
            pl.BlockSpec(rth.shape, const2),
            pl.BlockSpec(rtl.shape, const2),
            pl.BlockSpec(rbias.shape, const2),
            pl.BlockSpec((chunk, d, f), lambda i, c: (c, 0, 0)),
            pl.BlockSpec((chunk, d, f), lambda i, c: (c, 0, 0)),
            pl.BlockSpec((chunk, f, d), lambda i, c: (c, 0, 0)),
            pl.BlockSpec(sg.shape, const2),
            pl.BlockSpec(su.shape, const2),
            pl.BlockSpec(sd.shape, const2),
            pl.BlockSpec((1, d), const2),
            pl.BlockSpec((1, d), const2),
        ],
        out_specs=pl.BlockSpec((tm, d), lambda i, c: (i, 0)),
        scratch_shapes=[
            pltpu.VMEM((tm, d), BF16),
            pltpu.VMEM((tm, LANES), F32),
            pltpu.VMEM((tm, d), F32),
        ],
        compiler_params=_cparams("parallel", "arbitrary"),
        name="moe_ffn",
    )(x, rth, rtl, rbias, eg, eu, ed, sg, su, sd, g, b)


def _rope_tables(seq):
    t = jnp.arange(seq)
    row = (t // GRID_W).astype(F32)
    colp = (t % GRID_W).astype(F32)
    inv = ROPE_THETA ** (-jnp.arange(0, ROPE_AXIS_DIM, 2, dtype=F32) / ROPE_AXIS_DIM)
    ang = jnp.concatenate([row[:, None] * inv[None], colp[:, None] * inv[None]], axis=-1)
    ang = jnp.repeat(ang, 2, axis=-1)
    sign = jnp.where(jnp.arange(HEAD_DIM) % 2 == 0, -1.0, 1.0).astype(F32)
    cos = jnp.tile(jnp.cos(ang), (1, 2))
    sin = jnp.tile(jnp.sin(ang) * sign[None], (1, 2))
    return cos, sin


def _gqa_head_order():
    order = []
    for kvp in range(N_KV_HEADS // 2):
        base = 2 * GQA_GROUP * kvp
        for g in range(GQA_GROUP):
            order += [base + g, base + GQA_GROUP + g]
    return np.array(order)


def _pick_tile(n, target):
    t = min(n, target)
    while n % t:
        t //= 2
    return t


def kernel(x_prompt, x_sample, mem_prompt, mem_sample, a_w_in, a_lambda, a_subln, b_w_in, b_rpb,
           c_w_in, c_q_norm, c_k_norm, w_mem_kv, w_out, ln1_g, ln1_b, router, router_bias,
           e_gate, e_up, e_down, s_gate, s_up, s_down, ln2_g, ln2_b):
    depth = w_out.shape[0]
    alpha = (2.0 * depth) ** 0.25
    seq = x_prompt.shape[1]
    d = x_prompt.shape[2]
    assert x_sample.shape[1] == seq and seq % (NA_ROWS * GRID_W) == 0
    batch = x_prompt.shape[0] + x_sample.shape[0]
    x = jnp.concatenate([x_prompt.reshape(-1, d), x_sample.reshape(-1, d)], axis=0)
    mem = jnp.concatenate([mem_prompt, mem_sample], axis=0)
    n_mem = mem.shape[1]
    mem2d = mem.reshape(batch * n_mem, d)

    tm = _pick_tile(seq, 512)
    tk = _pick_tile(seq, 512)
    head_order = _gqa_head_order()
    col_order = (head_order[:, None] * HEAD_DIM + np.arange(HEAD_DIM)[None]).reshape(-1)
    cos, sin = _rope_tables(seq)
    slopes = _alibi_slopes(N_MIX_HEADS).reshape(N_MIX_PAIRS, 2)
    slope_tab = np.zeros((N_MIX_PAIRS, 8, LANES), np.float32)
    slope_tab[:, 0:2, :] = slopes[:, :, None]
    slope_tab = jnp.asarray(slope_tab) * LOG2E
    ones_row = functools.partial(jnp.ones, dtype=F32)
    mem_q_scale = jnp.full((MEM_WIDTH,), HEAD_DIM ** -0.5, F32)

    for i in range(depth):
        mixer, occ = i % N_MIXERS, i // N_MIXERS
        w_o = w_out[i]
        if mixer == 0:
            w_in = a_w_in[occ]
            colscale = jnp.concatenate([
                jnp.full((MIX_WIDTH,), DIFF_QK_DIM ** -0.5 * LOG2E, F32), ones_row((2 * MIX_WIDTH,)),
                mem_q_scale])
            heads = _in_projection(x, w_in.astype(BF16), colscale[None], tm, seq)
            gain = jnp.tile(a_subln[occ].astype(F32), 2)[None]
            mix = _diff_attention(heads, slope_tab, a_lambda[occ].astype(F32), gain, batch, seq,
                                  _pick_tile(seq, 256), tk, _diff_lambda_init(i))
        elif mixer == 1:
            w_in = b_w_in[occ]
            colscale = jnp.concatenate([
                jnp.full((MIX_WIDTH,), HEAD_DIM ** -0.5, F32), ones_row((2 * MIX_WIDTH,)),
                mem_q_scale])
            heads = _in_projection(x, w_in.astype(BF16), colscale[None], tm, seq)
            mix = _na_attention(heads, _na_bias_table(b_rpb[occ]), batch, seq)
        else:
            w_in = c_w_in[occ]
            kvw = N_KV_HEADS * HEAD_DIM
            w_in = jnp.concatenate([w_in[:, :MIX_WIDTH][:, col_order], w_in[:, MIX_WIDTH:]], axis=1)
            w_o = jnp.concatenate([w_o[:MIX_WIDTH][col_order], w_o[MIX_WIDTH:]], axis=0)
            colscale = jnp.concatenate([
                jnp.full((MIX_WIDTH,), HEAD_DIM ** -0.5 * LOG2E, F32), ones_row((2 * kvw,)),
                mem_q_scale])
            qg = jnp.tile(c_q_norm[occ].astype(F32), 2)[None, None]
            kg = jnp.tile(c_k_norm[occ].astype(F32), 2)[None, None]
            gains = jnp.concatenate([jnp.tile(qg, (N_MIX_PAIRS, 1, 1)),
                                     jnp.tile(kg, (N_KV_HEADS // 2, 1, 1))], axis=0)
            heads = _in_projection(x, w_in.astype(BF16), colscale[None], tm, seq,
                                   rope=(gains, cos, sin))
            mix = _gqa_attention(heads, batch, seq, _pick_tile(seq, 128), tk)
        kv = _matmul(mem2d, w_mem_kv[i].astype(BF16), _pick_tile(batch * n_mem, 512))
        kv = kv.reshape(batch, n_mem, 2 * MEM_WIDTH)
        x = _out_projection(mix, heads, kv, x, w_o.astype(BF16), ln1_g[i][None], ln1_b[i][None],
                            batch, seq, tm, alpha)
        rt = router[i].T.astype(F32)
        rth = rt.astype(BF16)
        rtl = (rt - rth.astype(F32)).astype(BF16)
        x = _moe(x, rth, rtl, router_bias[i].astype(F32)[:, None],
                 e_gate[i].astype(BF16), e_up[i].astype(BF16), e_down[i].astype(BF16),
                 s_gate[i].astype(BF16), s_up[i].astype(BF16), s_down[i].astype(BF16),
                 ln2_g[i][None], ln2_b[i][None], tm, 4, alpha)

    y = x.reshape(batch, seq, d)
    n_pb = x_prompt.shape[0]
    return (y[:n_pb], y[n_pb:])
```

```python
import functools
import math

import numpy as np
import jax
import jax.numpy as jnp
from jax import lax
from jax.experimental import pallas as pl
from jax.experimental.pallas import tpu as pltpu

F32 = jnp.float32
BF16 = jnp.bfloat16

D_MODEL = 1024
HEAD_DIM = 64
LANES = 128
N_MIX_HEADS = 12
N_MIX_PAIRS = N_MIX_HEADS // 2
N_MEM_HEADS = 4
N_MEM_PAIRS = N_MEM_HEADS // 2
MIX_WIDTH = N_MIX_HEADS * HEAD_DIM
MEM_WIDTH = N_MEM_HEADS * HEAD_DIM
N_MIXERS = 3
DIFF_QK_DIM = HEAD_DIM // 2
GRID_W = 64
NA_ROWS = 8
NA_COLS = 16
N_KV_HEADS = 4
GQA_GROUP = N_MIX_HEADS // N_KV_HEADS
ROPE_THETA = 10000.0
ROPE_AXIS_DIM = HEAD_DIM // 2
N_EXPERTS = 64
TOP_K = 8
N_GROUPS = 8
TOPK_GROUPS = 4
PER_GROUP = N_EXPERTS // N_GROUPS
D_EXPERT = 256
ROUTED_SCALE = 2.5
LN_EPS = 1e-5
RMS_EPS = 1e-6
NEG_BIG = -1e30
LOG2E = math.log2(math.e)

VMEM_LIMIT = 48 * 1024 * 1024

NT_DIMS = (((1,), (1,)), ((), ()))
TN_DIMS = (((0,), (0,)), ((), ()))


def _cparams(*sem):
    return pltpu.CompilerParams(dimension_semantics=sem, vmem_limit_bytes=VMEM_LIMIT)


def _alibi_slopes(n):
    def pow2_slopes(m):
        start = 2.0 ** (-8.0 / m)
        return [start ** (i + 1) for i in range(m)]
    if math.log2(n).is_integer():
        s = pow2_slopes(n)
    else:
        c = 2 ** math.floor(math.log2(n))
        s = pow2_slopes(c) + pow2_slopes(2 * c)[0::2][: n - c]
    return np.array(s, np.float32)


def _diff_lambda_init(layer_idx):
    return 0.8 - 0.6 * math.exp(-0.3 * layer_idx)


def _lane_iota(shape):
    return lax.broadcasted_iota(jnp.int32, shape, len(shape) - 1)


def _layer_norm(y, g, b):
    mu = jnp.mean(y, axis=-1, keepdims=True)
    yc = y - mu
    var = jnp.mean(yc * yc, axis=-1, keepdims=True)
    return yc * lax.rsqrt(var + LN_EPS) * g + b


def _head_rms_scale(x):
    lo = _lane_iota(x.shape) < HEAD_DIM
    xx = x * x
    ss_lo = jnp.sum(jnp.where(lo, xx, 0.0), axis=-1, keepdims=True)
    ss_hi = jnp.sum(jnp.where(lo, 0.0, xx), axis=-1, keepdims=True)
    inv = 1.0 / HEAD_DIM
    return jnp.where(lo, lax.rsqrt(ss_lo * inv + RMS_EPS), lax.rsqrt(ss_hi * inv + RMS_EPS))


def _inproj_kernel(x_ref, w_ref, cs_ref, o_ref):
    r = jnp.dot(x_ref[...].astype(BF16), w_ref[...], preferred_element_type=F32)
    r = r * cs_ref[...]
    for p in range(o_ref.shape[0]):
        o_ref[p] = r[:, p * LANES:(p + 1) * LANES].astype(BF16)


def _inproj_rope_kernel(x_ref, w_ref, cs_ref, g_ref, cos_ref, sin_ref, o_ref, *, n_rope_pairs):
    r = jnp.dot(x_ref[...].astype(BF16), w_ref[...], preferred_element_type=F32)
    cs = cs_ref[...]
    cos = cos_ref[...]
    sin = sin_ref[...]
    even = (_lane_iota(cos.shape) % 2) == 0
    for p in range(o_ref.shape[0]):
        xp = r[:, p * LANES:(p + 1) * LANES]
        if p < n_rope_pairs:
            xn = xp * _head_rms_scale(xp) * g_ref[p]
            partner = jnp.where(even, pltpu.roll(xn, LANES - 1, 1), pltpu.roll(xn, 1, 1))
            xp = xn * cos + partner * sin
        o_ref[p] = (xp * cs[:, p * LANES:(p + 1) * LANES]).astype(BF16)


def _in_projection(x, w, colscale, tm, seq, rope=None):
    n_tok, d = x.shape
    width = w.shape[1]
    n_pairs = width // LANES
    in_specs = [
        pl.BlockSpec((tm, d), lambda i: (i, 0)),
        pl.BlockSpec((d, width), lambda i: (0, 0)),
        pl.BlockSpec((1, width), lambda i: (0, 0)),
    ]
    args = [x, w, colscale]
    if rope is None:
        body = _inproj_kernel
    else:
        gains, cos, sin = rope
        tiles_per_seq = seq // tm
        body = functools.partial(_inproj_rope_kernel, n_rope_pairs=gains.shape[0])
        in_specs += [
            pl.BlockSpec(gains.shape, lambda i: (0, 0, 0)),
            pl.BlockSpec((tm, LANES), lambda i: (i % tiles_per_seq, 0)),
            pl.BlockSpec((tm, LANES), lambda i: (i % tiles_per_seq, 0)),
        ]
        args += [gains, cos, sin]
    return pl.pallas_call(
        body,
        out_shape=jax.ShapeDtypeStruct((n_pairs, n_tok, LANES), BF16),
        grid=(n_tok // tm,),
        in_specs=in_specs,
        out_specs=pl.BlockSpec((n_pairs, tm, LANES), lambda i: (0, i, 0)),
        compiler_params=_cparams("parallel"),
        name="in_projection",
    )(*args)


def _matmul_kernel(x_ref, w_ref, o_ref):
    o_ref[...] = jnp.dot(x_ref[...].astype(BF16), w_ref[...],
                         preferred_element_type=F32).astype(o_ref.dtype)


def _matmul(x, w, tm):
    m, k = x.shape
    n = w.shape[1]
    return pl.pallas_call(
        _matmul_kernel,
        out_shape=jax.ShapeDtypeStruct((m, n), BF16),
        grid=(m // tm,),
        in_specs=[pl.BlockSpec((tm, k), lambda i: (i, 0)), pl.BlockSpec((k, n), lambda i: (0, 0))],
        out_specs=pl.BlockSpec((tm, n), lambda i: (i, 0)),
        compiler_params=_cparams("parallel"),
        name="mem_kv_projection",
    )(x, w)


FLASH_UNROLL = 2


def _stack_masked_q(q_ref, qs_ref, tq):
    n_qpairs = q_ref.shape[0]
    per_pair = qs_ref.shape[0] // tq // n_qpairs
    lanes_per_map = LANES // per_pair
    for t in range(n_qpairs):
        q = q_ref[t].astype(F32)
        part = _lane_iota(q.shape) // lanes_per_map
        for a in range(per_pair):
            r0 = (t * per_pair + a) * tq
            qs_ref[r0:r0 + tq] = jnp.where(part == a, q, 0.0).astype(BF16)


def _softmax_accumulate(s, v, m, l, acc_ref, col_shift=None):
    cmax = jnp.max(s, axis=0, keepdims=True)
    if col_shift is None:
        m_new = jnp.maximum(m, cmax)
        m_sub = m_new
    else:
        m_new = jnp.maximum(m, cmax + col_shift)
        m_sub = m_new - col_shift
    alpha = jnp.exp2(m - m_new)
    p = jnp.exp2(s - m_sub)
    l_new = alpha * l + jnp.sum(p, axis=0, keepdims=True)
    pv = lax.dot_general(v, p.astype(BF16), TN_DIMS, preferred_element_type=F32)
    acc_ref[...] = alpha * acc_ref[...] + pv
    return m_new, l_new


def _diff_attn_kernel(q_ref, k_ref, v_ref, slope_ref, lam_ref, g_ref, o_ref, qs_ref, acc_ref,
                      *, tq, tk, seq, lambda_init):
    i = pl.program_id(2)
    n_maps = 4
    w = n_maps * tq
    _stack_masked_q(q_ref, qs_ref, tq)
    acc_ref[...] = jnp.zeros(acc_ref.shape, F32)
    slopes = slope_ref[0]
    jd = (i * tq) // tk

    def scores(j):
        off = pl.multiple_of(j * tk, tk)
        k = k_ref[0, pl.ds(off, tk), :]
        v = v_ref[0, pl.ds(off, tk), :]
        return lax.dot_general(k, qs_ref[...], NT_DIMS, preferred_element_type=F32), v

    s, v = scores(jd)
    rel = (lax.broadcasted_iota(jnp.int32, (tk, tq), 0) + (jd * tk - i * tq)
           - lax.broadcasted_iota(jnp.int32, (tk, tq), 1))
    dist = jnp.abs(rel).astype(F32)
    s = jnp.concatenate(
        [s[:, a * tq:(a + 1) * tq] - slopes[a // 2:a // 2 + 1, 0:1] * dist for a in range(n_maps)],
        axis=1)
    carry = _softmax_accumulate(s, v, jnp.full((1, w), -jnp.inf, F32), jnp.zeros((1, w), F32),
                                acc_ref)

    key_iota = lax.broadcasted_iota(jnp.int32, (tk, LANES), 0)
    qpos = lax.broadcasted_iota(jnp.int32, (1, tq), 1).astype(F32)

    def chunk(j, carry):
        m, l = carry
        s, v = scores(j)
        sgn = jnp.where(j < jd, 1.0, -1.0)
        skip = jnp.where(j == jd, -jnp.inf, 0.0)
        kpos = (key_iota + (j * tk - i * tq)).astype(F32)
        cols, shifts = [], []
        for h in range(2):
            sl = slopes[h:h + 1, 0:1] * sgn
            key_term = jnp.tile(sl * kpos + skip, (1, tq // LANES))
            q_term = -sl * qpos
            for c in range(2):
                a = 2 * h + c
                cols.append(s[:, a * tq:(a + 1) * tq] + key_term)
                shifts.append(q_term)
        return _softmax_accumulate(jnp.concatenate(cols, axis=1), v, m, l, acc_ref,
                                   col_shift=jnp.concatenate(shifts, axis=1))

    m, l = lax.fori_loop(0, seq // tk, chunk, carry, unroll=FLASH_UNROLL)

    o = acc_ref[...] / l
    lam = lam_ref[...]
    lam_full = (jnp.exp(jnp.sum(lam[0:1] * lam[1:2], axis=-1, keepdims=True))
                - jnp.exp(jnp.sum(lam[2:3] * lam[3:4], axis=-1, keepdims=True)) + lambda_init)
    h0 = o[:, 0:tq] - lam_full * o[:, tq:2 * tq]
    h1 = o[:, 2 * tq:3 * tq] - lam_full * o[:, 3 * tq:4 * tq]
    row = lax.broadcasted_iota(jnp.int32, (LANES, tq), 0)
    o = jnp.where(row < HEAD_DIM, h0, h1).T
    o = o * _head_rms_scale(o) * g_ref[...] * (1.0 - lambda_init)
    o_ref[0] = o.astype(BF16)


def _diff_attention(heads, slopes, lam, gain, batch, seq, tq, tk, lambda_init):
    assert tk % tq == 0 and tq % LANES == 0
    n_tok = heads.shape[1]
    nq = seq // tq
    body = functools.partial(_diff_attn_kernel, tq=tq, tk=tk, seq=seq, lambda_init=lambda_init)
    return pl.pallas_call(
        body,
        out_shape=jax.ShapeDtypeStruct((N_MIX_PAIRS, n_tok, LANES), BF16),
        grid=(batch, N_MIX_PAIRS, nq),
        in_specs=[
            pl.BlockSpec((1, tq, LANES), lambda b, p, i: (p, b * nq + i, 0)),
            pl.BlockSpec((1, seq, LANES), lambda b, p, i: (N_MIX_PAIRS + p, b, 0)),
            pl.BlockSpec((1, seq, LANES), lambda b, p, i: (2 * N_MIX_PAIRS + p, b, 0)),
            pl.BlockSpec((1, 8, LANES), lambda b, p, i: (p, 0, 0)),
            pl.BlockSpec(lam.shape, lambda b, p, i: (0, 0)),
            pl.BlockSpec((1, LANES), lambda b, p, i: (0, 0)),
        ],
        out_specs=pl.BlockSpec((1, tq, LANES), lambda b, p, i: (p, b * nq + i, 0)),
        scratch_shapes=[
            pltpu.VMEM((4 * tq, LANES), BF16),
            pltpu.VMEM((LANES, 4 * tq), F32),
        ],
        compiler_params=_cparams("parallel", "parallel", "parallel"),
        name="diff_attention",
    )(heads, heads, heads, slopes, lam, gain)


def _gqa_attn_kernel(q_ref, k_ref, v_ref, o_ref, qs_ref, acc_ref, *, tq, tk, seq):
    n_qpairs = q_ref.shape[0]
    w = 2 * n_qpairs * tq
    _stack_masked_q(q_ref, qs_ref, tq)
    acc_ref[...] = jnp.zeros(acc_ref.shape, F32)

    def chunk(j, carry):
        m, l = carry
        off = pl.multiple_of(j * tk, tk)
        k = k_ref[0, pl.ds(off, tk), :]
        v = v_ref[0, pl.ds(off, tk), :]
        s = lax.dot_general(k, qs_ref[...], NT_DIMS, preferred_element_type=F32)
        return _softmax_accumulate(s, v, m, l, acc_ref)

    carry = (jnp.full((1, w), -jnp.inf, F32), jnp.zeros((1, w), F32))
    m, l = lax.fori_loop(0, seq // tk, chunk, carry, unroll=FLASH_UNROLL)
    o = acc_ref[...] / l
    row = lax.broadcasted_iota(jnp.int32, (LANES, tq), 0)
    for t in range(n_qpairs):
        lo = o[:, (2 * t) * tq:(2 * t + 1) * tq]
        hi = o[:, (2 * t + 1) * tq:(2 * t + 2) * tq]
        o_ref[t] = jnp.where(row < HEAD_DIM, lo, hi).T.astype(BF16)


def _gqa_attention(heads, batch, seq, tq, tk):
    n_tok = heads.shape[1]
    nq = seq // tq
    n_kv_pairs = N_KV_HEADS // 2
    qp = N_MIX_PAIRS // n_kv_pairs
    body = functools.partial(_gqa_attn_kernel, tq=tq, tk=tk, seq=seq)
    return pl.pallas_call(
        body,
        out_shape=jax.ShapeDtypeStruct((N_MIX_PAIRS, n_tok, LANES), BF16),
        grid=(batch, n_kv_pairs, nq),
        in_specs=[
            pl.BlockSpec((qp, tq, LANES), lambda b, p, i: (p, b * nq + i, 0)),
            pl.BlockSpec((1, seq, LANES), lambda b, p, i: (N_MIX_PAIRS + p, b, 0)),
            pl.BlockSpec((1, seq, LANES), lambda b, p, i: (N_MIX_PAIRS + n_kv_pairs + p, b, 0)),
        ],
        out_specs=pl.BlockSpec((qp, tq, LANES), lambda b, p, i: (p, b * nq + i, 0)),
        scratch_shapes=[
            pltpu.VMEM((2 * qp * tq, LANES), BF16),
            pltpu.VMEM((LANES, 2 * qp * tq), F32),
        ],
        compiler_params=_cparams("parallel", "parallel", "parallel"),
        name="gqa_attention",
    )(heads, heads, heads)


def _na_attn_kernel(q_ref, kp_ref, kc_ref, kn_ref, vp_ref, vc_ref, vn_ref, bias_ref, o_ref,
                    kbuf_ref, vbuf_ref, *, n_grid_rows):
    rb = pl.program_id(2)
    blk = NA_ROWS * GRID_W
    win = NA_ROWS * GRID_W
    kbuf_ref[0:blk] = kp_ref[0]
    kbuf_ref[blk:2 * blk] = kc_ref[0]
    kbuf_ref[2 * blk:3 * blk] = kn_ref[0]
    vbuf_ref[0:blk] = vp_ref[0]
    vbuf_ref[blk:2 * blk] = vc_ref[0]
    vbuf_ref[2 * blk:3 * blk] = vn_ref[0]
    lo = _lane_iota((GRID_W, LANES)) < HEAD_DIM
    for u in range(NA_ROWS):
        r = rb * NA_ROWS + u
        rs = jnp.clip(r - NA_ROWS // 2, 0, n_grid_rows - NA_ROWS)
        off = pl.multiple_of((rs - (rb - 1) * NA_ROWS) * GRID_W, GRID_W)
        ro0 = rs - r + NA_ROWS - 1
        kw = kbuf_ref[pl.ds(off, win)]
        vw = vbuf_ref[pl.ds(off, win)]
        q = q_ref[0, u * GRID_W:(u + 1) * GRID_W].astype(F32)
        qs = jnp.concatenate([jnp.where(lo, q, 0.0), jnp.where(lo, 0.0, q)], axis=0).astype(BF16)
        s = lax.dot_general(qs, kw, NT_DIMS, preferred_element_type=F32)
        bias = jnp.concatenate(
            [jnp.concatenate([bias_ref[h, ro0 + 2 * t] for t in range(NA_ROWS // 2)], axis=1)
             for h in range(2)], axis=0)
        s = s + bias
        m = jnp.max(s, axis=-1, keepdims=True)
        p = jnp.exp(s - m)
        l = jnp.sum(p, axis=-1, keepdims=True)
        o = jnp.dot(p.astype(BF16), vw, preferred_element_type=F32) / l
        o_ref[0, u * GRID_W:(u + 1) * GRID_W] = jnp.where(lo, o[0:GRID_W], o[GRID_W:]).astype(BF16)


def _na_attention(heads, bias, batch, seq):
    n_tok = heads.shape[1]
    n_grid_rows = seq // GRID_W
    nrb = n_grid_rows // NA_ROWS
    blk = NA_ROWS * GRID_W

    def kv_spec(first_pair, shift):
        return pl.BlockSpec(
            (1, blk, LANES),
            lambda b, p, rb: (first_pair + p, b * nrb + jnp.clip(rb + shift, 0, nrb - 1), 0))

    body = functools.partial(_na_attn_kernel, n_grid_rows=n_grid_rows)
    return pl.pallas_call(
        body,
        out_shape=jax.ShapeDtypeStruct((N_MIX_PAIRS, n_tok, LANES), BF16),
        grid=(batch, N_MIX_PAIRS, nrb),
        in_specs=[
            pl.BlockSpec((1, blk, LANES), lambda b, p, rb: (p, b * nrb + rb, 0)),
            kv_spec(N_MIX_PAIRS, -1), kv_spec(N_MIX_PAIRS, 0), kv_spec(N_MIX_PAIRS, 1),
            kv_spec(2 * N_MIX_PAIRS, -1), kv_spec(2 * N_MIX_PAIRS, 0), kv_spec(2 * N_MIX_PAIRS, 1),
            pl.BlockSpec((2,) + bias.shape[1:], lambda b, p, rb: (p, 0, 0, 0)),
        ],
        out_specs=pl.BlockSpec((1, blk, LANES), lambda b, p, rb: (p, b * nrb + rb, 0)),
        scratch_shapes=[pltpu.VMEM((3 * blk, LANES), BF16), pltpu.VMEM((3 * blk, LANES), BF16)],
        compiler_params=_cparams("parallel", "parallel", "arbitrary"),
        name="neighbourhood_attention",
    )(heads, heads, heads, heads, heads, heads, heads, bias)


def _na_bias_table(rpb):
    col = jnp.arange(GRID_W)
    col_start = jnp.clip(col - NA_COLS // 2, 0, GRID_W - NA_COLS)
    col_mask = (col[None, :] >= col_start[:, None]) & (col[None, :] < col_start[:, None] + NA_COLS)
    col_bias_idx = jnp.clip(col[None, :] - col[:, None] + NA_COLS - 1, 0, 2 * NA_COLS - 2)
    tiles = rpb.astype(F32)[:, :, col_bias_idx]
    tiles = jnp.where(col_mask[None, None], tiles, NEG_BIG)
    return jnp.concatenate([tiles[:, :-1], tiles[:, 1:]], axis=-1)


def _outproj_kernel(mix_ref, qm_ref, kv_ref, x_ref, w_ref, g_ref, b_ref, o_ref, *, alpha):
    tm = x_ref.shape[0]
    lo = _lane_iota((tm, LANES)) < HEAD_DIM
    parts = [mix_ref[p] for p in range(mix_ref.shape[0])]
    kv = kv_ref[0]
    for t in range(N_MEM_PAIRS):
        q = qm_ref[t].astype(F32)
        qs = jnp.concatenate([jnp.where(lo, q, 0.0), jnp.where(lo, 0.0, q)], axis=0).astype(BF16)
        kt = kv[:, t * LANES:(t + 1) * LANES]
        vt = kv[:, MEM_WIDTH + t * LANES:MEM_WIDTH + (t + 1) * LANES]
        s = lax.dot_general(qs, kt, NT_DIMS, preferred_element_type=F32)
        m = jnp.max(s, axis=-1, keepdims=True)
        p = jnp.exp(s - m)
        l = jnp.sum(p, axis=-1, keepdims=True)
        o = jnp.dot(p.astype(BF16), vt, preferred_element_type=F32) / l
        parts.append(jnp.where(lo, o[0:tm], o[tm:]).astype(BF16))
    attn = jnp.concatenate(parts, axis=1)
    h = jnp.dot(attn, w_ref[...], preferred_element_type=F32)
    o_ref[...] = _layer_norm(alpha * x_ref[...] + h, g_ref[...], b_ref[...])


def _out_projection(mix, heads, kv, x, w_out, g, b, batch, seq, tm, alpha):
    n_tok, d = x.shape
    nt = seq // tm
    n_pairs = heads.shape[0]
    n_mem = kv.shape[1]
    body = functools.partial(_outproj_kernel, alpha=alpha)
    return pl.pallas_call(
        body,
        out_shape=jax.ShapeDtypeStruct((n_tok, d), F32),
        grid=(batch, nt),
        in_specs=[
            pl.BlockSpec((N_MIX_PAIRS, tm, LANES), lambda bi, i: (0, bi * nt + i, 0)),
            pl.BlockSpec((N_MEM_PAIRS, tm, LANES),
                         lambda bi, i: (n_pairs // N_MEM_PAIRS - 1, bi * nt + i, 0)),
            pl.BlockSpec((1, n_mem, 2 * MEM_WIDTH), lambda bi, i: (bi, 0, 0)),
            pl.BlockSpec((tm, d), lambda bi, i: (bi * nt + i, 0)),
            pl.BlockSpec(w_out.shape, lambda bi, i: (0, 0)),
            pl.BlockSpec((1, d), lambda bi, i: (0, 0)),
            pl.BlockSpec((1, d), lambda bi, i: (0, 0)),
        ],
        out_specs=pl.BlockSpec((tm, d), lambda bi, i: (bi * nt + i, 0)),
        compiler_params=_cparams("parallel", "parallel"),
        name="mem_attention_out_projection",
    )(mix, heads, kv, x, w_out, g, b)


def _first_index_of_max(vals, index, sentinel, axes):
    mx = vals
    for ax in axes:
        mx = jnp.max(mx, axis=ax, keepdims=True)
    idx = jnp.where(vals == mx, index, sentinel)
    for ax in axes:
        idx = jnp.min(idx, axis=ax, keepdims=True)
    return mx, idx


def _router_gates(x, rth_ref, rtl_ref, rbias_ref):
    tm = x.shape[0]
    xh = x.astype(BF16)
    xl = (x - xh.astype(F32)).astype(BF16)
    rth = rth_ref[...]
    logits = (lax.dot_general(rth, xh, NT_DIMS, preferred_element_type=F32)
              + lax.dot_general(rth, xl, NT_DIMS, preferred_element_type=F32)
              + lax.dot_general(rtl_ref[...], xh, NT_DIMS, preferred_element_type=F32))
    scores = jax.nn.sigmoid(logits)
    choice = scores + rbias_ref[...]
    shape3 = (N_GROUPS, PER_GROUP, tm)
    c3 = choice.reshape(shape3)
    s3 = scores.reshape(shape3)
    e_in_g = lax.broadcasted_iota(jnp.int32, shape3, 1).astype(F32)
    g_idx = lax.broadcasted_iota(jnp.int32, (N_GROUPS, 1, tm), 0).astype(F32)
    e_idx = lax.broadcasted_iota(jnp.int32, shape3, 0).astype(F32) * PER_GROUP + e_in_g
    m1, i1 = _first_index_of_max(c3, e_in_g, float(PER_GROUP), (1,))
    m2 = jnp.max(jnp.where(e_in_g == i1, -jnp.inf, c3), axis=1, keepdims=True)
    cur = m1 + m2
    gsel = jnp.zeros(cur.shape, F32)
    for _ in range(TOPK_GROUPS):
        _, gi = _first_index_of_max(cur, g_idx, float(N_GROUPS), (0,))
        hit = g_idx == gi
        gsel = jnp.where(hit, 1.0, gsel)
        cur = jnp.where(hit, -jnp.inf, cur)
    cur = jnp.where(gsel > 0.0, c3, -jnp.inf)
    sel = jnp.zeros(shape3, F32)
    for _ in range(TOP_K):
        _, ei = _first_index_of_max(cur, e_idx, float(N_EXPERTS), (1, 0))
        hit = e_idx == ei
        sel = jnp.where(hit, 1.0, sel)
        cur = jnp.where(hit, -jnp.inf, cur)
    w = sel * s3
    denom = jnp.sum(jnp.sum(w, axis=1, keepdims=True), axis=0, keepdims=True)
    gates_t = (w / denom * ROUTED_SCALE).reshape(N_EXPERTS, tm)
    gates_t = jnp.concatenate([gates_t, jnp.zeros((LANES - N_EXPERTS, tm), F32)], axis=0)
    return gates_t.T, xh


def _moe_kernel(x_ref, rth_ref, rtl_ref, rbias_ref, eg_ref, eu_ref, ed_ref,
                sg_ref, su_ref, sd_ref, g_ref, b_ref, o_ref,
                xb_ref, gate_ref, acc_ref, *, alpha, chunk):
    c = pl.program_id(1)

    @pl.when(c == 0)
    def _():
        gates, xh = _router_gates(x_ref[...], rth_ref, rtl_ref, rbias_ref)
        gate_ref[...] = gates
        xb_ref[...] = xh
        hs = (jax.nn.silu(jnp.dot(xh, sg_ref[...], preferred_element_type=F32))
              * jnp.dot(xh, su_ref[...], preferred_element_type=F32))
        acc_ref[...] = jnp.dot(hs.astype(BF16), sd_ref[...], preferred_element_type=F32)

    xb = xb_ref[...]
    gates = gate_ref[...]
    lane = _lane_iota(gates.shape)
    total = acc_ref[...]
    for jj in range(chunk):
        e = c * chunk + jj
        gcol = jnp.sum(jnp.where(lane == e, gates, 0.0), axis=-1, keepdims=True)
        h = (jax.nn.silu(jnp.dot(xb, eg_ref[jj], preferred_element_type=F32))
             * jnp.dot(xb, eu_ref[jj], preferred_element_type=F32)) * gcol
        total = total + jnp.dot(h.astype(BF16), ed_ref[jj], preferred_element_type=F32)
    acc_ref[...] = total

    @pl.when(c == pl.num_programs(1) - 1)
    def _():
        o_ref[...] = _layer_norm(alpha * x_ref[...] + acc_ref[...], g_ref[...], b_ref[...])


def _moe(x, rth, rtl, rbias, eg, eu, ed, sg, su, sd, g, b, tm, chunk, alpha):
    n_tok, d = x.shape
    n_exp, _, f = eg.shape
    const2 = lambda i, c: (0, 0)
    body = functools.partial(_moe_kernel, alpha=alpha, chunk=chunk)
    return pl.pallas_call(
        body,
        out_shape=jax.ShapeDtypeStruct((n_tok, d), F32),
        grid=(n_tok // tm, n_exp // chunk),
        in_specs=[
            pl.BlockSpec((tm, d), lambda i, c: (i, 0)),
            pl.BlockSpec(rth.shape, const2),
            pl.BlockSpec(rtl.shape, const2),
            pl.BlockSpec(rbias.shape, const2),
            pl.BlockSpec((chunk, d, f), lambda i, c: (c, 0, 0)),
            pl.BlockSpec((chunk, d, f), lambda i, c: (c, 0, 0)),
            pl.BlockSpec((chunk, f, d), lambda i, c: (c, 0, 0)),
            pl.BlockSpec(sg.shape, const2),
            pl.BlockSpec(su.shape, const2),
            pl.BlockSpec(sd.shape, const2),
            pl.BlockSpec((1, d), const2),
            pl.BlockSpec((1, d), const2),
        ],
        out_specs=pl.BlockSpec((tm, d), lambda i, c: (i, 0)),
        scratch_shapes=[
            pltpu.VMEM((tm, d), BF16),
            pltpu.VMEM((tm, LANES), F32),
            pltpu.VMEM((tm, d), F32),
        ],
        compiler_params=_cparams("parallel", "arbitrary"),
        name="moe_ffn",
    )(x, rth, rtl, rbias, eg, eu, ed, sg, su, sd, g, b)


def _rope_tables(seq):
    t = jnp.arange(seq)
    row = (t // GRID_W).astype(F32)
    colp = (t % GRID_W).astype(F32)
    inv = ROPE_THETA ** (-jnp.arange(0, ROPE_AXIS_DIM, 2, dtype=F32) / ROPE_AXIS_DIM)
    ang = jnp.concatenate([row[:, None] * inv[None], colp[:, None] * inv[None]], axis=-1)
    ang = jnp.repeat(ang, 2, axis=-1)
    sign = jnp.where(jnp.arange(HEAD_DIM) % 2 == 0, -1.0, 1.0).astype(F32)
    cos = jnp.tile(jnp.cos(ang), (1, 2))
    sin = jnp.tile(jnp.sin(ang) * sign[None], (1, 2))
    return cos, sin


def _gqa_head_order():
    order = []
    for kvp in range(N_KV_HEADS // 2):
        base = 2 * GQA_GROUP * kvp
        for g in range(GQA_GROUP):
            order += [base + g, base + GQA_GROUP + g]
    return np.array(order)


def _pick_tile(n, target):
    t = min(n, target)
    while n % t:
        t //= 2
    return t


def kernel(x_prompt, x_sample, mem_prompt, mem_sample, a_w_in, a_lambda, a_subln, b_w_in, b_rpb,
           c_w_in, c_q_norm, c_k_norm, w_mem_kv, w_out, ln1_g, ln1_b, router, router_bias,
           e_gate, e_up, e_down, s_gate, s_up, s_down, ln2_g, ln2_b):
    depth = w_out.shape[0]
    alpha = (2.0 * depth) ** 0.25
    seq = x_prompt.shape[1]
    d = x_prompt.shape[2]
    assert x_sample.shape[1] == seq and seq % (NA_ROWS * GRID_W) == 0
    batch = x_prompt.shape[0] + x_sample.shape[0]
    x = jnp.concatenate([x_prompt.reshape(-1, d), x_sample.reshape(-1, d)], axis=0)
    mem = jnp.concatenate([mem_prompt, mem_sample], axis=0)
    n_mem = mem.shape[1]
    mem2d = mem.reshape(batch * n_mem, d)

    tm = _pick_tile(seq, 512)
    tk = _pick_tile(seq, 512)
    head_order = _gqa_head_order()
    col_order = (head_order[:, None] * HEAD_DIM + np.arange(HEAD_DIM)[None]).reshape(-1)
    cos, sin = _rope_tables(seq)
    slopes = _alibi_slopes(N_MIX_HEADS).reshape(N_MIX_PAIRS, 2)
    slope_tab = np.zeros((N_MIX_PAIRS, 8, LANES), np.float32)
    slope_tab[:, 0:2, :] = slopes[:, :, None]
    slope_tab = jnp.asarray(slope_tab) * LOG2E
    ones_row = functools.partial(jnp.ones, dtype=F32)
    mem_q_scale = jnp.full((MEM_WIDTH,), HEAD_DIM ** -0.5, F32)

    for i in range(depth):
        mixer, occ = i % N_MIXERS, i // N_MIXERS
        w_o = w_out[i]
        if mixer == 0:
            w_in = a_w_in[occ]
            colscale = jnp.concatenate([
                jnp.full((MIX_WIDTH,), DIFF_QK_DIM ** -0.5 * LOG2E, F32), ones_row((2 * MIX_WIDTH,)),
                mem_q_scale])
            heads = _in_projection(x, w_in.astype(BF16), colscale[None], tm, seq)
            gain = jnp.tile(a_subln[occ].astype(F32), 2)[None]
            mix = _diff_attention(heads, slope_tab, a_lambda[occ].astype(F32), gain, batch, seq,
                                  _pick_tile(seq, 256), _pick_tile(seq, 1024),
                                  _diff_lambda_init(i))
        elif mixer == 1:
            w_in = b_w_in[occ]
            colscale = jnp.concatenate([
                jnp.full((MIX_WIDTH,), HEAD_DIM ** -0.5, F32), ones_row((2 * MIX_WIDTH,)),
                mem_q_scale])
            heads = _in_projection(x, w_in.astype(BF16), colscale[None], tm, seq)
            mix = _na_attention(heads, _na_bias_table(b_rpb[occ]), batch, seq)
        else:
            w_in = c_w_in[occ]
            kvw = N_KV_HEADS * HEAD_DIM
            w_in = jnp.concatenate([w_in[:, :MIX_WIDTH][:, col_order], w_in[:, MIX_WIDTH:]], axis=1)
            w_o = jnp.concatenate([w_o[:MIX_WIDTH][col_order], w_o[MIX_WIDTH:]], axis=0)
            colscale = jnp.concatenate([
                jnp.full((MIX_WIDTH,), HEAD_DIM ** -0.5 * LOG2E, F32), ones_row((2 * kvw,)),
                mem_q_scale])
            qg = jnp.tile(c_q_norm[occ].astype(F32), 2)[None, None]
            kg = jnp.tile(c_k_norm[occ].astype(F32), 2)[None, None]
            gains = jnp.concatenate([jnp.tile(qg, (N_MIX_PAIRS, 1, 1)),
                                     jnp.tile(kg, (N_KV_HEADS // 2, 1, 1))], axis=0)
            heads = _in_projection(x, w_in.astype(BF16), colscale[None], tm, seq,
                                   rope=(gains, cos, sin))
            mix = _gqa_attention(heads, batch, seq, _pick_tile(seq, 512), tk)
        kv = _matmul(mem2d, w_mem_kv[i].astype(BF16), _pick_tile(batch * n_mem, 512))
        kv = kv.reshape(batch, n_mem, 2 * MEM_WIDTH)
        x = _out_projection(mix, heads, kv, x, w_o.astype(BF16), ln1_g[i][None], ln1_b[i][None],
                            batch, seq, tm, alpha)
        rt = router[i].T.astype(F32)
        rth = rt.astype(BF16)
        rtl = (rt - rth.astype(F32)).astype(BF16)
        x = _moe(x, rth, rtl, router_bias[i].astype(F32)[:, None],
                 e_gate[i].astype(BF16), e_up[i].astype(BF16), e_down[i].astype(BF16),
                 s_gate[i].astype(BF16), s_up[i].astype(BF16), s_down[i].astype(BF16),
                 ln2_g[i][None], ln2_b[i][None], _pick_tile(batch * seq, 1024), 4, alpha)

    y = x.reshape(batch, seq, d)
    n_pb = x_prompt.shape[0]
    return (y[:n_pb], y[n_pb:])
```

```python
import functools
import math

import numpy as np
import jax
import jax.numpy as jnp
from jax import lax
from jax.experimental import pallas as pl
from jax.experimental.pallas import tpu as pltpu

F32 = jnp.float32
BF16 = jnp.bfloat16

D_MODEL = 1024
HEAD_DIM = 64
LANES = 128
N_MIX_HEADS = 12
N_MIX_PAIRS = N_MIX_HEADS // 2
N_MEM_HEADS = 4
N_MEM_PAIRS = N_MEM_HEADS // 2
MIX_WIDTH = N_MIX_HEADS * HEAD_DIM
MEM_WIDTH = N_MEM_HEADS * HEAD_DIM
N_MIXERS = 3
DIFF_QK_DIM = HEAD_DIM // 2
GRID_W = 64
NA_ROWS = 8
NA_COLS = 16
N_KV_HEADS = 4
GQA_GROUP = N_MIX_HEADS // N_KV_HEADS
ROPE_THETA = 10000.0
ROPE_AXIS_DIM = HEAD_DIM // 2
N_EXPERTS = 64
TOP_K = 8
N_GROUPS = 8
TOPK_GROUPS = 4
PER_GROUP = N_EXPERTS // N_GROUPS
D_EXPERT = 256
ROUTED_SCALE = 2.5
LN_EPS = 1e-5
RMS_EPS = 1e-6
NEG_BIG = -1e30
LOG2E = math.log2(math.e)

VMEM_LIMIT = 48 * 1024 * 1024

NT_DIMS = (((1,), (1,)), ((), ()))
TN_DIMS = (((0,), (0,)), ((), ()))


def _cparams(*sem):
    return pltpu.CompilerParams(dimension_semantics=sem, vmem_limit_bytes=VMEM_LIMIT)


def _alibi_slopes(n):
    def pow2_slopes(m):
        start = 2.0 ** (-8.0 / m)
        return [start ** (i + 1) for i in range(m)]
    if math.log2(n).is_integer():
        s = pow2_slopes(n)
    else:
        c = 2 ** math.floor(math.log2(n))
        s = pow2_slopes(c) + pow2_slopes(2 * c)[0::2][: n - c]
    return np.array(s, np.float32)


def _diff_lambda_init(layer_idx):
    return 0.8 - 0.6 * math.exp(-0.3 * layer_idx)


def _lane_iota(shape):
    return lax.broadcasted_iota(jnp.int32, shape, len(shape) - 1)


def _layer_norm(y, g, b):
    mu = jnp.mean(y, axis=-1, keepdims=True)
    yc = y - mu
    var = jnp.mean(yc * yc, axis=-1, keepdims=True)
    return yc * lax.rsqrt(var + LN_EPS) * g + b


def _head_rms_scale(x):
    lo = _lane_iota(x.shape) < HEAD_DIM
    xx = x * x
    ss_lo = jnp.sum(jnp.where(lo, xx, 0.0), axis=-1, keepdims=True)
    ss_hi = jnp.sum(jnp.where(lo, 0.0, xx), axis=-1, keepdims=True)
    inv = 1.0 / HEAD_DIM
    return jnp.where(lo, lax.rsqrt(ss_lo * inv + RMS_EPS), lax.rsqrt(ss_hi * inv + RMS_EPS))


def _inproj_kernel(x_ref, w_ref, cs_ref, o_ref):
    r = jnp.dot(x_ref[...].astype(BF16), w_ref[...], preferred_element_type=F32)
    r = r * cs_ref[...]
    for p in range(o_ref.shape[0]):
        o_ref[p] = r[:, p * LANES:(p + 1) * LANES].astype(BF16)


def _inproj_rope_kernel(x_ref, w_ref, cs_ref, g_ref, cos_ref, sin_ref, o_ref, *, n_rope_pairs):
    r = jnp.dot(x_ref[...].astype(BF16), w_ref[...], preferred_element_type=F32)
    cs = cs_ref[...]
    cos = cos_ref[...]
    sin = sin_ref[...]
    even = (_lane_iota(cos.shape) % 2) == 0
    for p in range(o_ref.shape[0]):
        xp = r[:, p * LANES:(p + 1) * LANES]
        if p < n_rope_pairs:
            xn = xp * _head_rms_scale(xp) * g_ref[p]
            partner = jnp.where(even, pltpu.roll(xn, LANES - 1, 1), pltpu.roll(xn, 1, 1))
            xp = xn * cos + partner * sin
        o_ref[p] = (xp * cs[:, p * LANES:(p + 1) * LANES]).astype(BF16)


def _in_projection(x, w, colscale, tm, seq, rope=None):
    n_tok, d = x.shape
    width = w.shape[1]
    n_pairs = width // LANES
    in_specs = [
        pl.BlockSpec((tm, d), lambda i: (i, 0)),
        pl.BlockSpec((d, width), lambda i: (0, 0)),
        pl.BlockSpec((1, width), lambda i: (0, 0)),
    ]
    args = [x, w, colscale]
    if rope is None:
        body = _inproj_kernel
    else:
        gains, cos, sin = rope
        tiles_per_seq = seq // tm
        body = functools.partial(_inproj_rope_kernel, n_rope_pairs=gains.shape[0])
        in_specs += [
            pl.BlockSpec(gains.shape, lambda i: (0, 0, 0)),
            pl.BlockSpec((tm, LANES), lambda i: (i % tiles_per_seq, 0)),
            pl.BlockSpec((tm, LANES), lambda i: (i % tiles_per_seq, 0)),
        ]
        args += [gains, cos, sin]
    return pl.pallas_call(
        body,
        out_shape=jax.ShapeDtypeStruct((n_pairs, n_tok, LANES), BF16),
        grid=(n_tok // tm,),
        in_specs=in_specs,
        out_specs=pl.BlockSpec((n_pairs, tm, LANES), lambda i: (0, i, 0)),
        compiler_params=_cparams("parallel"),
        name="in_projection",
    )(*args)


def _matmul_kernel(x_ref, w_ref, o_ref):
    o_ref[...] = jnp.dot(x_ref[...].astype(BF16), w_ref[...],
                         preferred_element_type=F32).astype(o_ref.dtype)


def _matmul(x, w, tm):
    m, k = x.shape
    n = w.shape[1]
    return pl.pallas_call(
        _matmul_kernel,
        out_shape=jax.ShapeDtypeStruct((m, n), BF16),
        grid=(m // tm,),
        in_specs=[pl.BlockSpec((tm, k), lambda i: (i, 0)), pl.BlockSpec((k, n), lambda i: (0, 0))],
        out_specs=pl.BlockSpec((tm, n), lambda i: (i, 0)),
        compiler_params=_cparams("parallel"),
        name="mem_kv_projection",
    )(x, w)


FLASH_UNROLL = 2


def _stack_masked_q(q_ref, qs_ref, tq):
    n_qpairs = q_ref.shape[0]
    per_pair = qs_ref.shape[0] // tq // n_qpairs
    lanes_per_map = LANES // per_pair
    for t in range(n_qpairs):
        q = q_ref[t].astype(F32)
        part = _lane_iota(q.shape) // lanes_per_map
        for a in range(per_pair):
            r0 = (t * per_pair + a) * tq
            qs_ref[r0:r0 + tq] = jnp.where(part == a, q, 0.0).astype(BF16)


def _softmax_accumulate(s, v, m, l, acc_ref):
    m_new = jnp.maximum(m, jnp.max(s, axis=0, keepdims=True))
    alpha = jnp.exp2(m - m_new)
    p = jnp.exp2(s - m_new)
    l_new = alpha * l + jnp.sum(p, axis=0, keepdims=True)
    pv = lax.dot_general(v, p.astype(BF16), TN_DIMS, preferred_element_type=F32)
    acc_ref[...] = alpha * acc_ref[...] + pv
    return m_new, l_new


def _diff_attn_kernel(q_ref, k_ref, v_ref, slope_ref, lam_ref, g_ref, o_ref, qs_ref, acc_ref,
                      *, tq, tk, seq, lambda_init):
    i = pl.program_id(2)
    n_maps = 4
    w = n_maps * tq
    _stack_masked_q(q_ref, qs_ref, tq)
    acc_ref[...] = jnp.zeros(acc_ref.shape, F32)
    slopes = slope_ref[0]
    rel0 = (lax.broadcasted_iota(jnp.int32, (tk, tq), 0)
            - lax.broadcasted_iota(jnp.int32, (tk, tq), 1) - i * tq)

    def chunk(j, carry):
        m, l = carry
        off = pl.multiple_of(j * tk, tk)
        k = k_ref[0, pl.ds(off, tk), :]
        v = v_ref[0, pl.ds(off, tk), :]
        s = lax.dot_general(k, qs_ref[...], NT_DIMS, preferred_element_type=F32)
        dist = jnp.abs(rel0 + j * tk).astype(F32)
        s = jnp.concatenate(
            [s[:, a * tq:(a + 1) * tq] - slopes[a // 2:a // 2 + 1, 0:1] * dist
             for a in range(n_maps)], axis=1)
        return _softmax_accumulate(s, v, m, l, acc_ref)

    carry = (jnp.full((1, w), -jnp.inf, F32), jnp.zeros((1, w), F32))
    m, l = lax.fori_loop(0, seq // tk, chunk, carry, unroll=FLASH_UNROLL)

    o = acc_ref[...] / l
    lam = lam_ref[...]
    lam_full = (jnp.exp(jnp.sum(lam[0:1] * lam[1:2], axis=-1, keepdims=True))
                - jnp.exp(jnp.sum(lam[2:3] * lam[3:4], axis=-1, keepdims=True)) + lambda_init)
    h0 = o[:, 0:tq] - lam_full * o[:, tq:2 * tq]
    h1 = o[:, 2 * tq:3 * tq] - lam_full * o[:, 3 * tq:4 * tq]
    row = lax.broadcasted_iota(jnp.int32, (LANES, tq), 0)
    o = jnp.where(row < HEAD_DIM, h0, h1).T
    o = o * _head_rms_scale(o) * g_ref[...] * (1.0 - lambda_init)
    o_ref[0] = o.astype(BF16)


def _diff_attention(heads, slopes, lam, gain, batch, seq, tq, tk, lambda_init):
    n_tok = heads.shape[1]
    nq = seq // tq
    body = functools.partial(_diff_attn_kernel, tq=tq, tk=tk, seq=seq, lambda_init=lambda_init)
    return pl.pallas_call(
        body,
        out_shape=jax.ShapeDtypeStruct((N_MIX_PAIRS, n_tok, LANES), BF16),
        grid=(batch, N_MIX_PAIRS, nq),
        in_specs=[
            pl.BlockSpec((1, tq, LANES), lambda b, p, i: (p, b * nq + i, 0)),
            pl.BlockSpec((1, seq, LANES), lambda b, p, i: (N_MIX_PAIRS + p, b, 0)),
            pl.BlockSpec((1, seq, LANES), lambda b, p, i: (2 * N_MIX_PAIRS + p, b, 0)),
            pl.BlockSpec((1, 8, LANES), lambda b, p, i: (p, 0, 0)),
            pl.BlockSpec(lam.shape, lambda b, p, i: (0, 0)),
            pl.BlockSpec((1, LANES), lambda b, p, i: (0, 0)),
        ],
        out_specs=pl.BlockSpec((1, tq, LANES), lambda b, p, i: (p, b * nq + i, 0)),
        scratch_shapes=[
            pltpu.VMEM((4 * tq, LANES), BF16),
            pltpu.VMEM((LANES, 4 * tq), F32),
        ],
        compiler_params=_cparams("parallel", "parallel", "parallel"),
        name="diff_attention",
    )(heads, heads, heads, slopes, lam, gain)


def _gqa_attn_kernel(q_ref, k_ref, v_ref, o_ref, qs_ref, acc_ref, *, tq, tk, seq):
    n_qpairs = q_ref.shape[0]
    w = 2 * n_qpairs * tq
    _stack_masked_q(q_ref, qs_ref, tq)
    acc_ref[...] = jnp.zeros(acc_ref.shape, F32)

    def chunk(j, carry):
        m, l = carry
        off = pl.multiple_of(j * tk, tk)
        k = k_ref[0, pl.ds(off, tk), :]
        v = v_ref[0, pl.ds(off, tk), :]
        s = lax.dot_general(k, qs_ref[...], NT_DIMS, preferred_element_type=F32)
        return _softmax_accumulate(s, v, m, l, acc_ref)

    carry = (jnp.full((1, w), -jnp.inf, F32), jnp.zeros((1, w), F32))
    m, l = lax.fori_loop(0, seq // tk, chunk, carry, unroll=FLASH_UNROLL)
    o = acc_ref[...] / l
    row = lax.broadcasted_iota(jnp.int32, (LANES, tq), 0)
    for t in range(n_qpairs):
        lo = o[:, (2 * t) * tq:(2 * t + 1) * tq]
        hi = o[:, (2 * t + 1) * tq:(2 * t + 2) * tq]
        o_ref[t] = jnp.where(row < HEAD_DIM, lo, hi).T.astype(BF16)


def _gqa_attention(heads, batch, seq, tq, tk):
    n_tok = heads.shape[1]
    nq = seq // tq
    n_kv_pairs = N_KV_HEADS // 2
    qp = N_MIX_PAIRS // n_kv_pairs
    body = functools.partial(_gqa_attn_kernel, tq=tq, tk=tk, seq=seq)
    return pl.pallas_call(
        body,
        out_shape=jax.ShapeDtypeStruct((N_MIX_PAIRS, n_tok, LANES), BF16),
        grid=(batch, n_kv_pairs, nq),
        in_specs=[
            pl.BlockSpec((qp, tq, LANES), lambda b, p, i: (p, b * nq + i, 0)),
            pl.BlockSpec((1, seq, LANES), lambda b, p, i: (N_MIX_PAIRS + p, b, 0)),
            pl.BlockSpec((1, seq, LANES), lambda b, p, i: (N_MIX_PAIRS + n_kv_pairs + p, b, 0)),
        ],
        out_specs=pl.BlockSpec((qp, tq, LANES), lambda b, p, i: (p, b * nq + i, 0)),
        scratch_shapes=[
            pltpu.VMEM((2 * qp * tq, LANES), BF16),
            pltpu.VMEM((LANES, 2 * qp * tq), F32),
        ],
        compiler_params=_cparams("parallel", "parallel", "parallel"),
        name="gqa_attention",
    )(heads, heads, heads)


def _na_attn_kernel(q_ref, kp_ref, kc_ref, kn_ref, vp_ref, vc_ref, vn_ref, bias_ref, o_ref,
                    kbuf_ref, vbuf_ref, *, n_grid_rows):
    rb = pl.program_id(2)
    blk = NA_ROWS * GRID_W
    win = NA_ROWS * GRID_W
    kbuf_ref[0:blk] = kp_ref[0]
    kbuf_ref[blk:2 * blk] = kc_ref[0]
    kbuf_ref[2 * blk:3 * blk] = kn_ref[0]
    vbuf_ref[0:blk] = vp_ref[0]
    vbuf_ref[blk:2 * blk] = vc_ref[0]
    vbuf_ref[2 * blk:3 * blk] = vn_ref[0]
    lo = _lane_iota((GRID_W, LANES)) < HEAD_DIM
    for u in range(NA_ROWS):
        r = rb * NA_ROWS + u
        rs = jnp.clip(r - NA_ROWS // 2, 0, n_grid_rows - NA_ROWS)
        off = pl.multiple_of((rs - (rb - 1) * NA_ROWS) * GRID_W, GRID_W)
        ro0 = rs - r + NA_ROWS - 1
        kw = kbuf_ref[pl.ds(off, win)]
        vw = vbuf_ref[pl.ds(off, win)]
        q = q_ref[0, u * GRID_W:(u + 1) * GRID_W].astype(F32)
        qs = jnp.concatenate([jnp.where(lo, q, 0.0), jnp.where(lo, 0.0, q)], axis=0).astype(BF16)
        s = lax.dot_general(qs, kw, NT_DIMS, preferred_element_type=F32)
        bias = jnp.concatenate(
            [jnp.concatenate([bias_ref[h, ro0 + 2 * t] for t in range(NA_ROWS // 2)], axis=1)
             for h in range(2)], axis=0)
        s = s + bias
        m = jnp.max(s, axis=-1, keepdims=True)
        p = jnp.exp(s - m)
        l = jnp.sum(p, axis=-1, keepdims=True)
        o = jnp.dot(p.astype(BF16), vw, preferred_element_type=F32) / l
        o_ref[0, u * GRID_W:(u + 1) * GRID_W] = jnp.where(lo, o[0:GRID_W], o[GRID_W:]).astype(BF16)


def _na_attention(heads, bias, batch, seq):
    n_tok = heads.shape[1]
    n_grid_rows = seq // GRID_W
    nrb = n_grid_rows // NA_ROWS
    blk = NA_ROWS * GRID_W

    def kv_spec(first_pair, shift):
        return pl.BlockSpec(
            (1, blk, LANES),
            lambda b, p, rb: (first_pair + p, b * nrb + jnp.clip(rb + shift, 0, nrb - 1), 0))

    body = functools.partial(_na_attn_kernel, n_grid_rows=n_grid_rows)
    return pl.pallas_call(
        body,
        out_shape=jax.ShapeDtypeStruct((N_MIX_PAIRS, n_tok, LANES), BF16),
        grid=(batch, N_MIX_PAIRS, nrb),
        in_specs=[
            pl.BlockSpec((1, blk, LANES), lambda b, p, rb: (p, b * nrb + rb, 0)),
            kv_spec(N_MIX_PAIRS, -1), kv_spec(N_MIX_PAIRS, 0), kv_spec(N_MIX_PAIRS, 1),
            kv_spec(2 * N_MIX_PAIRS, -1), kv_spec(2 * N_MIX_PAIRS, 0), kv_spec(2 * N_MIX_PAIRS, 1),
            pl.BlockSpec((2,) + bias.shape[1:], lambda b, p, rb: (p, 0, 0, 0)),
        ],
        out_specs=pl.BlockSpec((1, blk, LANES), lambda b, p, rb: (p, b * nrb + rb, 0)),
        scratch_shapes=[pltpu.VMEM((3 * blk, LANES), BF16), pltpu.VMEM((3 * blk, LANES), BF16)],
        compiler_params=_cparams("parallel", "parallel", "arbitrary"),
        name="neighbourhood_attention",
    )(heads, heads, heads, heads, heads, heads, heads, bias)


def _na_bias_table(rpb):
    col = jnp.arange(GRID_W)
    col_start = jnp.clip(col - NA_COLS // 2, 0, GRID_W - NA_COLS)
    col_mask = (col[None, :] >= col_start[:, None]) & (col[None, :] < col_start[:, None] + NA_COLS)
    col_bias_idx = jnp.clip(col[None, :] - col[:, None] + NA_COLS - 1, 0, 2 * NA_COLS - 2)
    tiles = rpb.astype(F32)[:, :, col_bias_idx]
    tiles = jnp.where(col_mask[None, None], tiles, NEG_BIG)
    return jnp.concatenate([tiles[:, :-1], tiles[:, 1:]], axis=-1)


def _outproj_kernel(mix_ref, qm_ref, kv_ref, x_ref, w_ref, g_ref, b_ref, o_ref, *, alpha):
    tm = x_ref.shape[0]
    lo = _lane_iota((tm, LANES)) < HEAD_DIM
    parts = [mix_ref[p] for p in range(mix_ref.shape[0])]
    kv = kv_ref[0]
    for t in range(N_MEM_PAIRS):
        q = qm_ref[t].astype(F32)
        qs = jnp.concatenate([jnp.where(lo, q, 0.0), jnp.where(lo, 0.0, q)], axis=0).astype(BF16)
        kt = kv[:, t * LANES:(t + 1) * LANES]
        vt = kv[:, MEM_WIDTH + t * LANES:MEM_WIDTH + (t + 1) * LANES]
        s = lax.dot_general(qs, kt, NT_DIMS, preferred_element_type=F32)
        m = jnp.max(s, axis=-1, keepdims=True)
        p = jnp.exp(s - m)
        l = jnp.sum(p, axis=-1, keepdims=True)
        o = jnp.dot(p.astype(BF16), vt, preferred_element_type=F32) / l
        parts.append(jnp.where(lo, o[0:tm], o[tm:]).astype(BF16))
    attn = jnp.concatenate(parts, axis=1)
    h = jnp.dot(attn, w_ref[...], preferred_element_type=F32)
    o_ref[...] = _layer_norm(alpha * x_ref[...] + h, g_ref[...], b_ref[...])


def _out_projection(mix, heads, kv, x, w_out, g, b, batch, seq, tm, alpha):
    n_tok, d = x.shape
    nt = seq // tm
    n_pairs = heads.shape[0]
    n_mem = kv.shape[1]
    body = functools.partial(_outproj_kernel, alpha=alpha)
    return pl.pallas_call(
        body,
        out_shape=jax.ShapeDtypeStruct((n_tok, d), F32),
        grid=(batch, nt),
        in_specs=[
            pl.BlockSpec((N_MIX_PAIRS, tm, LANES), lambda bi, i: (0, bi * nt + i, 0)),
            pl.BlockSpec((N_MEM_PAIRS, tm, LANES),
                         lambda bi, i: (n_pairs // N_MEM_PAIRS - 1, bi * nt + i, 0)),
            pl.BlockSpec((1, n_mem, 2 * MEM_WIDTH), lambda bi, i: (bi, 0, 0)),
            pl.BlockSpec((tm, d), lambda bi, i: (bi * nt + i, 0)),
            pl.BlockSpec(w_out.shape, lambda bi, i: (0, 0)),
            pl.BlockSpec((1, d), lambda bi, i: (0, 0)),
            pl.BlockSpec((1, d), lambda bi, i: (0, 0)),
        ],
        out_specs=pl.BlockSpec((tm, d), lambda bi, i: (bi * nt + i, 0)),
        compiler_params=_cparams("parallel", "parallel"),
        name="mem_attention_out_projection",
    )(mix, heads, kv, x, w_out, g, b)


def _first_index_of_max(vals, index, sentinel, axes):
    mx = vals
    for ax in axes:
        mx = jnp.max(mx, axis=ax, keepdims=True)
    idx = jnp.where(vals == mx, index, sentinel)
    for ax in axes:
        idx = jnp.min(idx, axis=ax, keepdims=True)
    return mx, idx


def _router_gates(x, rth_ref, rtl_ref, rbias_ref):
    tm = x.shape[0]
    xh = x.astype(BF16)
    xl = (x - xh.astype(F32)).astype(BF16)
    rth = rth_ref[...]
    logits = (lax.dot_general(rth, xh, NT_DIMS, preferred_element_type=F32)
              + lax.dot_general(rth, xl, NT_DIMS, preferred_element_type=F32)
              + lax.dot_general(rtl_ref[...], xh, NT_DIMS, preferred_element_type=F32))
    scores = jax.nn.sigmoid(logits)
    choice = scores + rbias_ref[...]
    shape3 = (N_GROUPS, PER_GROUP, tm)
    c3 = choice.reshape(shape3)
    s3 = scores.reshape(shape3)
    e_in_g = lax.broadcasted_iota(jnp.int32, shape3, 1).astype(F32)
    g_idx = lax.broadcasted_iota(jnp.int32, (N_GROUPS, 1, tm), 0).astype(F32)
    e_idx = lax.broadcasted_iota(jnp.int32, shape3, 0).astype(F32) * PER_GROUP + e_in_g
    m1, i1 = _first_index_of_max(c3, e_in_g, float(PER_GROUP), (1,))
    m2 = jnp.max(jnp.where(e_in_g == i1, -jnp.inf, c3), axis=1, keepdims=True)
    cur = m1 + m2
    gsel = jnp.zeros(cur.shape, F32)
    for _ in range(TOPK_GROUPS):
        _, gi = _first_index_of_max(cur, g_idx, float(N_GROUPS), (0,))
        hit = g_idx == gi
        gsel = jnp.where(hit, 1.0, gsel)
        cur = jnp.where(hit, -jnp.inf, cur)
    cur = jnp.where(gsel > 0.0, c3, -jnp.inf)
    sel = jnp.zeros(shape3, F32)
    for _ in range(TOP_K):
        _, ei = _first_index_of_max(cur, e_idx, float(N_EXPERTS), (1, 0))
        hit = e_idx == ei
        sel = jnp.where(hit, 1.0, sel)
        cur = jnp.where(hit, -jnp.inf, cur)
    w = sel * s3
    denom = jnp.sum(jnp.sum(w, axis=1, keepdims=True), axis=0, keepdims=True)
    gates_t = (w / denom * ROUTED_SCALE).reshape(N_EXPERTS, tm)
    gates_t = jnp.concatenate([gates_t, jnp.zeros((LANES - N_EXPERTS, tm), F32)], axis=0)
    return gates_t.T, xh


def _moe_kernel(x_ref, rth_ref, rtl_ref, rbias_ref, eg_ref, eu_ref, ed_ref,
                sg_ref, su_ref, sd_ref, g_ref, b_ref, o_ref,
                xb_ref, gate_ref, acc_ref, *, alpha, chunk):
    c = pl.program_id(1)

    @pl.when(c == 0)
    def _():
        gates, xh = _router_gates(x_ref[...], rth_ref, rtl_ref, rbias_ref)
        gate_ref[...] = gates
        xb_ref[...] = xh
        hs = (jax.nn.silu(jnp.dot(xh, sg_ref[...], preferred_element_type=F32))
              * jnp.dot(xh, su_ref[...], preferred_element_type=F32))
        acc_ref[...] = jnp.dot(hs.astype(BF16), sd_ref[...], preferred_element_type=F32)

    xb = xb_ref[...]
    gates = gate_ref[...]
    lane = _lane_iota(gates.shape)
    total = acc_ref[...]
    for jj in range(chunk):
        e = c * chunk + jj
        gcol = jnp.sum(jnp.where(lane == e, gates, 0.0), axis=-1, keepdims=True)
        h = (jax.nn.silu(jnp.dot(xb, eg_ref[jj], preferred_element_type=F32))
             * jnp.dot(xb, eu_ref[jj], preferred_element_type=F32)) * gcol
        total = total + jnp.dot(h.astype(BF16), ed_ref[jj], preferred_element_type=F32)
    acc_ref[...] = total

    @pl.when(c == pl.num_programs(1) - 1)
    def _():
        o_ref[...] = _layer_norm(alpha * x_ref[...] + acc_ref[...], g_ref[...], b_ref[...])


def _moe(x, rth, rtl, rbias, eg, eu, ed, sg, su, sd, g, b, tm, chunk, alpha):
    n_tok, d = x.shape
    n_exp, _, f = eg.shape
    const2 = lambda i, c: (0, 0)
    body = functools.partial(_moe_kernel, alpha=alpha, chunk=chunk)
    return pl.pallas_call(
        body,
        out_shape=jax.ShapeDtypeStruct((n_tok, d), F32),
        grid=(n_tok // tm, n_exp // chunk),
        in_specs=[
            pl.BlockSpec((tm, d), lambda i, c: (i, 0)),
            pl.BlockSpec(rth.shape, const2),
            pl.BlockSpec(rtl.shape, const2),
            pl.BlockSpec(rbias.shape, const2),
            pl.BlockSpec((chunk, d, f), lambda i, c: (c, 0, 0)),
            pl.BlockSpec((chunk, d, f), lambda i, c: (c, 0, 0)),
            pl.BlockSpec((chunk, f, d), lambda i, c: (c, 0, 0)),
            pl.BlockSpec(sg.shape, const2),
            pl.BlockSpec(su.shape, const2),
            pl.BlockSpec(sd.shape, const2),
            pl.BlockSpec((1, d), const2),
            pl.BlockSpec((1, d), const2),
        ],
        out_specs=pl.BlockSpec((tm, d), lambda i, c: (i, 0)),
        scratch_shapes=[
            pltpu.VMEM((tm, d), BF16),
            pltpu.VMEM((tm, LANES), F32),
            pltpu.VMEM((tm, d), F32),
        ],
        compiler_params=_cparams("parallel", "arbitrary"),
        name="moe_ffn",
    )(x, rth, rtl, rbias, eg, eu, ed, sg, su, sd, g, b)


def _rope_tables(seq):
    t = jnp.arange(seq)
    row = (t // GRID_W).astype(F32)
    colp = (t % GRID_W).astype(F32)
    inv = ROPE_THETA ** (-jnp.arange(0, ROPE_AXIS_DIM, 2, dtype=F32) / ROPE_AXIS_DIM)
    ang = jnp.concatenate([row[:, None] * inv[None], colp[:, None] * inv[None]], axis=-1)
    ang = jnp.repeat(ang, 2, axis=-1)
    sign = jnp.where(jnp.arange(HEAD_DIM) % 2 == 0, -1.0, 1.0).astype(F32)
    cos = jnp.tile(jnp.cos(ang), (1, 2))
    sin = jnp.tile(jnp.sin(ang) * sign[None], (1, 2))
    return cos, sin


def _gqa_head_order():
    order = []
    for kvp in range(N_KV_HEADS // 2):
        base = 2 * GQA_GROUP * kvp
        for g in range(GQA_GROUP):
            order += [base + g, base + GQA_GROUP + g]
    return np.array(order)


def _pick_tile(n, target):
    t = min(n, target)
    while n % t:
        t //= 2
    return t


def kernel(x_prompt, x_sample, mem_prompt, mem_sample, a_w_in, a_lambda, a_subln, b_w_in, b_rpb,
           c_w_in, c_q_norm, c_k_norm, w_mem_kv, w_out, ln1_g, ln1_b, router, router_bias,
           e_gate, e_up, e_down, s_gate, s_up, s_down, ln2_g, ln2_b):
    depth = w_out.shape[0]
    alpha = (2.0 * depth) ** 0.25
    seq = x_prompt.shape[1]
    d = x_prompt.shape[2]
    assert x_sample.shape[1] == seq and seq % (NA_ROWS * GRID_W) == 0
    batch = x_prompt.shape[0] + x_sample.shape[0]
    x = jnp.concatenate([x_prompt.reshape(-1, d), x_sample.reshape(-1, d)], axis=0)
    mem = jnp.concatenate([mem_prompt, mem_sample], axis=0)
    n_mem = mem.shape[1]
    mem2d = mem.reshape(batch * n_mem, d)

    tm = _pick_tile(seq, 512)
    tk = _pick_tile(seq, 512)
    head_order = _gqa_head_order()
    col_order = (head_order[:, None] * HEAD_DIM + np.arange(HEAD_DIM)[None]).reshape(-1)
    cos, sin = _rope_tables(seq)
    slopes = _alibi_slopes(N_MIX_HEADS).reshape(N_MIX_PAIRS, 2)
    slope_tab = np.zeros((N_MIX_PAIRS, 8, LANES), np.float32)
    slope_tab[:, 0:2, :] = slopes[:, :, None]
    slope_tab = jnp.asarray(slope_tab) * LOG2E
    ones_row = functools.partial(jnp.ones, dtype=F32)
    mem_q_scale = jnp.full((MEM_WIDTH,), HEAD_DIM ** -0.5, F32)

    for i in range(depth):
        mixer, occ = i % N_MIXERS, i // N_MIXERS
        w_o = w_out[i]
        if mixer == 0:
            w_in = a_w_in[occ]
            colscale = jnp.concatenate([
                jnp.full((MIX_WIDTH,), DIFF_QK_DIM ** -0.5 * LOG2E, F32), ones_row((2 * MIX_WIDTH,)),
                mem_q_scale])
            heads = _in_projection(x, w_in.astype(BF16), colscale[None], tm, seq)
            gain = jnp.tile(a_subln[occ].astype(F32), 2)[None]
            mix = _diff_attention(heads, slope_tab, a_lambda[occ].astype(F32), gain, batch, seq,
                                  _pick_tile(seq, 256), _pick_tile(seq, 1024),
                                  _diff_lambda_init(i))
        elif mixer == 1:
            w_in = b_w_in[occ]
            colscale = jnp.concatenate([
                jnp.full((MIX_WIDTH,), HEAD_DIM ** -0.5, F32), ones_row((2 * MIX_WIDTH,)),
                mem_q_scale])
            heads = _in_projection(x, w_in.astype(BF16), colscale[None], tm, seq)
            mix = _na_attention(heads, _na_bias_table(b_rpb[occ]), batch, seq)
        else:
            w_in = c_w_in[occ]
            kvw = N_KV_HEADS * HEAD_DIM
            w_in = jnp.concatenate([w_in[:, :MIX_WIDTH][:, col_order], w_in[:, MIX_WIDTH:]], axis=1)
            w_o = jnp.concatenate([w_o[:MIX_WIDTH][col_order], w_o[MIX_WIDTH:]], axis=0)
            colscale = jnp.concatenate([
                jnp.full((MIX_WIDTH,), HEAD_DIM ** -0.5 * LOG2E, F32), ones_row((2 * kvw,)),
                mem_q_scale])
            qg = jnp.tile(c_q_norm[occ].astype(F32), 2)[None, None]
            kg = jnp.tile(c_k_norm[occ].astype(F32), 2)[None, None]
            gains = jnp.concatenate([jnp.tile(qg, (N_MIX_PAIRS, 1, 1)),
                                     jnp.tile(kg, (N_KV_HEADS // 2, 1, 1))], axis=0)
            heads = _in_projection(x, w_in.astype(BF16), colscale[None], tm, seq,
                                   rope=(gains, cos, sin))
            mix = _gqa_attention(heads, batch, seq, _pick_tile(seq, 512), tk)
        kv = _matmul(mem2d, w_mem_kv[i].astype(BF16), _pick_tile(batch * n_mem, 512))
        kv = kv.reshape(batch, n_mem, 2 * MEM_WIDTH)
        x = _out_projection(mix, heads, kv, x, w_o.astype(BF16), ln1_g[i][None], ln1_b[i][None],
                            batch, seq, tm, alpha)
        rt = router[i].T.astype(F32)
        rth = rt.astype(BF16)
        rtl = (rt - rth.astype(F32)).astype(BF16)
        x = _moe(x, rth, rtl, router_bias[i].astype(F32)[:, None],
                 e_gate[i].astype(BF16), e_up[i].astype(BF16), e_down[i].astype(BF16),
                 s_gate[i].astype(BF16), s_up[i].astype(BF16), s_down[i].astype(BF16),
                 ln2_g[i][None], ln2_b[i][None], _pick_tile(batch * seq, 1024), 4, alpha)

    y = x.reshape(batch, seq, d)
    n_pb = x_prompt.shape[0]
    return (y[:n_pb], y[n_pb:])
```

```python
import functools
import math

import numpy as np
import jax
import jax.numpy as jnp
from jax import lax
from jax.experimental import pallas as pl
from jax.experimental.pallas import tpu as pltpu
from jax.experimental.pallas import tpu_sc as plsc

F32 = jnp.float32
BF16 = jnp.bfloat16

D_MODEL = 1024
HEAD_DIM = 64
LANES = 128
N_MIX_HEADS = 12
N_MIX_PAIRS = N_MIX_HEADS // 2
N_MEM_HEADS = 4
N_MEM_PAIRS = N_MEM_HEADS // 2
MIX_WIDTH = N_MIX_HEADS * HEAD_DIM
MEM_WIDTH = N_MEM_HEADS * HEAD_DIM
N_MIXERS = 3
DIFF_QK_DIM = HEAD_DIM // 2
GRID_W = 64
NA_ROWS = 8
NA_COLS = 16
N_KV_HEADS = 4
GQA_GROUP = N_MIX_HEADS // N_KV_HEADS
ROPE_THETA = 10000.0
ROPE_AXIS_DIM = HEAD_DIM // 2
N_EXPERTS = 64
TOP_K = 8
N_GROUPS = 8
TOPK_GROUPS = 4
PER_GROUP = N_EXPERTS // N_GROUPS
D_EXPERT = 256
ROUTED_SCALE = 2.5
LN_EPS = 1e-5
RMS_EPS = 1e-6
NEG_BIG = -1e30
LOG2E = math.log2(math.e)

VMEM_LIMIT = 48 * 1024 * 1024

NT_DIMS = (((1,), (1,)), ((), ()))
TN_DIMS = (((0,), (0,)), ((), ()))


def _cparams(*sem):
    return pltpu.CompilerParams(dimension_semantics=sem, vmem_limit_bytes=VMEM_LIMIT)


def _alibi_slopes(n):
    def pow2_slopes(m):
        start = 2.0 ** (-8.0 / m)
        return [start ** (i + 1) for i in range(m)]
    if math.log2(n).is_integer():
        s = pow2_slopes(n)
    else:
        c = 2 ** math.floor(math.log2(n))
        s = pow2_slopes(c) + pow2_slopes(2 * c)[0::2][: n - c]
    return np.array(s, np.float32)


def _diff_lambda_init(layer_idx):
    return 0.8 - 0.6 * math.exp(-0.3 * layer_idx)


def _lane_iota(shape):
    return lax.broadcasted_iota(jnp.int32, shape, len(shape) - 1)


def _layer_norm(y, g, b):
    mu = jnp.mean(y, axis=-1, keepdims=True)
    yc = y - mu
    var = jnp.mean(yc * yc, axis=-1, keepdims=True)
    return yc * lax.rsqrt(var + LN_EPS) * g + b


def _head_rms_scale(x):
    lo = _lane_iota(x.shape) < HEAD_DIM
    xx = x * x
    ss_lo = jnp.sum(jnp.where(lo, xx, 0.0), axis=-1, keepdims=True)
    ss_hi = jnp.sum(jnp.where(lo, 0.0, xx), axis=-1, keepdims=True)
    inv = 1.0 / HEAD_DIM
    return jnp.where(lo, lax.rsqrt(ss_lo * inv + RMS_EPS), lax.rsqrt(ss_hi * inv + RMS_EPS))


def _inproj_kernel(x_ref, w_ref, cs_ref, o_ref):
    r = jnp.dot(x_ref[...].astype(BF16), w_ref[...], preferred_element_type=F32)
    r = r * cs_ref[...]
    for p in range(o_ref.shape[0]):
        o_ref[p] = r[:, p * LANES:(p + 1) * LANES].astype(BF16)


def _inproj_rope_kernel(x_ref, w_ref, cs_ref, g_ref, cos_ref, sin_ref, o_ref, *, n_rope_pairs):
    r = jnp.dot(x_ref[...].astype(BF16), w_ref[...], preferred_element_type=F32)
    cs = cs_ref[...]
    cos = cos_ref[...]
    sin = sin_ref[...]
    even = (_lane_iota(cos.shape) % 2) == 0
    for p in range(o_ref.shape[0]):
        xp = r[:, p * LANES:(p + 1) * LANES]
        if p < n_rope_pairs:
            xn = xp * _head_rms_scale(xp) * g_ref[p]
            partner = jnp.where(even, pltpu.roll(xn, LANES - 1, 1), pltpu.roll(xn, 1, 1))
            xp = xn * cos + partner * sin
        o_ref[p] = (xp * cs[:, p * LANES:(p + 1) * LANES]).astype(BF16)


def _in_projection(x, w, colscale, tm, seq, rope=None):
    n_tok, d = x.shape
    width = w.shape[1]
    n_pairs = width // LANES
    in_specs = [
        pl.BlockSpec((tm, d), lambda i: (i, 0)),
        pl.BlockSpec((d, width), lambda i: (0, 0)),
        pl.BlockSpec((1, width), lambda i: (0, 0)),
    ]
    args = [x, w, colscale]
    if rope is None:
        body = _inproj_kernel
    else:
        gains, cos, sin = rope
        tiles_per_seq = seq // tm
        body = functools.partial(_inproj_rope_kernel, n_rope_pairs=gains.shape[0])
        in_specs += [
            pl.BlockSpec(gains.shape, lambda i: (0, 0, 0)),
            pl.BlockSpec((tm, LANES), lambda i: (i % tiles_per_seq, 0)),
            pl.BlockSpec((tm, LANES), lambda i: (i % tiles_per_seq, 0)),
        ]
        args += [gains, cos, sin]
    return pl.pallas_call(
        body,
        out_shape=jax.ShapeDtypeStruct((n_pairs, n_tok, LANES), BF16),
        grid=(n_tok // tm,),
        in_specs=in_specs,
        out_specs=pl.BlockSpec((n_pairs, tm, LANES), lambda i: (0, i, 0)),
        compiler_params=_cparams("parallel"),
        name="in_projection",
    )(*args)


def _matmul_kernel(x_ref, w_ref, o_ref):
    o_ref[...] = jnp.dot(x_ref[...].astype(BF16), w_ref[...],
                         preferred_element_type=F32).astype(o_ref.dtype)


def _matmul(x, w, tm):
    m, k = x.shape
    n = w.shape[1]
    return pl.pallas_call(
        _matmul_kernel,
        out_shape=jax.ShapeDtypeStruct((m, n), BF16),
        grid=(m // tm,),
        in_specs=[pl.BlockSpec((tm, k), lambda i: (i, 0)), pl.BlockSpec((k, n), lambda i: (0, 0))],
        out_specs=pl.BlockSpec((tm, n), lambda i: (i, 0)),
        compiler_params=_cparams("parallel"),
        name="mem_kv_projection",
    )(x, w)


FLASH_UNROLL = 2


def _stack_masked_q(q_ref, qs_ref, tq):
    n_qpairs = q_ref.shape[0]
    per_pair = qs_ref.shape[0] // tq // n_qpairs
    lanes_per_map = LANES // per_pair
    for t in range(n_qpairs):
        q = q_ref[t].astype(F32)
        part = _lane_iota(q.shape) // lanes_per_map
        for a in range(per_pair):
            r0 = (t * per_pair + a) * tq
            qs_ref[r0:r0 + tq] = jnp.where(part == a, q, 0.0).astype(BF16)


def _softmax_accumulate(s, v, m, l, acc_ref):
    m_new = jnp.maximum(m, jnp.max(s, axis=0, keepdims=True))
    alpha = jnp.exp2(m - m_new)
    p = jnp.exp2(s - m_new)
    l_new = alpha * l + jnp.sum(p, axis=0, keepdims=True)
    pv = lax.dot_general(v, p.astype(BF16), TN_DIMS, preferred_element_type=F32)
    acc_ref[...] = alpha * acc_ref[...] + pv
    return m_new, l_new


def _diff_attn_kernel(q_ref, k_ref, v_ref, slope_ref, lam_ref, g_ref, o_ref, qs_ref, acc_ref,
                      *, tq, tk, seq, lambda_init):
    i = pl.program_id(2)
    n_maps = 4
    w = n_maps * tq
    _stack_masked_q(q_ref, qs_ref, tq)
    acc_ref[...] = jnp.zeros(acc_ref.shape, F32)
    slopes = slope_ref[0]
    rel0 = (lax.broadcasted_iota(jnp.int32, (tk, tq), 0)
            - lax.broadcasted_iota(jnp.int32, (tk, tq), 1) - i * tq)

    def chunk(j, carry):
        m, l = carry
        off = pl.multiple_of(j * tk, tk)
        k = k_ref[0, pl.ds(off, tk), :]
        v = v_ref[0, pl.ds(off, tk), :]
        s = lax.dot_general(k, qs_ref[...], NT_DIMS, preferred_element_type=F32)
        dist = jnp.abs(rel0 + j * tk).astype(F32)
        s = jnp.concatenate(
            [s[:, a * tq:(a + 1) * tq] - slopes[a // 2:a // 2 + 1, 0:1] * dist
             for a in range(n_maps)], axis=1)
        return _softmax_accumulate(s, v, m, l, acc_ref)

    carry = (jnp.full((1, w), -jnp.inf, F32), jnp.zeros((1, w), F32))
    m, l = lax.fori_loop(0, seq // tk, chunk, carry, unroll=FLASH_UNROLL)

    o = acc_ref[...] / l
    lam = lam_ref[...]
    lam_full = (jnp.exp(jnp.sum(lam[0:1] * lam[1:2], axis=-1, keepdims=True))
                - jnp.exp(jnp.sum(lam[2:3] * lam[3:4], axis=-1, keepdims=True)) + lambda_init)
    h0 = o[:, 0:tq] - lam_full * o[:, tq:2 * tq]
    h1 = o[:, 2 * tq:3 * tq] - lam_full * o[:, 3 * tq:4 * tq]
    row = lax.broadcasted_iota(jnp.int32, (LANES, tq), 0)
    o = jnp.where(row < HEAD_DIM, h0, h1).T
    o = o * _head_rms_scale(o) * g_ref[...] * (1.0 - lambda_init)
    o_ref[0] = o.astype(BF16)


def _diff_attention(heads, slopes, lam, gain, batch, seq, tq, tk, lambda_init):
    n_tok = heads.shape[1]
    nq = seq // tq
    body = functools.partial(_diff_attn_kernel, tq=tq, tk=tk, seq=seq, lambda_init=lambda_init)
    return pl.pallas_call(
        body,
        out_shape=jax.ShapeDtypeStruct((N_MIX_PAIRS, n_tok, LANES), BF16),
        grid=(batch, N_MIX_PAIRS, nq),
        in_specs=[
            pl.BlockSpec((1, tq, LANES), lambda b, p, i: (p, b * nq + i, 0)),
            pl.BlockSpec((1, seq, LANES), lambda b, p, i: (N_MIX_PAIRS + p, b, 0)),
            pl.BlockSpec((1, seq, LANES), lambda b, p, i: (2 * N_MIX_PAIRS + p, b, 0)),
            pl.BlockSpec((1, 8, LANES), lambda b, p, i: (p, 0, 0)),
            pl.BlockSpec(lam.shape, lambda b, p, i: (0, 0)),
            pl.BlockSpec((1, LANES), lambda b, p, i: (0, 0)),
        ],
        out_specs=pl.BlockSpec((1, tq, LANES), lambda b, p, i: (p, b * nq + i, 0)),
        scratch_shapes=[
            pltpu.VMEM((4 * tq, LANES), BF16),
            pltpu.VMEM((LANES, 4 * tq), F32),
        ],
        compiler_params=_cparams("parallel", "parallel", "parallel"),
        name="diff_attention",
    )(heads, heads, heads, slopes, lam, gain)


def _gqa_attn_kernel(q_ref, k_ref, v_ref, o_ref, qs_ref, acc_ref, *, tq, tk, seq):
    n_qpairs = q_ref.shape[0]
    w = 2 * n_qpairs * tq
    _stack_masked_q(q_ref, qs_ref, tq)
    acc_ref[...] = jnp.zeros(acc_ref.shape, F32)

    def chunk(j, carry):
        m, l = carry
        off = pl.multiple_of(j * tk, tk)
        k = k_ref[0, pl.ds(off, tk), :]
        v = v_ref[0, pl.ds(off, tk), :]
        s = lax.dot_general(k, qs_ref[...], NT_DIMS, preferred_element_type=F32)
        return _softmax_accumulate(s, v, m, l, acc_ref)

    carry = (jnp.full((1, w), -jnp.inf, F32), jnp.zeros((1, w), F32))
    m, l = lax.fori_loop(0, seq // tk, chunk, carry, unroll=FLASH_UNROLL)
    o = acc_ref[...] / l
    row = lax.broadcasted_iota(jnp.int32, (LANES, tq), 0)
    for t in range(n_qpairs):
        lo = o[:, (2 * t) * tq:(2 * t + 1) * tq]
        hi = o[:, (2 * t + 1) * tq:(2 * t + 2) * tq]
        o_ref[t] = jnp.where(row < HEAD_DIM, lo, hi).T.astype(BF16)


def _gqa_attention(heads, batch, seq, tq, tk):
    n_tok = heads.shape[1]
    nq = seq // tq
    n_kv_pairs = N_KV_HEADS // 2
    qp = N_MIX_PAIRS // n_kv_pairs
    body = functools.partial(_gqa_attn_kernel, tq=tq, tk=tk, seq=seq)
    return pl.pallas_call(
        body,
        out_shape=jax.ShapeDtypeStruct((N_MIX_PAIRS, n_tok, LANES), BF16),
        grid=(batch, n_kv_pairs, nq),
        in_specs=[
            pl.BlockSpec((qp, tq, LANES), lambda b, p, i: (p, b * nq + i, 0)),
            pl.BlockSpec((1, seq, LANES), lambda b, p, i: (N_MIX_PAIRS + p, b, 0)),
            pl.BlockSpec((1, seq, LANES), lambda b, p, i: (N_MIX_PAIRS + n_kv_pairs + p, b, 0)),
        ],
        out_specs=pl.BlockSpec((qp, tq, LANES), lambda b, p, i: (p, b * nq + i, 0)),
        scratch_shapes=[
            pltpu.VMEM((2 * qp * tq, LANES), BF16),
            pltpu.VMEM((LANES, 2 * qp * tq), F32),
        ],
        compiler_params=_cparams("parallel", "parallel", "parallel"),
        name="gqa_attention",
    )(heads, heads, heads)


def _na_attn_kernel(q_ref, kp_ref, kc_ref, kn_ref, vp_ref, vc_ref, vn_ref, bias_ref, o_ref,
                    kbuf_ref, vbuf_ref, *, n_grid_rows):
    rb = pl.program_id(2)
    blk = NA_ROWS * GRID_W
    win = NA_ROWS * GRID_W
    kbuf_ref[0:blk] = kp_ref[0]
    kbuf_ref[blk:2 * blk] = kc_ref[0]
    kbuf_ref[2 * blk:3 * blk] = kn_ref[0]
    vbuf_ref[0:blk] = vp_ref[0]
    vbuf_ref[blk:2 * blk] = vc_ref[0]
    vbuf_ref[2 * blk:3 * blk] = vn_ref[0]
    lo = _lane_iota((GRID_W, LANES)) < HEAD_DIM
    for u in range(NA_ROWS):
        r = rb * NA_ROWS + u
        rs = jnp.clip(r - NA_ROWS // 2, 0, n_grid_rows - NA_ROWS)
        off = pl.multiple_of((rs - (rb - 1) * NA_ROWS) * GRID_W, GRID_W)
        ro0 = rs - r + NA_ROWS - 1
        kw = kbuf_ref[pl.ds(off, win)]
        vw = vbuf_ref[pl.ds(off, win)]
        q = q_ref[0, u * GRID_W:(u + 1) * GRID_W].astype(F32)
        qs = jnp.concatenate([jnp.where(lo, q, 0.0), jnp.where(lo, 0.0, q)], axis=0).astype(BF16)
        s = lax.dot_general(qs, kw, NT_DIMS, preferred_element_type=F32)
        bias = jnp.concatenate(
            [jnp.concatenate([bias_ref[h, ro0 + 2 * t] for t in range(NA_ROWS // 2)], axis=1)
             for h in range(2)], axis=0)
        s = s + bias
        m = jnp.max(s, axis=-1, keepdims=True)
        p = jnp.exp(s - m)
        l = jnp.sum(p, axis=-1, keepdims=True)
        o = jnp.dot(p.astype(BF16), vw, preferred_element_type=F32) / l
        o_ref[0, u * GRID_W:(u + 1) * GRID_W] = jnp.where(lo, o[0:GRID_W], o[GRID_W:]).astype(BF16)


def _na_attention(heads, bias, batch, seq):
    n_tok = heads.shape[1]
    n_grid_rows = seq // GRID_W
    nrb = n_grid_rows // NA_ROWS
    blk = NA_ROWS * GRID_W

    def kv_spec(first_pair, shift):
        return pl.BlockSpec(
            (1, blk, LANES),
            lambda b, p, rb: (first_pair + p, b * nrb + jnp.clip(rb + shift, 0, nrb - 1), 0))

    body = functools.partial(_na_attn_kernel, n_grid_rows=n_grid_rows)
    return pl.pallas_call(
        body,
        out_shape=jax.ShapeDtypeStruct((N_MIX_PAIRS, n_tok, LANES), BF16),
        grid=(batch, N_MIX_PAIRS, nrb),
        in_specs=[
            pl.BlockSpec((1, blk, LANES), lambda b, p, rb: (p, b * nrb + rb, 0)),
            kv_spec(N_MIX_PAIRS, -1), kv_spec(N_MIX_PAIRS, 0), kv_spec(N_MIX_PAIRS, 1),
            kv_spec(2 * N_MIX_PAIRS, -1), kv_spec(2 * N_MIX_PAIRS, 0), kv_spec(2 * N_MIX_PAIRS, 1),
            pl.BlockSpec((2,) + bias.shape[1:], lambda b, p, rb: (p, 0, 0, 0)),
        ],
        out_specs=pl.BlockSpec((1, blk, LANES), lambda b, p, rb: (p, b * nrb + rb, 0)),
        scratch_shapes=[pltpu.VMEM((3 * blk, LANES), BF16), pltpu.VMEM((3 * blk, LANES), BF16)],
        compiler_params=_cparams("parallel", "parallel", "arbitrary"),
        name="neighbourhood_attention",
    )(heads, heads, heads, heads, heads, heads, heads, bias)


def _na_bias_table(rpb):
    col = jnp.arange(GRID_W)
    col_start = jnp.clip(col - NA_COLS // 2, 0, GRID_W - NA_COLS)
    col_mask = (col[None, :] >= col_start[:, None]) & (col[None, :] < col_start[:, None] + NA_COLS)
    col_bias_idx = jnp.clip(col[None, :] - col[:, None] + NA_COLS - 1, 0, 2 * NA_COLS - 2)
    tiles = rpb.astype(F32)[:, :, col_bias_idx]
    tiles = jnp.where(col_mask[None, None], tiles, NEG_BIG)
    return jnp.concatenate([tiles[:, :-1], tiles[:, 1:]], axis=-1)


def _outproj_kernel(mix_ref, qm_ref, kv_ref, x_ref, w_ref, g_ref, b_ref, o_ref, *, alpha):
    tm = x_ref.shape[0]
    lo = _lane_iota((tm, LANES)) < HEAD_DIM
    parts = [mix_ref[p] for p in range(mix_ref.shape[0])]
    kv = kv_ref[0]
    for t in range(N_MEM_PAIRS):
        q = qm_ref[t].astype(F32)
        qs = jnp.concatenate([jnp.where(lo, q, 0.0), jnp.where(lo, 0.0, q)], axis=0).astype(BF16)
        kt = kv[:, t * LANES:(t + 1) * LANES]
        vt = kv[:, MEM_WIDTH + t * LANES:MEM_WIDTH + (t + 1) * LANES]
        s = lax.dot_general(qs, kt, NT_DIMS, preferred_element_type=F32)
        m = jnp.max(s, axis=-1, keepdims=True)
        p = jnp.exp(s - m)
        l = jnp.sum(p, axis=-1, keepdims=True)
        o = jnp.dot(p.astype(BF16), vt, preferred_element_type=F32) / l
        parts.append(jnp.where(lo, o[0:tm], o[tm:]).astype(BF16))
    attn = jnp.concatenate(parts, axis=1)
    h = jnp.dot(attn, w_ref[...], preferred_element_type=F32)
    o_ref[...] = _layer_norm(alpha * x_ref[...] + h, g_ref[...], b_ref[...])


def _out_projection(mix, heads, kv, x, w_out, g, b, batch, seq, tm, alpha):
    n_tok, d = x.shape
    nt = seq // tm
    n_pairs = heads.shape[0]
    n_mem = kv.shape[1]
    body = functools.partial(_outproj_kernel, alpha=alpha)
    return pl.pallas_call(
        body,
        out_shape=jax.ShapeDtypeStruct((n_tok, d), F32),
        grid=(batch, nt),
        in_specs=[
            pl.BlockSpec((N_MIX_PAIRS, tm, LANES), lambda bi, i: (0, bi * nt + i, 0)),
            pl.BlockSpec((N_MEM_PAIRS, tm, LANES),
                         lambda bi, i: (n_pairs // N_MEM_PAIRS - 1, bi * nt + i, 0)),
            pl.BlockSpec((1, n_mem, 2 * MEM_WIDTH), lambda bi, i: (bi, 0, 0)),
            pl.BlockSpec((tm, d), lambda bi, i: (bi * nt + i, 0)),
            pl.BlockSpec(w_out.shape, lambda bi, i: (0, 0)),
            pl.BlockSpec((1, d), lambda bi, i: (0, 0)),
            pl.BlockSpec((1, d), lambda bi, i: (0, 0)),
        ],
        out_specs=pl.BlockSpec((tm, d), lambda bi, i: (bi * nt + i, 0)),
        compiler_params=_cparams("parallel", "parallel"),
        name="mem_attention_out_projection",
    )(mix, heads, kv, x, w_out, g, b)


ROW_TILE = 256
SC_WINDOW = 128
HALF = D_MODEL // 2


def _bf16_bits(v):
    u = lax.bitcast_convert_type(v, jnp.uint32)
    return (u + jnp.uint32(0x7FFF) + ((u >> 16) & jnp.uint32(1))) >> 16


def _pack_half(v):
    q = HALF // 2
    return _bf16_bits(v[:, :q]) | (_bf16_bits(v[:, q:]) << 16)


def _unpack_half(w):
    lo = lax.bitcast_convert_type(w << 16, F32)
    hi = lax.bitcast_convert_type(w & jnp.uint32(0xFFFF0000), F32)
    return jnp.concatenate([lo, hi], axis=1)

def _first_index_of_max(vals, index, sentinel, axes):
    mx = vals
    for ax in axes:
        mx = jnp.max(mx, axis=ax, keepdims=True)
    idx = jnp.where(vals == mx, index, sentinel)
    for ax in axes:
        idx = jnp.min(idx, axis=ax, keepdims=True)
    return mx, idx


def _route_kernel(x_ref, rth_ref, rtl_ref, rbias_ref, tri_ref,
                  xp_ref, e8_ref, r8_ref, w8_ref, cnt_ref, run_ref):
    @pl.when(pl.program_id(0) == 0)
    def _():
        run_ref[...] = jnp.zeros(run_ref.shape, F32)

    x = x_ref[...]
    tm = x.shape[0]
    xp_ref[0] = _pack_half(x[:, :HALF])
    xp_ref[1] = _pack_half(x[:, HALF:])
    xh = x.astype(BF16)
    xl = (x - xh.astype(F32)).astype(BF16)
    rth = rth_ref[...]
    logits = (lax.dot_general(rth, xh, NT_DIMS, preferred_element_type=F32)
              + lax.dot_general(rth, xl, NT_DIMS, preferred_element_type=F32)
              + lax.dot_general(rtl_ref[...], xh, NT_DIMS, preferred_element_type=F32))
    scores = jax.nn.sigmoid(logits)
    choice = scores + rbias_ref[...]
    shape3 = (N_GROUPS, PER_GROUP, tm)
    c3 = choice.reshape(shape3)
    s3 = scores.reshape(shape3)
    e_in_g = lax.broadcasted_iota(jnp.int32, shape3, 1).astype(F32)
    g_idx = lax.broadcasted_iota(jnp.int32, (N_GROUPS, 1, tm), 0).astype(F32)
    e_idx = lax.broadcasted_iota(jnp.int32, shape3, 0).astype(F32) * PER_GROUP + e_in_g
    m1, i1 = _first_index_of_max(c3, e_in_g, float(PER_GROUP), (1,))
    m2 = jnp.max(jnp.where(e_in_g == i1, -jnp.inf, c3), axis=1, keepdims=True)
    cur = m1 + m2
    gsel = jnp.zeros(cur.shape, F32)
    for _ in range(TOPK_GROUPS):
        _, gi = _first_index_of_max(cur, g_idx, float(N_GROUPS), (0,))
        hit = g_idx == gi
        gsel = jnp.where(hit, 1.0, gsel)
        cur = jnp.where(hit, -jnp.inf, cur)
    cur = jnp.where(gsel > 0.0, c3, -jnp.inf)
    sel = jnp.zeros(shape3, F32)
    picks = []
    for _ in range(TOP_K):
        _, ei = _first_index_of_max(cur, e_idx, float(N_EXPERTS), (1, 0))
        hit = e_idx == ei
        sel = jnp.where(hit, 1.0, sel)
        cur = jnp.where(hit, -jnp.inf, cur)
        picks.append(ei)
    w = sel * s3
    denom = jnp.sum(jnp.sum(w, axis=1, keepdims=True), axis=0, keepdims=True)
    gates3 = w / denom * ROUTED_SCALE
    sel2 = sel.reshape(N_EXPERTS, tm)
    before = jnp.dot(sel2.astype(BF16), tri_ref[...], preferred_element_type=F32)
    rank3 = (before + run_ref[:, 0:1]).reshape(shape3)
    run_ref[...] = run_ref[...] + jnp.sum(sel2, axis=1, keepdims=True)
    cnt_ref[...] = run_ref[...]

    def pick(values, hit):
        return jnp.sum(jnp.sum(jnp.where(hit, values, 0.0), axis=1, keepdims=True), axis=0)

    for k, ei in enumerate(picks):
        hit = e_idx == ei
        e8_ref[k:k + 1, :] = ei.reshape(1, tm).astype(jnp.int32)
        r8_ref[k:k + 1, :] = pick(rank3, hit).astype(jnp.int32)
        w8_ref[k:k + 1, :] = pick(gates3, hit)


def _route(x, rth, rtl, rbias, tm):
    n_tok, d = x.shape
    tri = (jnp.arange(tm)[:, None] < jnp.arange(tm)[None, :]).astype(BF16)
    const2 = lambda i: (0, 0)
    return pl.pallas_call(
        _route_kernel,
        out_shape=(
            jax.ShapeDtypeStruct((2, n_tok, HALF // 2), jnp.uint32),
            jax.ShapeDtypeStruct((TOP_K, n_tok), jnp.int32),
            jax.ShapeDtypeStruct((TOP_K, n_tok), jnp.int32),
            jax.ShapeDtypeStruct((TOP_K, n_tok), F32),
            jax.ShapeDtypeStruct((N_EXPERTS, LANES), F32),
        ),
        grid=(n_tok // tm,),
        in_specs=[
            pl.BlockSpec((tm, d), lambda i: (i, 0)),
            pl.BlockSpec(rth.shape, const2),
            pl.BlockSpec(rtl.shape, const2),
            pl.BlockSpec(rbias.shape, const2),
            pl.BlockSpec((tm, tm), const2),
        ],
        out_specs=(
            pl.BlockSpec((2, tm, HALF // 2), lambda i: (0, i, 0)),
            pl.BlockSpec((TOP_K, tm), lambda i: (0, i)),
            pl.BlockSpec((TOP_K, tm), lambda i: (0, i)),
            pl.BlockSpec((TOP_K, tm), lambda i: (0, i)),
            pl.BlockSpec((N_EXPERTS, LANES), const2),
        ),
        scratch_shapes=[pltpu.VMEM((N_EXPERTS, LANES), F32)],
        compiler_params=_cparams("arbitrary"),
        name="moe_router",
    )(x, rth, rtl, rbias, tri)


def _positions_kernel(start_ref, e8_ref, r8_ref, idx_ref, *, n_rows):
    e8 = e8_ref[...]
    pos = r8_ref[...]
    for e in range(N_EXPERTS):
        pos = pos + jnp.where(e8 == e, start_ref[e], 0)
    idx_ref[0] = pos
    idx_ref[1] = pos + n_rows


def _positions(start, e8, r8, n_rows, tn):
    n_tok = e8.shape[1]
    grid_spec = pltpu.PrefetchScalarGridSpec(
        num_scalar_prefetch=1,
        grid=(n_tok // tn,),
        in_specs=[pl.BlockSpec((TOP_K, tn), lambda i, st: (0, i)),
                  pl.BlockSpec((TOP_K, tn), lambda i, st: (0, i))],
        out_specs=pl.BlockSpec((2, TOP_K, tn), lambda i, st: (0, 0, i)),
    )
    return pl.pallas_call(
        functools.partial(_positions_kernel, n_rows=n_rows),
        out_shape=jax.ShapeDtypeStruct((2, TOP_K, n_tok), jnp.int32),
        grid_spec=grid_spec,
        compiler_params=_cparams("parallel"),
        name="moe_positions",
    )(start, e8, r8)


def _experts_kernel(te_ref, used_ref, xs_ref, wg_ref, wu_ref, wd_ref, ys_ref):
    @pl.when(pl.program_id(0) < used_ref[0])
    def _():
        x = jnp.concatenate([_unpack_half(xs_ref[0]), _unpack_half(xs_ref[1])], axis=1).astype(BF16)
        h = (jax.nn.silu(jnp.dot(x, wg_ref[0], preferred_element_type=F32))
             * jnp.dot(x, wu_ref[0], preferred_element_type=F32))
        y = jnp.dot(h.astype(BF16), wd_ref[0], preferred_element_type=F32)
        ys_ref[0] = _pack_half(y[:, :HALF])
        ys_ref[1] = _pack_half(y[:, HALF:])


def _experts(tile_expert, n_used, xs, eg, eu, ed):
    n_rows = xs.shape[1]
    _, d, f = eg.shape
    grid_spec = pltpu.PrefetchScalarGridSpec(
        num_scalar_prefetch=2,
        grid=(n_rows // ROW_TILE,),
        in_specs=[
            pl.BlockSpec((2, ROW_TILE, HALF // 2), lambda i, te, nu: (0, i, 0)),
            pl.BlockSpec((1, d, f), lambda i, te, nu: (te[i], 0, 0)),
            pl.BlockSpec((1, d, f), lambda i, te, nu: (te[i], 0, 0)),
            pl.BlockSpec((1, f, d), lambda i, te, nu: (te[i], 0, 0)),
        ],
        out_specs=pl.BlockSpec((2, ROW_TILE, HALF // 2), lambda i, te, nu: (0, i, 0)),
    )
    return pl.pallas_call(
        _experts_kernel,
        out_shape=jax.ShapeDtypeStruct(xs.shape, jnp.uint32),
        grid_spec=grid_spec,
        compiler_params=_cparams("parallel"),
        name="moe_experts",
    )(tile_expert, n_used, xs, eg, eu, ed)


def _combine_kernel(x_ref, yg_ref, w8_ref, sg_ref, su_ref, sd_ref, g_ref, b_ref, o_ref, *, alpha):
    x = x_ref[...]
    tm = x.shape[0]
    xh = x.astype(BF16)
    hs = (jax.nn.silu(jnp.dot(xh, sg_ref[...], preferred_element_type=F32))
          * jnp.dot(xh, su_ref[...], preferred_element_type=F32))
    acc = jnp.dot(hs.astype(BF16), sd_ref[...], preferred_element_type=F32)
    w8 = jnp.concatenate([w8_ref[...], jnp.zeros((LANES - TOP_K, tm), F32)], axis=0).T
    for k in range(TOP_K):
        yk = jnp.concatenate([_unpack_half(yg_ref[0, k]), _unpack_half(yg_ref[1, k])], axis=1)
        acc = acc + w8[:, k:k + 1] * yk
    o_ref[...] = _layer_norm(alpha * x + acc, g_ref[...], b_ref[...])


def _combine(x, yg, w8, sg, su, sd, g, b, tm, alpha):
    n_tok, d = x.shape
    const2 = lambda i: (0, 0)
    return pl.pallas_call(
        functools.partial(_combine_kernel, alpha=alpha),
        out_shape=jax.ShapeDtypeStruct((n_tok, d), F32),
        grid=(n_tok // tm,),
        in_specs=[
            pl.BlockSpec((tm, d), lambda i: (i, 0)),
            pl.BlockSpec((2, TOP_K, tm, HALF // 2), lambda i: (0, 0, i, 0)),
            pl.BlockSpec((TOP_K, tm), lambda i: (0, i)),
            pl.BlockSpec(sg.shape, const2),
            pl.BlockSpec(su.shape, const2),
            pl.BlockSpec(sd.shape, const2),
            pl.BlockSpec((1, d), const2),
            pl.BlockSpec((1, d), const2),
        ],
        out_specs=pl.BlockSpec((tm, d), lambda i: (i, 0)),
        compiler_params=_cparams("parallel"),
        name="moe_combine",
    )(x, yg, w8, sg, su, sd, g, b)


def _sc_mesh():
    return plsc.VectorSubcoreMesh(core_axis_name="c", subcore_axis_name="s")


def _sc_gather_rows(table, idx):
    n = idx.shape[0]
    d = table.shape[1]

    @functools.partial(pl.kernel, out_type=jax.ShapeDtypeStruct((n, d), table.dtype),
                       mesh=_sc_mesh())
    def gather(table_hbm, idx_hbm, out_hbm):
        def body(idx_vmem, out_vmem):
            pltpu.sync_copy(table_hbm.at[idx_vmem.at[0]], out_vmem)

        pltpu.emit_pipeline(
            body, grid=(n // SC_WINDOW,),
            in_specs=[pl.BlockSpec((1, SC_WINDOW), index_map=lambda i: (0, i))],
            out_specs=[pl.BlockSpec((SC_WINDOW, d), index_map=lambda i: (i, 0))],
            core_axis_name=("c", "s"), dimension_semantics=(pltpu.PARALLEL,),
        )(idx_hbm, out_hbm)

    return gather(table, idx.reshape(1, n))


def _sc_scatter_rows(rows, idx, n_out, src_block):
    n = idx.shape[0]
    d = rows.shape[1]

    @functools.partial(pl.kernel, out_type=jax.ShapeDtypeStruct((n_out, d), rows.dtype),
                       mesh=_sc_mesh())
    def scatter(rows_hbm, idx_hbm, out_hbm):
        def body(rows_vmem, idx_vmem):
            pltpu.sync_copy(rows_vmem, out_hbm.at[idx_vmem.at[0]])

        pltpu.emit_pipeline(
            body, grid=(n // SC_WINDOW,),
            in_specs=[pl.BlockSpec((SC_WINDOW, d), index_map=lambda i: (src_block(i), 0)),
                      pl.BlockSpec((1, SC_WINDOW), index_map=lambda i: (0, i))],
            out_specs=[],
            core_axis_name=("c", "s"), dimension_semantics=(pltpu.PARALLEL,),
        )(rows_hbm, idx_hbm)

    return scatter(rows, idx.reshape(1, n))


def _moe(x, rth, rtl, rbias, eg, eu, ed, sg, su, sd, g, b, tm, alpha):
    n_tok, d = x.shape
    n_rows = TOP_K * n_tok + N_EXPERTS * ROW_TILE
    n_tiles = n_rows // ROW_TILE
    xp, e8, r8, w8, cnt = _route(x, rth, rtl, rbias, tm)
    counts = cnt[:, 0].astype(jnp.int32)
    padded = (counts + ROW_TILE - 1) // ROW_TILE * ROW_TILE
    ends = jnp.cumsum(padded)
    start = ends - padded
    n_used = (ends[-1:] // ROW_TILE).astype(jnp.int32)
    tile_expert = jnp.minimum(
        jnp.searchsorted(ends // ROW_TILE, jnp.arange(n_tiles, dtype=jnp.int32), side="right"),
        N_EXPERTS - 1).astype(jnp.int32)
    idx = _positions(start, e8, r8, n_rows, _pick_tile(n_tok, 2048)).reshape(-1)
    blocks_per_half = n_tok // SC_WINDOW
    xs = _sc_scatter_rows(
        xp.reshape(2 * n_tok, HALF // 2), idx, 2 * n_rows,
        lambda i: (i // (TOP_K * blocks_per_half)) * blocks_per_half + i % blocks_per_half)
    ys = _experts(tile_expert, n_used, xs.reshape(2, n_rows, HALF // 2), eg, eu, ed)
    yg = _sc_gather_rows(ys.reshape(2 * n_rows, HALF // 2), idx)
    return _combine(x, yg.reshape(2, TOP_K, n_tok, HALF // 2), w8, sg, su, sd, g, b, tm, alpha)


def _rope_tables(seq):
    t = jnp.arange(seq)
    row = (t // GRID_W).astype(F32)
    colp = (t % GRID_W).astype(F32)
    inv = ROPE_THETA ** (-jnp.arange(0, ROPE_AXIS_DIM, 2, dtype=F32) / ROPE_AXIS_DIM)
    ang = jnp.concatenate([row[:, None] * inv[None], colp[:, None] * inv[None]], axis=-1)
    ang = jnp.repeat(ang, 2, axis=-1)
    sign = jnp.where(jnp.arange(HEAD_DIM) % 2 == 0, -1.0, 1.0).astype(F32)
    cos = jnp.tile(jnp.cos(ang), (1, 2))
    sin = jnp.tile(jnp.sin(ang) * sign[None], (1, 2))
    return cos, sin


def _gqa_head_order():
    order = []
    for kvp in range(N_KV_HEADS // 2):
        base = 2 * GQA_GROUP * kvp
        for g in range(GQA_GROUP):
            order += [base + g, base + GQA_GROUP + g]
    return np.array(order)


def _pick_tile(n, target):
    t = min(n, target)
    while n % t:
        t //= 2
    return t


def kernel(x_prompt, x_sample, mem_prompt, mem_sample, a_w_in, a_lambda, a_subln, b_w_in, b_rpb,
           c_w_in, c_q_norm, c_k_norm, w_mem_kv, w_out, ln1_g, ln1_b, router, router_bias,
           e_gate, e_up, e_down, s_gate, s_up, s_down, ln2_g, ln2_b):
    depth = w_out.shape[0]
    alpha = (2.0 * depth) ** 0.25
    seq = x_prompt.shape[1]
    d = x_prompt.shape[2]
    assert x_sample.shape[1] == seq and seq % (NA_ROWS * GRID_W) == 0
    batch = x_prompt.shape[0] + x_sample.shape[0]
    x = jnp.concatenate([x_prompt.reshape(-1, d), x_sample.reshape(-1, d)], axis=0)
    mem = jnp.concatenate([mem_prompt, mem_sample], axis=0)
    n_mem = mem.shape[1]
    mem2d = mem.reshape(batch * n_mem, d)

    tm = _pick_tile(seq, 512)
    tk = _pick_tile(seq, 512)
    head_order = _gqa_head_order()
    col_order = (head_order[:, None] * HEAD_DIM + np.arange(HEAD_DIM)[None]).reshape(-1)
    cos, sin = _rope_tables(seq)
    slopes = _alibi_slopes(N_MIX_HEADS).reshape(N_MIX_PAIRS, 2)
    slope_tab = np.zeros((N_MIX_PAIRS, 8, LANES), np.float32)
    slope_tab[:, 0:2, :] = slopes[:, :, None]
    slope_tab = jnp.asarray(slope_tab) * LOG2E
    ones_row = functools.partial(jnp.ones, dtype=F32)
    mem_q_scale = jnp.full((MEM_WIDTH,), HEAD_DIM ** -0.5, F32)

    for i in range(depth):
        mixer, occ = i % N_MIXERS, i // N_MIXERS
        w_o = w_out[i]
        if mixer == 0:
            w_in = a_w_in[occ]
            colscale = jnp.concatenate([
                jnp.full((MIX_WIDTH,), DIFF_QK_DIM ** -0.5 * LOG2E, F32), ones_row((2 * MIX_WIDTH,)),
                mem_q_scale])
            heads = _in_projection(x, w_in.astype(BF16), colscale[None], tm, seq)
            gain = jnp.tile(a_subln[occ].astype(F32), 2)[None]
            mix = _diff_attention(heads, slope_tab, a_lambda[occ].astype(F32), gain, batch, seq,
                                  _pick_tile(seq, 256), _pick_tile(seq, 1024),
                                  _diff_lambda_init(i))
        elif mixer == 1:
            w_in = b_w_in[occ]
            colscale = jnp.concatenate([
                jnp.full((MIX_WIDTH,), HEAD_DIM ** -0.5, F32), ones_row((2 * MIX_WIDTH,)),
                mem_q_scale])
            heads = _in_projection(x, w_in.astype(BF16), colscale[None], tm, seq)
            mix = _na_attention(heads, _na_bias_table(b_rpb[occ]), batch, seq)
        else:
            w_in = c_w_in[occ]
            kvw = N_KV_HEADS * HEAD_DIM
            w_in = jnp.concatenate([w_in[:, :MIX_WIDTH][:, col_order], w_in[:, MIX_WIDTH:]], axis=1)
            w_o = jnp.concatenate([w_o[:MIX_WIDTH][col_order], w_o[MIX_WIDTH:]], axis=0)
            colscale = jnp.concatenate([
                jnp.full((MIX_WIDTH,), HEAD_DIM ** -0.5 * LOG2E, F32), ones_row((2 * kvw,)),
                mem_q_scale])
            qg = jnp.tile(c_q_norm[occ].astype(F32), 2)[None, None]
            kg = jnp.tile(c_k_norm[occ].astype(F32), 2)[None, None]
            gains = jnp.concatenate([jnp.tile(qg, (N_MIX_PAIRS, 1, 1)),
                                     jnp.tile(kg, (N_KV_HEADS // 2, 1, 1))], axis=0)
            heads = _in_projection(x, w_in.astype(BF16), colscale[None], tm, seq,
                                   rope=(gains, cos, sin))
            mix = _gqa_attention(heads, batch, seq, _pick_tile(seq, 512), tk)
        kv = _matmul(mem2d, w_mem_kv[i].astype(BF16), _pick_tile(batch * n_mem, 512))
        kv = kv.reshape(batch, n_mem, 2 * MEM_WIDTH)
        x = _out_projection(mix, heads, kv, x, w_o.astype(BF16), ln1_g[i][None], ln1_b[i][None],
                            batch, seq, tm, alpha)
        rt = router[i].T.astype(F32)
        rth = rt.astype(BF16)
        rtl = (rt - rth.astype(F32)).astype(BF16)
        x = _moe(x, rth, rtl, router_bias[i].astype(F32)[:, None],
                 e_gate[i].astype(BF16), e_up[i].astype(BF16), e_down[i].astype(BF16),
                 s_gate[i].astype(BF16), s_up[i].astype(BF16), s_down[i].astype(BF16),
                 ln2_g[i][None], ln2_b[i][None], tm, alpha)

    y = x.reshape(batch, seq, d)
    n_pb = x_prompt.shape[0]
    return (y[:n_pb], y[n_pb:])
```

```python
import functools
import math

import numpy as np
import jax
import jax.numpy as jnp
from jax import lax
from jax.experimental import pallas as pl
from jax.experimental.pallas import tpu as pltpu
from jax.experimental.pallas import tpu_sc as plsc

F32 = jnp.float32
BF16 = jnp.bfloat16

D_MODEL = 1024
HEAD_DIM = 64
LANES = 128
N_MIX_HEADS = 12
N_MIX_PAIRS = N_MIX_HEADS // 2
N_MEM_HEADS = 4
N_MEM_PAIRS = N_MEM_HEADS // 2
MIX_WIDTH = N_MIX_HEADS * HEAD_DIM
MEM_WIDTH = N_MEM_HEADS * HEAD_DIM
N_MIXERS = 3
DIFF_QK_DIM = HEAD_DIM // 2
GRID_W = 64
NA_ROWS = 8
NA_COLS = 16
N_KV_HEADS = 4
GQA_GROUP = N_MIX_HEADS // N_KV_HEADS
ROPE_THETA = 10000.0
ROPE_AXIS_DIM = HEAD_DIM // 2
N_EXPERTS = 64
TOP_K = 8
N_GROUPS = 8
TOPK_GROUPS = 4
PER_GROUP = N_EXPERTS // N_GROUPS
D_EXPERT = 256
ROUTED_SCALE = 2.5
LN_EPS = 1e-5
RMS_EPS = 1e-6
NEG_BIG = -1e30
LOG2E = math.log2(math.e)

VMEM_LIMIT = 48 * 1024 * 1024

NT_DIMS = (((1,), (1,)), ((), ()))
TN_DIMS = (((0,), (0,)), ((), ()))


def _cparams(*sem):
    return pltpu.CompilerParams(dimension_semantics=sem, vmem_limit_bytes=VMEM_LIMIT)


def _alibi_slopes(n):
    def pow2_slopes(m):
        start = 2.0 ** (-8.0 / m)
        return [start ** (i + 1) for i in range(m)]
    if math.log2(n).is_integer():
        s = pow2_slopes(n)
    else:
        c = 2 ** math.floor(math.log2(n))
        s = pow2_slopes(c) + pow2_slopes(2 * c)[0::2][: n - c]
    return np.array(s, np.float32)


def _diff_lambda_init(layer_idx):
    return 0.8 - 0.6 * math.exp(-0.3 * layer_idx)


def _lane_iota(shape):
    return lax.broadcasted_iota(jnp.int32, shape, len(shape) - 1)


def _layer_norm(y, g, b):
    mu = jnp.mean(y, axis=-1, keepdims=True)
    yc = y - mu
    var = jnp.mean(yc * yc, axis=-1, keepdims=True)
    return yc * lax.rsqrt(var + LN_EPS) * g + b


def _head_rms_scale(x):
    lo = _lane_iota(x.shape) < HEAD_DIM
    xx = x * x
    ss_lo = jnp.sum(jnp.where(lo, xx, 0.0), axis=-1, keepdims=True)
    ss_hi = jnp.sum(jnp.where(lo, 0.0, xx), axis=-1, keepdims=True)
    inv = 1.0 / HEAD_DIM
    return jnp.where(lo, lax.rsqrt(ss_lo * inv + RMS_EPS), lax.rsqrt(ss_hi * inv + RMS_EPS))


def _inproj_kernel(x_ref, w_ref, cs_ref, o_ref):
    r = jnp.dot(x_ref[...].astype(BF16), w_ref[...], preferred_element_type=F32)
    r = r * cs_ref[...]
    for p in range(o_ref.shape[0]):
        o_ref[p] = r[:, p * LANES:(p + 1) * LANES].astype(BF16)


def _inproj_rope_kernel(x_ref, w_ref, cs_ref, g_ref, cos_ref, sin_ref, o_ref, *, n_rope_pairs):
    r = jnp.dot(x_ref[...].astype(BF16), w_ref[...], preferred_element_type=F32)
    cs = cs_ref[...]
    cos = cos_ref[...]
    sin = sin_ref[...]
    even = (_lane_iota(cos.shape) % 2) == 0
    for p in range(o_ref.shape[0]):
        xp = r[:, p * LANES:(p + 1) * LANES]
        if p < n_rope_pairs:
            xn = xp * _head_rms_scale(xp) * g_ref[p]
            partner = jnp.where(even, pltpu.roll(xn, LANES - 1, 1), pltpu.roll(xn, 1, 1))
            xp = xn * cos + partner * sin
        o_ref[p] = (xp * cs[:, p * LANES:(p + 1) * LANES]).astype(BF16)


def _in_projection(x, w, colscale, tm, seq, rope=None):
    n_tok, d = x.shape
    width = w.shape[1]
    n_pairs = width // LANES
    in_specs = [
        pl.BlockSpec((tm, d), lambda i: (i, 0)),
        pl.BlockSpec((d, width), lambda i: (0, 0)),
        pl.BlockSpec((1, width), lambda i: (0, 0)),
    ]
    args = [x, w, colscale]
    if rope is None:
        body = _inproj_kernel
    else:
        gains, cos, sin = rope
        tiles_per_seq = seq // tm
        body = functools.partial(_inproj_rope_kernel, n_rope_pairs=gains.shape[0])
        in_specs += [
            pl.BlockSpec(gains.shape, lambda i: (0, 0, 0)),
            pl.BlockSpec((tm, LANES), lambda i: (i % tiles_per_seq, 0)),
            pl.BlockSpec((tm, LANES), lambda i: (i % tiles_per_seq, 0)),
        ]
        args += [gains, cos, sin]
    return pl.pallas_call(
        body,
        out_shape=jax.ShapeDtypeStruct((n_pairs, n_tok, LANES), BF16),
        grid=(n_tok // tm,),
        in_specs=in_specs,
        out_specs=pl.BlockSpec((n_pairs, tm, LANES), lambda i: (0, i, 0)),
        compiler_params=_cparams("parallel"),
        name="in_projection",
    )(*args)


def _matmul_kernel(x_ref, w_ref, o_ref):
    o_ref[...] = jnp.dot(x_ref[...].astype(BF16), w_ref[...],
                         preferred_element_type=F32).astype(o_ref.dtype)


def _matmul(x, w, tm):
    m, k = x.shape
    n = w.shape[1]
    return pl.pallas_call(
        _matmul_kernel,
        out_shape=jax.ShapeDtypeStruct((m, n), BF16),
        grid=(m // tm,),
        in_specs=[pl.BlockSpec((tm, k), lambda i: (i, 0)), pl.BlockSpec((k, n), lambda i: (0, 0))],
        out_specs=pl.BlockSpec((tm, n), lambda i: (i, 0)),
        compiler_params=_cparams("parallel"),
        name="mem_kv_projection",
    )(x, w)


FLASH_UNROLL = 2


def _stack_masked_q(q_ref, qs_ref, tq):
    n_qpairs = q_ref.shape[0]
    per_pair = qs_ref.shape[0] // tq // n_qpairs
    lanes_per_map = LANES // per_pair
    for t in range(n_qpairs):
        q = q_ref[t].astype(F32)
        part = _lane_iota(q.shape) // lanes_per_map
        for a in range(per_pair):
            r0 = (t * per_pair + a) * tq
            qs_ref[r0:r0 + tq] = jnp.where(part == a, q, 0.0).astype(BF16)


def _softmax_accumulate(s, v, m, l, acc_ref):
    m_new = jnp.maximum(m, jnp.max(s, axis=0, keepdims=True))
    alpha = jnp.exp2(m - m_new)
    p = jnp.exp2(s - m_new)
    l_new = alpha * l + jnp.sum(p, axis=0, keepdims=True)
    pv = lax.dot_general(v, p.astype(BF16), TN_DIMS, preferred_element_type=F32)
    acc_ref[...] = alpha * acc_ref[...] + pv
    return m_new, l_new


def _diff_attn_kernel(q_ref, k_ref, v_ref, slope_ref, lam_ref, g_ref, o_ref, qs_ref, acc_ref,
                      *, tq, tk, seq, lambda_init):
    i = pl.program_id(2)
    n_maps = 4
    w = n_maps * tq
    _stack_masked_q(q_ref, qs_ref, tq)
    acc_ref[...] = jnp.zeros(acc_ref.shape, F32)
    slopes = slope_ref[0]
    rel0 = (lax.broadcasted_iota(jnp.int32, (tk, tq), 0)
            - lax.broadcasted_iota(jnp.int32, (tk, tq), 1) - i * tq)

    def chunk(j, carry):
        m, l = carry
        off = pl.multiple_of(j * tk, tk)
        k = k_ref[0, pl.ds(off, tk), :]
        v = v_ref[0, pl.ds(off, tk), :]
        s = lax.dot_general(k, qs_ref[...], NT_DIMS, preferred_element_type=F32)
        dist = jnp.abs(rel0 + j * tk).astype(F32)
        s = jnp.concatenate(
            [s[:, a * tq:(a + 1) * tq] - slopes[a // 2:a // 2 + 1, 0:1] * dist
             for a in range(n_maps)], axis=1)
        return _softmax_accumulate(s, v, m, l, acc_ref)

    carry = (jnp.full((1, w), -jnp.inf, F32), jnp.zeros((1, w), F32))
    m, l = lax.fori_loop(0, seq // tk, chunk, carry, unroll=FLASH_UNROLL)

    o = acc_ref[...] / l
    lam = lam_ref[...]
    lam_full = (jnp.exp(jnp.sum(lam[0:1] * lam[1:2], axis=-1, keepdims=True))
                - jnp.exp(jnp.sum(lam[2:3] * lam[3:4], axis=-1, keepdims=True)) + lambda_init)
    h0 = o[:, 0:tq] - lam_full * o[:, tq:2 * tq]
    h1 = o[:, 2 * tq:3 * tq] - lam_full * o[:, 3 * tq:4 * tq]
    row = lax.broadcasted_iota(jnp.int32, (LANES, tq), 0)
    o = jnp.where(row < HEAD_DIM, h0, h1).T
    o = o * _head_rms_scale(o) * g_ref[...] * (1.0 - lambda_init)
    o_ref[0] = o.astype(BF16)


def _diff_attention(heads, slopes, lam, gain, batch, seq, tq, tk, lambda_init):
    n_tok = heads.shape[1]
    nq = seq // tq
    body = functools.partial(_diff_attn_kernel, tq=tq, tk=tk, seq=seq, lambda_init=lambda_init)
    return pl.pallas_call(
        body,
        out_shape=jax.ShapeDtypeStruct((N_MIX_PAIRS, n_tok, LANES), BF16),
        grid=(batch, N_MIX_PAIRS, nq),
        in_specs=[
            pl.BlockSpec((1, tq, LANES), lambda b, p, i: (p, b * nq + i, 0)),
            pl.BlockSpec((1, seq, LANES), lambda b, p, i: (N_MIX_PAIRS + p, b, 0)),
            pl.BlockSpec((1, seq, LANES), lambda b, p, i: (2 * N_MIX_PAIRS + p, b, 0)),
            pl.BlockSpec((1, 8, LANES), lambda b, p, i: (p, 0, 0)),
            pl.BlockSpec(lam.shape, lambda b, p, i: (0, 0)),
            pl.BlockSpec((1, LANES), lambda b, p, i: (0, 0)),
        ],
        out_specs=pl.BlockSpec((1, tq, LANES), lambda b, p, i: (p, b * nq + i, 0)),
        scratch_shapes=[
            pltpu.VMEM((4 * tq, LANES), BF16),
            pltpu.VMEM((LANES, 4 * tq), F32),
        ],
        compiler_params=_cparams("parallel", "parallel", "parallel"),
        name="diff_attention",
    )(heads, heads, heads, slopes, lam, gain)


def _gqa_attn_kernel(q_ref, k_ref, v_ref, o_ref, qs_ref, acc_ref, *, tq, tk, seq):
    n_qpairs = q_ref.shape[0]
    w = 2 * n_qpairs * tq
    _stack_masked_q(q_ref, qs_ref, tq)
    acc_ref[...] = jnp.zeros(acc_ref.shape, F32)

    def chunk(j, carry):
        m, l = carry
        off = pl.multiple_of(j * tk, tk)
        k = k_ref[0, pl.ds(off, tk), :]
        v = v_ref[0, pl.ds(off, tk), :]
        s = lax.dot_general(k, qs_ref[...], NT_DIMS, preferred_element_type=F32)
        return _softmax_accumulate(s, v, m, l, acc_ref)

    carry = (jnp.full((1, w), -jnp.inf, F32), jnp.zeros((1, w), F32))
    m, l = lax.fori_loop(0, seq // tk, chunk, carry, unroll=FLASH_UNROLL)
    o = acc_ref[...] / l
    row = lax.broadcasted_iota(jnp.int32, (LANES, tq), 0)
    for t in range(n_qpairs):
        lo = o[:, (2 * t) * tq:(2 * t + 1) * tq]
        hi = o[:, (2 * t + 1) * tq:(2 * t + 2) * tq]
        o_ref[t] = jnp.where(row < HEAD_DIM, lo, hi).T.astype(BF16)


def _gqa_attention(heads, batch, seq, tq, tk):
    n_tok = heads.shape[1]
    nq = seq // tq
    n_kv_pairs = N_KV_HEADS // 2
    qp = N_MIX_PAIRS // n_kv_pairs
    body = functools.partial(_gqa_attn_kernel, tq=tq, tk=tk, seq=seq)
    return pl.pallas_call(
        body,
        out_shape=jax.ShapeDtypeStruct((N_MIX_PAIRS, n_tok, LANES), BF16),
        grid=(batch, n_kv_pairs, nq),
        in_specs=[
            pl.BlockSpec((qp, tq, LANES), lambda b, p, i: (p, b * nq + i, 0)),
            pl.BlockSpec((1, seq, LANES), lambda b, p, i: (N_MIX_PAIRS + p, b, 0)),
            pl.BlockSpec((1, seq, LANES), lambda b, p, i: (N_MIX_PAIRS + n_kv_pairs + p, b, 0)),
        ],
        out_specs=pl.BlockSpec((qp, tq, LANES), lambda b, p, i: (p, b * nq + i, 0)),
        scratch_shapes=[
            pltpu.VMEM((2 * qp * tq, LANES), BF16),
            pltpu.VMEM((LANES, 2 * qp * tq), F32),
        ],
        compiler_params=_cparams("parallel", "parallel", "parallel"),
        name="gqa_attention",
    )(heads, heads, heads)


def _na_attn_kernel(q_ref, kp_ref, kc_ref, kn_ref, vp_ref, vc_ref, vn_ref, bias_ref, o_ref,
                    kbuf_ref, vbuf_ref, *, n_grid_rows):
    rb = pl.program_id(2)
    blk = NA_ROWS * GRID_W
    win = NA_ROWS * GRID_W
    kbuf_ref[0:blk] = kp_ref[0]
    kbuf_ref[blk:2 * blk] = kc_ref[0]
    kbuf_ref[2 * blk:3 * blk] = kn_ref[0]
    vbuf_ref[0:blk] = vp_ref[0]
    vbuf_ref[blk:2 * blk] = vc_ref[0]
    vbuf_ref[2 * blk:3 * blk] = vn_ref[0]
    lo = _lane_iota((GRID_W, LANES)) < HEAD_DIM
    for u in range(NA_ROWS):
        r = rb * NA_ROWS + u
        rs = jnp.clip(r - NA_ROWS // 2, 0, n_grid_rows - NA_ROWS)
        off = pl.multiple_of((rs - (rb - 1) * NA_ROWS) * GRID_W, GRID_W)
        ro0 = rs - r + NA_ROWS - 1
        kw = kbuf_ref[pl.ds(off, win)]
        vw = vbuf_ref[pl.ds(off, win)]
        q = q_ref[0, u * GRID_W:(u + 1) * GRID_W].astype(F32)
        qs = jnp.concatenate([jnp.where(lo, q, 0.0), jnp.where(lo, 0.0, q)], axis=0).astype(BF16)
        s = lax.dot_general(qs, kw, NT_DIMS, preferred_element_type=F32)
        bias = jnp.concatenate(
            [jnp.concatenate([bias_ref[h, ro0 + 2 * t] for t in range(NA_ROWS // 2)], axis=1)
             for h in range(2)], axis=0)
        s = s + bias
        m = jnp.max(s, axis=-1, keepdims=True)
        p = jnp.exp(s - m)
        l = jnp.sum(p, axis=-1, keepdims=True)
        o = jnp.dot(p.astype(BF16), vw, preferred_element_type=F32) / l
        o_ref[0, u * GRID_W:(u + 1) * GRID_W] = jnp.where(lo, o[0:GRID_W], o[GRID_W:]).astype(BF16)


def _na_attention(heads, bias, batch, seq):
    n_tok = heads.shape[1]
    n_grid_rows = seq // GRID_W
    nrb = n_grid_rows // NA_ROWS
    blk = NA_ROWS * GRID_W

    def kv_spec(first_pair, shift):
        return pl.BlockSpec(
            (1, blk, LANES),
            lambda b, p, rb: (first_pair + p, b * nrb + jnp.clip(rb + shift, 0, nrb - 1), 0))

    body = functools.partial(_na_attn_kernel, n_grid_rows=n_grid_rows)
    return pl.pallas_call(
        body,
        out_shape=jax.ShapeDtypeStruct((N_MIX_PAIRS, n_tok, LANES), BF16),
        grid=(batch, N_MIX_PAIRS, nrb),
        in_specs=[
            pl.BlockSpec((1, blk, LANES), lambda b, p, rb: (p, b * nrb + rb, 0)),
            kv_spec(N_MIX_PAIRS, -1), kv_spec(N_MIX_PAIRS, 0), kv_spec(N_MIX_PAIRS, 1),
            kv_spec(2 * N_MIX_PAIRS, -1), kv_spec(2 * N_MIX_PAIRS, 0), kv_spec(2 * N_MIX_PAIRS, 1),
            pl.BlockSpec((2,) + bias.shape[1:], lambda b, p, rb: (p, 0, 0, 0)),
        ],
        out_specs=pl.BlockSpec((1, blk, LANES), lambda b, p, rb: (p, b * nrb + rb, 0)),
        scratch_shapes=[pltpu.VMEM((3 * blk, LANES), BF16), pltpu.VMEM((3 * blk, LANES), BF16)],
        compiler_params=_cparams("parallel", "parallel", "arbitrary"),
        name="neighbourhood_attention",
    )(heads, heads, heads, heads, heads, heads, heads, bias)


def _na_bias_table(rpb):
    col = jnp.arange(GRID_W)
    col_start = jnp.clip(col - NA_COLS // 2, 0, GRID_W - NA_COLS)
    col_mask = (col[None, :] >= col_start[:, None]) & (col[None, :] < col_start[:, None] + NA_COLS)
    col_bias_idx = jnp.clip(col[None, :] - col[:, None] + NA_COLS - 1, 0, 2 * NA_COLS - 2)
    tiles = rpb.astype(F32)[:, :, col_bias_idx]
    tiles = jnp.where(col_mask[None, None], tiles, NEG_BIG)
    return jnp.concatenate([tiles[:, :-1], tiles[:, 1:]], axis=-1)


def _outproj_kernel(mix_ref, qm_ref, kv_ref, x_ref, w_ref, g_ref, b_ref, o_ref, *, alpha):
    tm = x_ref.shape[0]
    lo = _lane_iota((tm, LANES)) < HEAD_DIM
    parts = [mix_ref[p] for p in range(mix_ref.shape[0])]
    kv = kv_ref[0]
    for t in range(N_MEM_PAIRS):
        q = qm_ref[t].astype(F32)
        qs = jnp.concatenate([jnp.where(lo, q, 0.0), jnp.where(lo, 0.0, q)], axis=0).astype(BF16)
        kt = kv[:, t * LANES:(t + 1) * LANES]
        vt = kv[:, MEM_WIDTH + t * LANES:MEM_WIDTH + (t + 1) * LANES]
        s = lax.dot_general(qs, kt, NT_DIMS, preferred_element_type=F32)
        m = jnp.max(s, axis=-1, keepdims=True)
        p = jnp.exp(s - m)
        l = jnp.sum(p, axis=-1, keepdims=True)
        o = jnp.dot(p.astype(BF16), vt, preferred_element_type=F32) / l
        parts.append(jnp.where(lo, o[0:tm], o[tm:]).astype(BF16))
    attn = jnp.concatenate(parts, axis=1)
    h = jnp.dot(attn, w_ref[...], preferred_element_type=F32)
    o_ref[...] = _layer_norm(alpha * x_ref[...] + h, g_ref[...], b_ref[...])


def _out_projection(mix, heads, kv, x, w_out, g, b, batch, seq, tm, alpha):
    n_tok, d = x.shape
    nt = seq // tm
    n_pairs = heads.shape[0]
    n_mem = kv.shape[1]
    body = functools.partial(_outproj_kernel, alpha=alpha)
    return pl.pallas_call(
        body,
        out_shape=jax.ShapeDtypeStruct((n_tok, d), F32),
        grid=(batch, nt),
        in_specs=[
            pl.BlockSpec((N_MIX_PAIRS, tm, LANES), lambda bi, i: (0, bi * nt + i, 0)),
            pl.BlockSpec((N_MEM_PAIRS, tm, LANES),
                         lambda bi, i: (n_pairs // N_MEM_PAIRS - 1, bi * nt + i, 0)),
            pl.BlockSpec((1, n_mem, 2 * MEM_WIDTH), lambda bi, i: (bi, 0, 0)),
            pl.BlockSpec((tm, d), lambda bi, i: (bi * nt + i, 0)),
            pl.BlockSpec(w_out.shape, lambda bi, i: (0, 0)),
            pl.BlockSpec((1, d), lambda bi, i: (0, 0)),
            pl.BlockSpec((1, d), lambda bi, i: (0, 0)),
        ],
        out_specs=pl.BlockSpec((tm, d), lambda bi, i: (bi * nt + i, 0)),
        compiler_params=_cparams("parallel", "parallel"),
        name="mem_attention_out_projection",
    )(mix, heads, kv, x, w_out, g, b)


ROW_TILE = 512
SC_WINDOW = 128
HALF = D_MODEL // 2


def _bf16_bits(v):
    u = lax.bitcast_convert_type(v, jnp.uint32)
    return (u + jnp.uint32(0x7FFF) + ((u >> 16) & jnp.uint32(1))) >> 16


def _pack_half(v):
    q = HALF // 2
    return _bf16_bits(v[:, :q]) | (_bf16_bits(v[:, q:]) << 16)


def _unpack_half(w):
    lo = lax.bitcast_convert_type(w << 16, F32)
    hi = lax.bitcast_convert_type(w & jnp.uint32(0xFFFF0000), F32)
    return jnp.concatenate([lo, hi], axis=1)

def _first_index_of_max(vals, index, sentinel, axes):
    mx = vals
    for ax in axes:
        mx = jnp.max(mx, axis=ax, keepdims=True)
    idx = jnp.where(vals == mx, index, sentinel)
    for ax in axes:
        idx = jnp.min(idx, axis=ax, keepdims=True)
    return mx, idx


def _route_kernel(x_ref, rth_ref, rtl_ref, rbias_ref, tri_ref,
                  xp_ref, e8_ref, r8_ref, w8_ref, cnt_ref, run_ref):
    @pl.when(pl.program_id(0) == 0)
    def _():
        run_ref[...] = jnp.zeros(run_ref.shape, F32)

    x = x_ref[...]
    tm = x.shape[0]
    xp_ref[0] = _pack_half(x[:, :HALF])
    xp_ref[1] = _pack_half(x[:, HALF:])
    xh = x.astype(BF16)
    xl = (x - xh.astype(F32)).astype(BF16)
    rth = rth_ref[...]
    logits = (lax.dot_general(rth, xh, NT_DIMS, preferred_element_type=F32)
              + lax.dot_general(rth, xl, NT_DIMS, preferred_element_type=F32)
              + lax.dot_general(rtl_ref[...], xh, NT_DIMS, preferred_element_type=F32))
    scores = jax.nn.sigmoid(logits)
    choice = scores + rbias_ref[...]
    shape3 = (N_GROUPS, PER_GROUP, tm)
    c3 = choice.reshape(shape3)
    s3 = scores.reshape(shape3)
    e_in_g = lax.broadcasted_iota(jnp.int32, shape3, 1).astype(F32)
    g_idx = lax.broadcasted_iota(jnp.int32, (N_GROUPS, 1, tm), 0).astype(F32)
    e_idx = lax.broadcasted_iota(jnp.int32, shape3, 0).astype(F32) * PER_GROUP + e_in_g
    m1, i1 = _first_index_of_max(c3, e_in_g, float(PER_GROUP), (1,))
    m2 = jnp.max(jnp.where(e_in_g == i1, -jnp.inf, c3), axis=1, keepdims=True)
    cur = m1 + m2
    gsel = jnp.zeros(cur.shape, F32)
    for _ in range(TOPK_GROUPS):
        _, gi = _first_index_of_max(cur, g_idx, float(N_GROUPS), (0,))
        hit = g_idx == gi
        gsel = jnp.where(hit, 1.0, gsel)
        cur = jnp.where(hit, -jnp.inf, cur)
    cur = jnp.where(gsel > 0.0, c3, -jnp.inf)
    sel = jnp.zeros(shape3, F32)
    picks = []
    for _ in range(TOP_K):
        _, ei = _first_index_of_max(cur, e_idx, float(N_EXPERTS), (1, 0))
        hit = e_idx == ei
        sel = jnp.where(hit, 1.0, sel)
        cur = jnp.where(hit, -jnp.inf, cur)
        picks.append(ei)
    w = sel * s3
    denom = jnp.sum(jnp.sum(w, axis=1, keepdims=True), axis=0, keepdims=True)
    gates3 = w / denom * ROUTED_SCALE
    sel2 = sel.reshape(N_EXPERTS, tm)
    before = jnp.dot(sel2.astype(BF16), tri_ref[...], preferred_element_type=F32)
    rank3 = (before + run_ref[:, 0:1]).reshape(shape3)
    run_ref[...] = run_ref[...] + jnp.sum(sel2, axis=1, keepdims=True)
    cnt_ref[...] = run_ref[...]

    def pick(values, hit):
        return jnp.sum(jnp.sum(jnp.where(hit, values, 0.0), axis=1, keepdims=True), axis=0)

    for k, ei in enumerate(picks):
        hit = e_idx == ei
        e8_ref[k:k + 1, :] = ei.reshape(1, tm).astype(jnp.int32)
        r8_ref[k:k + 1, :] = pick(rank3, hit).astype(jnp.int32)
        w8_ref[k:k + 1, :] = pick(gates3, hit)


def _route(x, rth, rtl, rbias, tm):
    n_tok, d = x.shape
    tri = (jnp.arange(tm)[:, None] < jnp.arange(tm)[None, :]).astype(BF16)
    const2 = lambda i: (0, 0)
    return pl.pallas_call(
        _route_kernel,
        out_shape=(
            jax.ShapeDtypeStruct((2, n_tok, HALF // 2), jnp.uint32),
            jax.ShapeDtypeStruct((TOP_K, n_tok), jnp.int32),
            jax.ShapeDtypeStruct((TOP_K, n_tok), jnp.int32),
            jax.ShapeDtypeStruct((TOP_K, n_tok), F32),
            jax.ShapeDtypeStruct((N_EXPERTS, LANES), F32),
        ),
        grid=(n_tok // tm,),
        in_specs=[
            pl.BlockSpec((tm, d), lambda i: (i, 0)),
            pl.BlockSpec(rth.shape, const2),
            pl.BlockSpec(rtl.shape, const2),
            pl.BlockSpec(rbias.shape, const2),
            pl.BlockSpec((tm, tm), const2),
        ],
        out_specs=(
            pl.BlockSpec((2, tm, HALF // 2), lambda i: (0, i, 0)),
            pl.BlockSpec((TOP_K, tm), lambda i: (0, i)),
            pl.BlockSpec((TOP_K, tm), lambda i: (0, i)),
            pl.BlockSpec((TOP_K, tm), lambda i: (0, i)),
            pl.BlockSpec((N_EXPERTS, LANES), const2),
        ),
        scratch_shapes=[pltpu.VMEM((N_EXPERTS, LANES), F32)],
        compiler_params=_cparams("arbitrary"),
        name="moe_router",
    )(x, rth, rtl, rbias, tri)


def _positions_kernel(start_ref, e8_ref, r8_ref, idx_ref, *, n_rows):
    e8 = e8_ref[...]
    pos = r8_ref[...]
    for e in range(N_EXPERTS):
        pos = pos + jnp.where(e8 == e, start_ref[e], 0)
    idx_ref[0] = pos
    idx_ref[1] = pos + n_rows


def _positions(start, e8, r8, n_rows, tn):
    n_tok = e8.shape[1]
    grid_spec = pltpu.PrefetchScalarGridSpec(
        num_scalar_prefetch=1,
        grid=(n_tok // tn,),
        in_specs=[pl.BlockSpec((TOP_K, tn), lambda i, st: (0, i)),
                  pl.BlockSpec((TOP_K, tn), lambda i, st: (0, i))],
        out_specs=pl.BlockSpec((2, TOP_K, tn), lambda i, st: (0, 0, i)),
    )
    return pl.pallas_call(
        functools.partial(_positions_kernel, n_rows=n_rows),
        out_shape=jax.ShapeDtypeStruct((2, TOP_K, n_tok), jnp.int32),
        grid_spec=grid_spec,
        compiler_params=_cparams("parallel"),
        name="moe_positions",
    )(start, e8, r8)


def _experts_kernel(te_ref, used_ref, xs_ref, wg_ref, wu_ref, wd_ref, ys_ref):
    @pl.when(pl.program_id(0) < used_ref[0])
    def _():
        x = jnp.concatenate([_unpack_half(xs_ref[0]), _unpack_half(xs_ref[1])], axis=1).astype(BF16)
        h = (jax.nn.silu(jnp.dot(x, wg_ref[0], preferred_element_type=F32))
             * jnp.dot(x, wu_ref[0], preferred_element_type=F32))
        y = jnp.dot(h.astype(BF16), wd_ref[0], preferred_element_type=F32)
        ys_ref[0] = _pack_half(y[:, :HALF])
        ys_ref[1] = _pack_half(y[:, HALF:])


def _experts(tile_expert, n_used, xs, eg, eu, ed):
    n_rows = xs.shape[1]
    _, d, f = eg.shape
    grid_spec = pltpu.PrefetchScalarGridSpec(
        num_scalar_prefetch=2,
        grid=(n_rows // ROW_TILE,),
        in_specs=[
            pl.BlockSpec((2, ROW_TILE, HALF // 2), lambda i, te, nu: (0, i, 0)),
            pl.BlockSpec((1, d, f), lambda i, te, nu: (te[i], 0, 0)),
            pl.BlockSpec((1, d, f), lambda i, te, nu: (te[i], 0, 0)),
            pl.BlockSpec((1, f, d), lambda i, te, nu: (te[i], 0, 0)),
        ],
        out_specs=pl.BlockSpec((2, ROW_TILE, HALF // 2), lambda i, te, nu: (0, i, 0)),
    )
    return pl.pallas_call(
        _experts_kernel,
        out_shape=jax.ShapeDtypeStruct(xs.shape, jnp.uint32),
        grid_spec=grid_spec,
        compiler_params=_cparams("parallel"),
        name="moe_experts",
    )(tile_expert, n_used, xs, eg, eu, ed)


def _combine_kernel(x_ref, yg_ref, w8_ref, sg_ref, su_ref, sd_ref, g_ref, b_ref, o_ref, *, alpha):
    x = x_ref[...]
    tm = x.shape[0]
    xh = x.astype(BF16)
    hs = (jax.nn.silu(jnp.dot(xh, sg_ref[...], preferred_element_type=F32))
          * jnp.dot(xh, su_ref[...], preferred_element_type=F32))
    acc = jnp.dot(hs.astype(BF16), sd_ref[...], preferred_element_type=F32)
    w8 = jnp.concatenate([w8_ref[...], jnp.zeros((LANES - TOP_K, tm), F32)], axis=0).T
    for k in range(TOP_K):
        yk = jnp.concatenate([_unpack_half(yg_ref[0, k]), _unpack_half(yg_ref[1, k])], axis=1)
        acc = acc + w8[:, k:k + 1] * yk
    o_ref[...] = _layer_norm(alpha * x + acc, g_ref[...], b_ref[...])


def _combine(x, yg, w8, sg, su, sd, g, b, tm, alpha):
    n_tok, d = x.shape
    const2 = lambda i: (0, 0)
    return pl.pallas_call(
        functools.partial(_combine_kernel, alpha=alpha),
        out_shape=jax.ShapeDtypeStruct((n_tok, d), F32),
        grid=(n_tok // tm,),
        in_specs=[
            pl.BlockSpec((tm, d), lambda i: (i, 0)),
            pl.BlockSpec((2, TOP_K, tm, HALF // 2), lambda i: (0, 0, i, 0)),
            pl.BlockSpec((TOP_K, tm), lambda i: (0, i)),
            pl.BlockSpec(sg.shape, const2),
            pl.BlockSpec(su.shape, const2),
            pl.BlockSpec(sd.shape, const2),
            pl.BlockSpec((1, d), const2),
            pl.BlockSpec((1, d), const2),
        ],
        out_specs=pl.BlockSpec((tm, d), lambda i: (i, 0)),
        compiler_params=_cparams("parallel"),
        name="moe_combine",
    )(x, yg, w8, sg, su, sd, g, b)


def _sc_mesh():
    return plsc.VectorSubcoreMesh(core_axis_name="c", subcore_axis_name="s")


def _sc_gather_rows(table, idx):
    n = idx.shape[0]
    d = table.shape[1]

    @functools.partial(pl.kernel, out_type=jax.ShapeDtypeStruct((n, d), table.dtype),
                       mesh=_sc_mesh())
    def gather(table_hbm, idx_hbm, out_hbm):
        def body(idx_vmem, out_vmem):
            pltpu.sync_copy(table_hbm.at[idx_vmem.at[0]], out_vmem)

        pltpu.emit_pipeline(
            body, grid=(n // SC_WINDOW,),
            in_specs=[pl.BlockSpec((1, SC_WINDOW), index_map=lambda i: (0, i))],
            out_specs=[pl.BlockSpec((SC_WINDOW, d), index_map=lambda i: (i, 0))],
            core_axis_name=("c", "s"), dimension_semantics=(pltpu.PARALLEL,), trace_scopes=False,
        )(idx_hbm, out_hbm)

    return gather(table, idx.reshape(1, n))


def _sc_scatter_rows(rows, idx, n_out, src_block):
    n = idx.shape[0]
    d = rows.shape[1]

    @functools.partial(pl.kernel, out_type=jax.ShapeDtypeStruct((n_out, d), rows.dtype),
                       mesh=_sc_mesh())
    def scatter(rows_hbm, idx_hbm, out_hbm):
        def body(rows_vmem, idx_vmem):
            pltpu.sync_copy(rows_vmem, out_hbm.at[idx_vmem.at[0]])

        pltpu.emit_pipeline(
            body, grid=(n // SC_WINDOW,),
            in_specs=[pl.BlockSpec((SC_WINDOW, d), index_map=lambda i: (src_block(i), 0)),
                      pl.BlockSpec((1, SC_WINDOW), index_map=lambda i: (0, i))],
            out_specs=[],
            core_axis_name=("c", "s"), dimension_semantics=(pltpu.PARALLEL,), trace_scopes=False,
        )(rows_hbm, idx_hbm)

    return scatter(rows, idx.reshape(1, n))


def _moe(x, rth, rtl, rbias, eg, eu, ed, sg, su, sd, g, b, tm, alpha):
    n_tok, d = x.shape
    n_rows = TOP_K * n_tok + N_EXPERTS * ROW_TILE
    n_tiles = n_rows // ROW_TILE
    xp, e8, r8, w8, cnt = _route(x, rth, rtl, rbias, tm)
    counts = cnt[:, 0].astype(jnp.int32)
    padded = (counts + ROW_TILE - 1) // ROW_TILE * ROW_TILE
    ends = jnp.cumsum(padded)
    start = ends - padded
    n_used = (ends[-1:] // ROW_TILE).astype(jnp.int32)
    tile_ids = jnp.arange(n_tiles, dtype=jnp.int32)
    tile_expert = jnp.minimum(
        jnp.sum((tile_ids[:, None] >= (ends // ROW_TILE)[None, :]).astype(jnp.int32), axis=1),
        N_EXPERTS - 1)
    idx = _positions(start, e8, r8, n_rows, _pick_tile(n_tok, 2048)).reshape(-1)
    blocks_per_half = n_tok // SC_WINDOW
    xs = _sc_scatter_rows(
        xp.reshape(2 * n_tok, HALF // 2), idx, 2 * n_rows,
        lambda i: (i // (TOP_K * blocks_per_half)) * blocks_per_half + i % blocks_per_half)
    ys = _experts(tile_expert, n_used, xs.reshape(2, n_rows, HALF // 2), eg, eu, ed)
    yg = _sc_gather_rows(ys.reshape(2 * n_rows, HALF // 2), idx)
    return _combine(x, yg.reshape(2, TOP_K, n_tok, HALF // 2), w8, sg, su, sd, g, b, tm, alpha)


def _rope_tables(seq):
    t = jnp.arange(seq)
    row = (t // GRID_W).astype(F32)
    colp = (t % GRID_W).astype(F32)
    inv = ROPE_THETA ** (-jnp.arange(0, ROPE_AXIS_DIM, 2, dtype=F32) / ROPE_AXIS_DIM)
    ang = jnp.concatenate([row[:, None] * inv[None], colp[:, None] * inv[None]], axis=-1)
    ang = jnp.repeat(ang, 2, axis=-1)
    sign = jnp.where(jnp.arange(HEAD_DIM) % 2 == 0, -1.0, 1.0).astype(F32)
    cos = jnp.tile(jnp.cos(ang), (1, 2))
    sin = jnp.tile(jnp.sin(ang) * sign[None], (1, 2))
    return cos, sin


def _gqa_head_order():
    order = []
    for kvp in range(N_KV_HEADS // 2):
        base = 2 * GQA_GROUP * kvp
        for g in range(GQA_GROUP):
            order += [base + g, base + GQA_GROUP + g]
    return np.array(order)


def _pick_tile(n, target):
    t = min(n, target)
    while n % t:
        t //= 2
    return t


def kernel(x_prompt, x_sample, mem_prompt, mem_sample, a_w_in, a_lambda, a_subln, b_w_in, b_rpb,
           c_w_in, c_q_norm, c_k_norm, w_mem_kv, w_out, ln1_g, ln1_b, router, router_bias,
           e_gate, e_up, e_down, s_gate, s_up, s_down, ln2_g, ln2_b):
    depth = w_out.shape[0]
    alpha = (2.0 * depth) ** 0.25
    seq = x_prompt.shape[1]
    d = x_prompt.shape[2]
    assert x_sample.shape[1] == seq and seq % (NA_ROWS * GRID_W) == 0
    batch = x_prompt.shape[0] + x_sample.shape[0]
    x = jnp.concatenate([x_prompt.reshape(-1, d), x_sample.reshape(-1, d)], axis=0)
    mem = jnp.concatenate([mem_prompt, mem_sample], axis=0)
    n_mem = mem.shape[1]
    mem2d = mem.reshape(batch * n_mem, d)

    tm = _pick_tile(seq, 512)
    tk = _pick_tile(seq, 512)
    head_order = _gqa_head_order()
    col_order = (head_order[:, None] * HEAD_DIM + np.arange(HEAD_DIM)[None]).reshape(-1)
    cos, sin = _rope_tables(seq)
    slopes = _alibi_slopes(N_MIX_HEADS).reshape(N_MIX_PAIRS, 2)
    slope_tab = np.zeros((N_MIX_PAIRS, 8, LANES), np.float32)
    slope_tab[:, 0:2, :] = slopes[:, :, None]
    slope_tab = jnp.asarray(slope_tab) * LOG2E
    ones_row = functools.partial(jnp.ones, dtype=F32)
    mem_q_scale = jnp.full((MEM_WIDTH,), HEAD_DIM ** -0.5, F32)

    for i in range(depth):
        mixer, occ = i % N_MIXERS, i // N_MIXERS
        w_o = w_out[i]
        if mixer == 0:
            w_in = a_w_in[occ]
            colscale = jnp.concatenate([
                jnp.full((MIX_WIDTH,), DIFF_QK_DIM ** -0.5 * LOG2E, F32), ones_row((2 * MIX_WIDTH,)),
                mem_q_scale])
            heads = _in_projection(x, w_in.astype(BF16), colscale[None], tm, seq)
            gain = jnp.tile(a_subln[occ].astype(F32), 2)[None]
            mix = _diff_attention(heads, slope_tab, a_lambda[occ].astype(F32), gain, batch, seq,
                                  _pick_tile(seq, 256), _pick_tile(seq, 1024),
                                  _diff_lambda_init(i))
        elif mixer == 1:
            w_in = b_w_in[occ]
            colscale = jnp.concatenate([
                jnp.full((MIX_WIDTH,), HEAD_DIM ** -0.5, F32), ones_row((2 * MIX_WIDTH,)),
                mem_q_scale])
            heads = _in_projection(x, w_in.astype(BF16), colscale[None], tm, seq)
            mix = _na_attention(heads, _na_bias_table(b_rpb[occ]), batch, seq)
        else:
            w_in = c_w_in[occ]
            kvw = N_KV_HEADS * HEAD_DIM
            w_in = jnp.concatenate([w_in[:, :MIX_WIDTH][:, col_order], w_in[:, MIX_WIDTH:]], axis=1)
            w_o = jnp.concatenate([w_o[:MIX_WIDTH][col_order], w_o[MIX_WIDTH:]], axis=0)
            colscale = jnp.concatenate([
                jnp.full((MIX_WIDTH,), HEAD_DIM ** -0.5 * LOG2E, F32), ones_row((2 * kvw,)),
                mem_q_scale])
            qg = jnp.tile(c_q_norm[occ].astype(F32), 2)[None, None]
            kg = jnp.tile(c_k_norm[occ].astype(F32), 2)[None, None]
            gains = jnp.concatenate([jnp.tile(qg, (N_MIX_PAIRS, 1, 1)),
                                     jnp.tile(kg, (N_KV_HEADS // 2, 1, 1))], axis=0)
            heads = _in_projection(x, w_in.astype(BF16), colscale[None], tm, seq,
                                   rope=(gains, cos, sin))
            mix = _gqa_attention(heads, batch, seq, _pick_tile(seq, 512), tk)
        kv = _matmul(mem2d, w_mem_kv[i].astype(BF16), _pick_tile(batch * n_mem, 512))
        kv = kv.reshape(batch, n_mem, 2 * MEM_WIDTH)
        x = _out_projection(mix, heads, kv, x, w_o.astype(BF16), ln1_g[i][None], ln1_b[i][None],
                            batch, seq, tm, alpha)
        rt = router[i].T.astype(F32)
        rth = rt.astype(BF16)
        rtl = (rt - rth.astype(F32)).astype(BF16)
        x = _moe(x, rth, rtl, router_bias[i].astype(F32)[:, None],
                 e_gate[i].astype(BF16), e_up[i].astype(BF16), e_down[i].astype(BF16),
                 s_gate[i].astype(BF16), s_up[i].astype(BF16), s_down[i].astype(BF16),
                 ln2_g[i][None], ln2_b[i][None], tm, alpha)

    y = x.reshape(batch, seq, d)
    n_pb = x_prompt.shape[0]
    return (y[:n_pb], y[n_pb:])
```

```python
import functools
import math

import numpy as np
import jax
import jax.numpy as jnp
from jax import lax
from jax.experimental import pallas as pl
from jax.experimental.pallas import tpu as pltpu
from jax.experimental.pallas import tpu_sc as plsc

F32 = jnp.float32
BF16 = jnp.bfloat16

D_MODEL = 1024
HEAD_DIM = 64
LANES = 128
N_MIX_HEADS = 12
N_MIX_PAIRS = N_MIX_HEADS // 2
N_MEM_HEADS = 4
N_MEM_PAIRS = N_MEM_HEADS // 2
MIX_WIDTH = N_MIX_HEADS * HEAD_DIM
MEM_WIDTH = N_MEM_HEADS * HEAD_DIM
N_MIXERS = 3
DIFF_QK_DIM = HEAD_DIM // 2
GRID_W = 64
NA_ROWS = 8
NA_COLS = 16
N_KV_HEADS = 4
GQA_GROUP = N_MIX_HEADS // N_KV_HEADS
ROPE_THETA = 10000.0
ROPE_AXIS_DIM = HEAD_DIM // 2
N_EXPERTS = 64
TOP_K = 8
N_GROUPS = 8
TOPK_GROUPS = 4
PER_GROUP = N_EXPERTS // N_GROUPS
D_EXPERT = 256
ROUTED_SCALE = 2.5
LN_EPS = 1e-5
RMS_EPS = 1e-6
NEG_BIG = -1e30
LOG2E = math.log2(math.e)

VMEM_LIMIT = 48 * 1024 * 1024

NT_DIMS = (((1,), (1,)), ((), ()))
TN_DIMS = (((0,), (0,)), ((), ()))


def _cparams(*sem):
    return pltpu.CompilerParams(dimension_semantics=sem, vmem_limit_bytes=VMEM_LIMIT)


def _alibi_slopes(n):
    def pow2_slopes(m):
        start = 2.0 ** (-8.0 / m)
        return [start ** (i + 1) for i in range(m)]
    if math.log2(n).is_integer():
        s = pow2_slopes(n)
    else:
        c = 2 ** math.floor(math.log2(n))
        s = pow2_slopes(c) + pow2_slopes(2 * c)[0::2][: n - c]
    return np.array(s, np.float32)


def _diff_lambda_init(layer_idx):
    return 0.8 - 0.6 * math.exp(-0.3 * layer_idx)


def _lane_iota(shape):
    return lax.broadcasted_iota(jnp.int32, shape, len(shape) - 1)


def _layer_norm(y, g, b):
    mu = jnp.mean(y, axis=-1, keepdims=True)
    yc = y - mu
    var = jnp.mean(yc * yc, axis=-1, keepdims=True)
    return yc * lax.rsqrt(var + LN_EPS) * g + b


def _head_rms_scale(x):
    lo = _lane_iota(x.shape) < HEAD_DIM
    xx = x * x
    ss_lo = jnp.sum(jnp.where(lo, xx, 0.0), axis=-1, keepdims=True)
    ss_hi = jnp.sum(jnp.where(lo, 0.0, xx), axis=-1, keepdims=True)
    inv = 1.0 / HEAD_DIM
    return jnp.where(lo, lax.rsqrt(ss_lo * inv + RMS_EPS), lax.rsqrt(ss_hi * inv + RMS_EPS))


def _inproj_kernel(x_ref, w_ref, cs_ref, o_ref):
    r = jnp.dot(x_ref[...].astype(BF16), w_ref[...], preferred_element_type=F32)
    r = r * cs_ref[...]
    for p in range(o_ref.shape[0]):
        o_ref[p] = r[:, p * LANES:(p + 1) * LANES].astype(BF16)


def _inproj_rope_kernel(x_ref, w_ref, cs_ref, g_ref, cos_ref, sin_ref, o_ref, *, n_rope_pairs):
    r = jnp.dot(x_ref[...].astype(BF16), w_ref[...], preferred_element_type=F32)
    cs = cs_ref[...]
    cos = cos_ref[...]
    sin = sin_ref[...]
    even = (_lane_iota(cos.shape) % 2) == 0
    for p in range(o_ref.shape[0]):
        xp = r[:, p * LANES:(p + 1) * LANES]
        if p < n_rope_pairs:
            xn = xp * _head_rms_scale(xp) * g_ref[p]
            partner = jnp.where(even, pltpu.roll(xn, LANES - 1, 1), pltpu.roll(xn, 1, 1))
            xp = xn * cos + partner * sin
        o_ref[p] = (xp * cs[:, p * LANES:(p + 1) * LANES]).astype(BF16)


def _in_projection(x, w, colscale, tm, seq, rope=None):
    n_tok, d = x.shape
    width = w.shape[1]
    n_pairs = width // LANES
    in_specs = [
        pl.BlockSpec((tm, d), lambda i: (i, 0)),
        pl.BlockSpec((d, width), lambda i: (0, 0)),
        pl.BlockSpec((1, width), lambda i: (0, 0)),
    ]
    args = [x, w, colscale]
    if rope is None:
        body = _inproj_kernel
    else:
        gains, cos, sin = rope
        tiles_per_seq = seq // tm
        body = functools.partial(_inproj_rope_kernel, n_rope_pairs=gains.shape[0])
        in_specs += [
            pl.BlockSpec(gains.shape, lambda i: (0, 0, 0)),
            pl.BlockSpec((tm, LANES), lambda i: (i % tiles_per_seq, 0)),
            pl.BlockSpec((tm, LANES), lambda i: (i % tiles_per_seq, 0)),
        ]
        args += [gains, cos, sin]
    return pl.pallas_call(
        body,
        out_shape=jax.ShapeDtypeStruct((n_pairs, n_tok, LANES), BF16),
        grid=(n_tok // tm,),
        in_specs=in_specs,
        out_specs=pl.BlockSpec((n_pairs, tm, LANES), lambda i: (0, i, 0)),
        compiler_params=_cparams("parallel"),
        name="in_projection",
    )(*args)


def _matmul_kernel(x_ref, w_ref, o_ref):
    o_ref[...] = jnp.dot(x_ref[...].astype(BF16), w_ref[...],
                         preferred_element_type=F32).astype(o_ref.dtype)


def _matmul(x, w, tm):
    m, k = x.shape
    n = w.shape[1]
    return pl.pallas_call(
        _matmul_kernel,
        out_shape=jax.ShapeDtypeStruct((m, n), BF16),
        grid=(m // tm,),
        in_specs=[pl.BlockSpec((tm, k), lambda i: (i, 0)), pl.BlockSpec((k, n), lambda i: (0, 0))],
        out_specs=pl.BlockSpec((tm, n), lambda i: (i, 0)),
        compiler_params=_cparams("parallel"),
        name="mem_kv_projection",
    )(x, w)


FLASH_UNROLL = 2


def _stack_masked_q(q_ref, qs_ref, tq):
    n_qpairs = q_ref.shape[0]
    per_pair = qs_ref.shape[0] // tq // n_qpairs
    lanes_per_map = LANES // per_pair
    for t in range(n_qpairs):
        q = q_ref[t].astype(F32)
        part = _lane_iota(q.shape) // lanes_per_map
        for a in range(per_pair):
            r0 = (t * per_pair + a) * tq
            qs_ref[r0:r0 + tq] = jnp.where(part == a, q, 0.0).astype(BF16)


def _softmax_accumulate(s, v, m, l, acc_ref):
    m_new = jnp.maximum(m, jnp.max(s, axis=0, keepdims=True))
    alpha = jnp.exp2(m - m_new)
    p = jnp.exp2(s - m_new)
    l_new = alpha * l + jnp.sum(p, axis=0, keepdims=True)
    pv = lax.dot_general(v, p.astype(BF16), TN_DIMS, preferred_element_type=F32)
    acc_ref[...] = alpha * acc_ref[...] + pv
    return m_new, l_new


def _diff_attn_kernel(q_ref, k_ref, v_ref, slope_ref, lam_ref, g_ref, o_ref, qs_ref, acc_ref,
                      *, tq, tk, seq, lambda_init):
    i = pl.program_id(2)
    n_maps = 4
    w = n_maps * tq
    _stack_masked_q(q_ref, qs_ref, tq)
    acc_ref[...] = jnp.zeros(acc_ref.shape, F32)
    slopes = slope_ref[0]
    rel0 = (lax.broadcasted_iota(jnp.int32, (tk, tq), 0)
            - lax.broadcasted_iota(jnp.int32, (tk, tq), 1) - i * tq)

    def chunk(j, carry):
        m, l = carry
        off = pl.multiple_of(j * tk, tk)
        k = k_ref[0, pl.ds(off, tk), :]
        v = v_ref[0, pl.ds(off, tk), :]
        s = lax.dot_general(k, qs_ref[...], NT_DIMS, preferred_element_type=F32)
        dist = jnp.abs(rel0 + j * tk).astype(F32)
        s = jnp.concatenate(
            [s[:, a * tq:(a + 1) * tq] - slopes[a // 2:a // 2 + 1, 0:1] * dist
             for a in range(n_maps)], axis=1)
        return _softmax_accumulate(s, v, m, l, acc_ref)

    carry = (jnp.full((1, w), -jnp.inf, F32), jnp.zeros((1, w), F32))
    m, l = lax.fori_loop(0, seq // tk, chunk, carry, unroll=FLASH_UNROLL)

    o = acc_ref[...] / l
    lam = lam_ref[...]
    lam_full = (jnp.exp(jnp.sum(lam[0:1] * lam[1:2], axis=-1, keepdims=True))
                - jnp.exp(jnp.sum(lam[2:3] * lam[3:4], axis=-1, keepdims=True)) + lambda_init)
    h0 = o[:, 0:tq] - lam_full * o[:, tq:2 * tq]
    h1 = o[:, 2 * tq:3 * tq] - lam_full * o[:, 3 * tq:4 * tq]
    row = lax.broadcasted_iota(jnp.int32, (LANES, tq), 0)
    o = jnp.where(row < HEAD_DIM, h0, h1).T
    o = o * _head_rms_scale(o) * g_ref[...] * (1.0 - lambda_init)
    o_ref[0] = o.astype(BF16)


def _diff_attention(heads, slopes, lam, gain, batch, seq, tq, tk, lambda_init):
    n_tok = heads.shape[1]
    nq = seq // tq
    body = functools.partial(_diff_attn_kernel, tq=tq, tk=tk, seq=seq, lambda_init=lambda_init)
    return pl.pallas_call(
        body,
        out_shape=jax.ShapeDtypeStruct((N_MIX_PAIRS, n_tok, LANES), BF16),
        grid=(batch, N_MIX_PAIRS, nq),
        in_specs=[
            pl.BlockSpec((1, tq, LANES), lambda b, p, i: (p, b * nq + i, 0)),
            pl.BlockSpec((1, seq, LANES), lambda b, p, i: (N_MIX_PAIRS + p, b, 0)),
            pl.BlockSpec((1, seq, LANES), lambda b, p, i: (2 * N_MIX_PAIRS + p, b, 0)),
            pl.BlockSpec((1, 8, LANES), lambda b, p, i: (p, 0, 0)),
            pl.BlockSpec(lam.shape, lambda b, p, i: (0, 0)),
            pl.BlockSpec((1, LANES), lambda b, p, i: (0, 0)),
        ],
        out_specs=pl.BlockSpec((1, tq, LANES), lambda b, p, i: (p, b * nq + i, 0)),
        scratch_shapes=[
            pltpu.VMEM((4 * tq, LANES), BF16),
            pltpu.VMEM((LANES, 4 * tq), F32),
        ],
        compiler_params=_cparams("parallel", "parallel", "parallel"),
        name="diff_attention",
    )(heads, heads, heads, slopes, lam, gain)


def _gqa_attn_kernel(q_ref, k_ref, v_ref, o_ref, qs_ref, acc_ref, *, tq, tk, seq):
    n_qpairs = q_ref.shape[0]
    w = 2 * n_qpairs * tq
    _stack_masked_q(q_ref, qs_ref, tq)
    acc_ref[...] = jnp.zeros(acc_ref.shape, F32)

    def chunk(j, carry):
        m, l = carry
        off = pl.multiple_of(j * tk, tk)
        k = k_ref[0, pl.ds(off, tk), :]
        v = v_ref[0, pl.ds(off, tk), :]
        s = lax.dot_general(k, qs_ref[...], NT_DIMS, preferred_element_type=F32)
        return _softmax_accumulate(s, v, m, l, acc_ref)

    carry = (jnp.full((1, w), -jnp.inf, F32), jnp.zeros((1, w), F32))
    m, l = lax.fori_loop(0, seq // tk, chunk, carry, unroll=FLASH_UNROLL)
    o = acc_ref[...] / l
    row = lax.broadcasted_iota(jnp.int32, (LANES, tq), 0)
    for t in range(n_qpairs):
        lo = o[:, (2 * t) * tq:(2 * t + 1) * tq]
        hi = o[:, (2 * t + 1) * tq:(2 * t + 2) * tq]
        o_ref[t] = jnp.where(row < HEAD_DIM, lo, hi).T.astype(BF16)


def _gqa_attention(heads, batch, seq, tq, tk):
    n_tok = heads.shape[1]
    nq = seq // tq
    n_kv_pairs = N_KV_HEADS // 2
    qp = N_MIX_PAIRS // n_kv_pairs
    body = functools.partial(_gqa_attn_kernel, tq=tq, tk=tk, seq=seq)
    return pl.pallas_call(
        body,
        out_shape=jax.ShapeDtypeStruct((N_MIX_PAIRS, n_tok, LANES), BF16),
        grid=(batch, n_kv_pairs, nq),
        in_specs=[
            pl.BlockSpec((qp, tq, LANES), lambda b, p, i: (p, b * nq + i, 0)),
            pl.BlockSpec((1, seq, LANES), lambda b, p, i: (N_MIX_PAIRS + p, b, 0)),
            pl.BlockSpec((1, seq, LANES), lambda b, p, i: (N_MIX_PAIRS + n_kv_pairs + p, b, 0)),
        ],
        out_specs=pl.BlockSpec((qp, tq, LANES), lambda b, p, i: (p, b * nq + i, 0)),
        scratch_shapes=[
            pltpu.VMEM((2 * qp * tq, LANES), BF16),
            pltpu.VMEM((LANES, 2 * qp * tq), F32),
        ],
        compiler_params=_cparams("parallel", "parallel", "parallel"),
        name="gqa_attention",
    )(heads, heads, heads)


def _na_attn_kernel(q_ref, kp_ref, kc_ref, kn_ref, vp_ref, vc_ref, vn_ref, bias_ref, o_ref,
                    kbuf_ref, vbuf_ref, *, n_grid_rows):
    rb = pl.program_id(2)
    blk = NA_ROWS * GRID_W
    win = NA_ROWS * GRID_W
    kbuf_ref[0:blk] = kp_ref[0]
    kbuf_ref[blk:2 * blk] = kc_ref[0]
    kbuf_ref[2 * blk:3 * blk] = kn_ref[0]
    vbuf_ref[0:blk] = vp_ref[0]
    vbuf_ref[blk:2 * blk] = vc_ref[0]
    vbuf_ref[2 * blk:3 * blk] = vn_ref[0]
    lo = _lane_iota((GRID_W, LANES)) < HEAD_DIM
    for u in range(NA_ROWS):
        r = rb * NA_ROWS + u
        rs = jnp.clip(r - NA_ROWS // 2, 0, n_grid_rows - NA_ROWS)
        off = pl.multiple_of((rs - (rb - 1) * NA_ROWS) * GRID_W, GRID_W)
        ro0 = rs - r + NA_ROWS - 1
        kw = kbuf_ref[pl.ds(off, win)]
        vw = vbuf_ref[pl.ds(off, win)]
        q = q_ref[0, u * GRID_W:(u + 1) * GRID_W].astype(F32)
        qs = jnp.concatenate([jnp.where(lo, q, 0.0), jnp.where(lo, 0.0, q)], axis=0).astype(BF16)
        s = lax.dot_general(qs, kw, NT_DIMS, preferred_element_type=F32)
        bias = jnp.concatenate(
            [jnp.concatenate([bias_ref[h, ro0 + 2 * t] for t in range(NA_ROWS // 2)], axis=1)
             for h in range(2)], axis=0)
        s = s + bias
        m = jnp.max(s, axis=-1, keepdims=True)
        p = jnp.exp(s - m)
        l = jnp.sum(p, axis=-1, keepdims=True)
        o = jnp.dot(p.astype(BF16), vw, preferred_element_type=F32) / l
        o_ref[0, u * GRID_W:(u + 1) * GRID_W] = jnp.where(lo, o[0:GRID_W], o[GRID_W:]).astype(BF16)


def _na_attention(heads, bias, batch, seq):
    n_tok = heads.shape[1]
    n_grid_rows = seq // GRID_W
    nrb = n_grid_rows // NA_ROWS
    blk = NA_ROWS * GRID_W

    def kv_spec(first_pair, shift):
        return pl.BlockSpec(
            (1, blk, LANES),
            lambda b, p, rb: (first_pair + p, b * nrb + jnp.clip(rb + shift, 0, nrb - 1), 0))

    body = functools.partial(_na_attn_kernel, n_grid_rows=n_grid_rows)
    return pl.pallas_call(
        body,
        out_shape=jax.ShapeDtypeStruct((N_MIX_PAIRS, n_tok, LANES), BF16),
        grid=(batch, N_MIX_PAIRS, nrb),
        in_specs=[
            pl.BlockSpec((1, blk, LANES), lambda b, p, rb: (p, b * nrb + rb, 0)),
            kv_spec(N_MIX_PAIRS, -1), kv_spec(N_MIX_PAIRS, 0), kv_spec(N_MIX_PAIRS, 1),
            kv_spec(2 * N_MIX_PAIRS, -1), kv_spec(2 * N_MIX_PAIRS, 0), kv_spec(2 * N_MIX_PAIRS, 1),
            pl.BlockSpec((2,) + bias.shape[1:], lambda b, p, rb: (p, 0, 0, 0)),
        ],
        out_specs=pl.BlockSpec((1, blk, LANES), lambda b, p, rb: (p, b * nrb + rb, 0)),
        scratch_shapes=[pltpu.VMEM((3 * blk, LANES), BF16), pltpu.VMEM((3 * blk, LANES), BF16)],
        compiler_params=_cparams("parallel", "parallel", "arbitrary"),
        name="neighbourhood_attention",
    )(heads, heads, heads, heads, heads, heads, heads, bias)


def _na_bias_table(rpb):
    col = jnp.arange(GRID_W)
    col_start = jnp.clip(col - NA_COLS // 2, 0, GRID_W - NA_COLS)
    col_mask = (col[None, :] >= col_start[:, None]) & (col[None, :] < col_start[:, None] + NA_COLS)
    col_bias_idx = jnp.clip(col[None, :] - col[:, None] + NA_COLS - 1, 0, 2 * NA_COLS - 2)
    tiles = rpb.astype(F32)[:, :, col_bias_idx]
    tiles = jnp.where(col_mask[None, None], tiles, NEG_BIG)
    return jnp.concatenate([tiles[:, :-1], tiles[:, 1:]], axis=-1)


def _outproj_kernel(mix_ref, qm_ref, kv_ref, x_ref, w_ref, g_ref, b_ref, o_ref, *, alpha):
    tm = x_ref.shape[0]
    lo = _lane_iota((tm, LANES)) < HEAD_DIM
    parts = [mix_ref[p] for p in range(mix_ref.shape[0])]
    kv = kv_ref[0]
    for t in range(N_MEM_PAIRS):
        q = qm_ref[t].astype(F32)
        qs = jnp.concatenate([jnp.where(lo, q, 0.0), jnp.where(lo, 0.0, q)], axis=0).astype(BF16)
        kt = kv[:, t * LANES:(t + 1) * LANES]
        vt = kv[:, MEM_WIDTH + t * LANES:MEM_WIDTH + (t + 1) * LANES]
        s = lax.dot_general(qs, kt, NT_DIMS, preferred_element_type=F32)
        m = jnp.max(s, axis=-1, keepdims=True)
        p = jnp.exp(s - m)
        l = jnp.sum(p, axis=-1, keepdims=True)
        o = jnp.dot(p.astype(BF16), vt, preferred_element_type=F32) / l
        parts.append(jnp.where(lo, o[0:tm], o[tm:]).astype(BF16))
    attn = jnp.concatenate(parts, axis=1)
    h = jnp.dot(attn, w_ref[...], preferred_element_type=F32)
    o_ref[...] = _layer_norm(alpha * x_ref[...] + h, g_ref[...], b_ref[...])


def _out_projection(mix, heads, kv, x, w_out, g, b, batch, seq, tm, alpha):
    n_tok, d = x.shape
    nt = seq // tm
    n_pairs = heads.shape[0]
    n_mem = kv.shape[1]
    body = functools.partial(_outproj_kernel, alpha=alpha)
    return pl.pallas_call(
        body,
        out_shape=jax.ShapeDtypeStruct((n_tok, d), F32),
        grid=(batch, nt),
        in_specs=[
            pl.BlockSpec((N_MIX_PAIRS, tm, LANES), lambda bi, i: (0, bi * nt + i, 0)),
            pl.BlockSpec((N_MEM_PAIRS, tm, LANES),
                         lambda bi, i: (n_pairs // N_MEM_PAIRS - 1, bi * nt + i, 0)),
            pl.BlockSpec((1, n_mem, 2 * MEM_WIDTH), lambda bi, i: (bi, 0, 0)),
            pl.BlockSpec((tm, d), lambda bi, i: (bi * nt + i, 0)),
            pl.BlockSpec(w_out.shape, lambda bi, i: (0, 0)),
            pl.BlockSpec((1, d), lambda bi, i: (0, 0)),
            pl.BlockSpec((1, d), lambda bi, i: (0, 0)),
        ],
        out_specs=pl.BlockSpec((tm, d), lambda bi, i: (bi * nt + i, 0)),
        compiler_params=_cparams("parallel", "parallel"),
        name="mem_attention_out_projection",
    )(mix, heads, kv, x, w_out, g, b)


ROW_TILE = 512
SC_WINDOW = 128
HALF = D_MODEL // 2


def _bf16_bits(v):
    u = lax.bitcast_convert_type(v, jnp.uint32)
    return (u + jnp.uint32(0x7FFF) + ((u >> 16) & jnp.uint32(1))) >> 16


def _pack_half(v):
    q = HALF // 2
    return _bf16_bits(v[:, :q]) | (_bf16_bits(v[:, q:]) << 16)


def _unpack_half(w):
    lo = lax.bitcast_convert_type(w << 16, F32)
    hi = lax.bitcast_convert_type(w & jnp.uint32(0xFFFF0000), F32)
    return jnp.concatenate([lo, hi], axis=1)

def _first_index_of_max(vals, index, sentinel, axes):
    mx = vals
    for ax in axes:
        mx = jnp.max(mx, axis=ax, keepdims=True)
    idx = jnp.where(vals == mx, index, sentinel)
    for ax in axes:
        idx = jnp.min(idx, axis=ax, keepdims=True)
    return mx, idx


def _route_kernel(x_ref, rth_ref, rtl_ref, rbias_ref, tri_ref,
                  xp_ref, e8_ref, r8_ref, w8_ref, cnt_ref, run_ref):
    @pl.when(pl.program_id(0) == 0)
    def _():
        run_ref[...] = jnp.zeros(run_ref.shape, F32)

    x = x_ref[...]
    tm = x.shape[0]
    xp_ref[0] = _pack_half(x[:, :HALF])
    xp_ref[1] = _pack_half(x[:, HALF:])
    xh = x.astype(BF16)
    xl = (x - xh.astype(F32)).astype(BF16)
    rth = rth_ref[...]
    logits = (lax.dot_general(rth, xh, NT_DIMS, preferred_element_type=F32)
              + lax.dot_general(rth, xl, NT_DIMS, preferred_element_type=F32)
              + lax.dot_general(rtl_ref[...], xh, NT_DIMS, preferred_element_type=F32))
    scores = jax.nn.sigmoid(logits)
    choice = scores + rbias_ref[...]
    shape3 = (N_GROUPS, PER_GROUP, tm)
    c3 = choice.reshape(shape3)
    s3 = scores.reshape(shape3)
    e_in_g = lax.broadcasted_iota(jnp.int32, shape3, 1).astype(F32)
    g_idx = lax.broadcasted_iota(jnp.int32, (N_GROUPS, 1, tm), 0).astype(F32)
    e_idx = lax.broadcasted_iota(jnp.int32, shape3, 0).astype(F32) * PER_GROUP + e_in_g
    m1, i1 = _first_index_of_max(c3, e_in_g, float(PER_GROUP), (1,))
    m2 = jnp.max(jnp.where(e_in_g == i1, -jnp.inf, c3), axis=1, keepdims=True)
    cur = m1 + m2
    gsel = jnp.zeros(cur.shape, F32)
    for _ in range(TOPK_GROUPS):
        _, gi = _first_index_of_max(cur, g_idx, float(N_GROUPS), (0,))
        hit = g_idx == gi
        gsel = jnp.where(hit, 1.0, gsel)
        cur = jnp.where(hit, -jnp.inf, cur)
    cur = jnp.where(gsel > 0.0, c3, -jnp.inf)
    sel = jnp.zeros(shape3, F32)
    picks = []
    for _ in range(TOP_K):
        _, ei = _first_index_of_max(cur, e_idx, float(N_EXPERTS), (1, 0))
        hit = e_idx == ei
        sel = jnp.where(hit, 1.0, sel)
        cur = jnp.where(hit, -jnp.inf, cur)
        picks.append(ei)
    w = sel * s3
    denom = jnp.sum(jnp.sum(w, axis=1, keepdims=True), axis=0, keepdims=True)
    gates3 = w / denom * ROUTED_SCALE
    sel2 = sel.reshape(N_EXPERTS, tm)
    before = jnp.dot(sel2.astype(BF16), tri_ref[...], preferred_element_type=F32)
    rank3 = (before + run_ref[:, 0:1]).reshape(shape3)
    run_ref[...] = run_ref[...] + jnp.sum(sel2, axis=1, keepdims=True)
    cnt_ref[...] = run_ref[...]

    def pick(values, hit):
        return jnp.sum(jnp.sum(jnp.where(hit, values, 0.0), axis=1, keepdims=True), axis=0)

    for k, ei in enumerate(picks):
        hit = e_idx == ei
        e8_ref[k:k + 1, :] = ei.reshape(1, tm).astype(jnp.int32)
        r8_ref[k:k + 1, :] = pick(rank3, hit).astype(jnp.int32)
        w8_ref[k:k + 1, :] = pick(gates3, hit)


def _route(x, rth, rtl, rbias, tm):
    n_tok, d = x.shape
    tri = (jnp.arange(tm)[:, None] < jnp.arange(tm)[None, :]).astype(BF16)
    const2 = lambda i: (0, 0)
    return pl.pallas_call(
        _route_kernel,
        out_shape=(
            jax.ShapeDtypeStruct((2, n_tok, HALF // 2), jnp.uint32),
            jax.ShapeDtypeStruct((TOP_K, n_tok), jnp.int32),
            jax.ShapeDtypeStruct((TOP_K, n_tok), jnp.int32),
            jax.ShapeDtypeStruct((TOP_K, n_tok), F32),
            jax.ShapeDtypeStruct((N_EXPERTS, LANES), F32),
        ),
        grid=(n_tok // tm,),
        in_specs=[
            pl.BlockSpec((tm, d), lambda i: (i, 0)),
            pl.BlockSpec(rth.shape, const2),
            pl.BlockSpec(rtl.shape, const2),
            pl.BlockSpec(rbias.shape, const2),
            pl.BlockSpec((tm, tm), const2),
        ],
        out_specs=(
            pl.BlockSpec((2, tm, HALF // 2), lambda i: (0, i, 0)),
            pl.BlockSpec((TOP_K, tm), lambda i: (0, i)),
            pl.BlockSpec((TOP_K, tm), lambda i: (0, i)),
            pl.BlockSpec((TOP_K, tm), lambda i: (0, i)),
            pl.BlockSpec((N_EXPERTS, LANES), const2),
        ),
        scratch_shapes=[pltpu.VMEM((N_EXPERTS, LANES), F32)],
        compiler_params=_cparams("arbitrary"),
        name="moe_router",
    )(x, rth, rtl, rbias, tri)


def _positions_kernel(start_ref, e8_ref, r8_ref, idx_ref, *, n_rows):
    e8 = e8_ref[...]
    pos = r8_ref[...]
    for e in range(N_EXPERTS):
        pos = pos + jnp.where(e8 == e, start_ref[e], 0)
    idx_ref[0] = pos
    idx_ref[1] = pos + n_rows


def _positions(start, e8, r8, n_rows, tn):
    n_tok = e8.shape[1]
    grid_spec = pltpu.PrefetchScalarGridSpec(
        num_scalar_prefetch=1,
        grid=(n_tok // tn,),
        in_specs=[pl.BlockSpec((TOP_K, tn), lambda i, st: (0, i)),
                  pl.BlockSpec((TOP_K, tn), lambda i, st: (0, i))],
        out_specs=pl.BlockSpec((2, TOP_K, tn), lambda i, st: (0, 0, i)),
    )
    return pl.pallas_call(
        functools.partial(_positions_kernel, n_rows=n_rows),
        out_shape=jax.ShapeDtypeStruct((2, TOP_K, n_tok), jnp.int32),
        grid_spec=grid_spec,
        compiler_params=_cparams("parallel"),
        name="moe_positions",
    )(start, e8, r8)


def _experts_kernel(te_ref, used_ref, xs_ref, wg_ref, wu_ref, wd_ref, ys_ref):
    @pl.when(pl.program_id(0) < used_ref[0])
    def _():
        x = jnp.concatenate([_unpack_half(xs_ref[0]), _unpack_half(xs_ref[1])], axis=1).astype(BF16)
        h = (jax.nn.silu(jnp.dot(x, wg_ref[0], preferred_element_type=F32))
             * jnp.dot(x, wu_ref[0], preferred_element_type=F32))
        y = jnp.dot(h.astype(BF16), wd_ref[0], preferred_element_type=F32)
        ys_ref[0] = _pack_half(y[:, :HALF])
        ys_ref[1] = _pack_half(y[:, HALF:])


def _experts(tile_expert, n_used, xs, eg, eu, ed):
    n_rows = xs.shape[1]
    _, d, f = eg.shape
    grid_spec = pltpu.PrefetchScalarGridSpec(
        num_scalar_prefetch=2,
        grid=(n_rows // ROW_TILE,),
        in_specs=[
            pl.BlockSpec((2, ROW_TILE, HALF // 2), lambda i, te, nu: (0, i, 0)),
            pl.BlockSpec((1, d, f), lambda i, te, nu: (te[i], 0, 0)),
            pl.BlockSpec((1, d, f), lambda i, te, nu: (te[i], 0, 0)),
            pl.BlockSpec((1, f, d), lambda i, te, nu: (te[i], 0, 0)),
        ],
        out_specs=pl.BlockSpec((2, ROW_TILE, HALF // 2), lambda i, te, nu: (0, i, 0)),
    )
    return pl.pallas_call(
        _experts_kernel,
        out_shape=jax.ShapeDtypeStruct(xs.shape, jnp.uint32),
        grid_spec=grid_spec,
        compiler_params=_cparams("parallel"),
        name="moe_experts",
    )(tile_expert, n_used, xs, eg, eu, ed)


def _combine_kernel(x_ref, yg_ref, w8_ref, sg_ref, su_ref, sd_ref, g_ref, b_ref, o_ref, *, alpha):
    x = x_ref[...]
    tm = x.shape[0]
    xh = x.astype(BF16)
    hs = (jax.nn.silu(jnp.dot(xh, sg_ref[...], preferred_element_type=F32))
          * jnp.dot(xh, su_ref[...], preferred_element_type=F32))
    acc = jnp.dot(hs.astype(BF16), sd_ref[...], preferred_element_type=F32)
    w8 = jnp.concatenate([w8_ref[...], jnp.zeros((LANES - TOP_K, tm), F32)], axis=0).T
    for k in range(TOP_K):
        yk = jnp.concatenate([_unpack_half(yg_ref[0, k]), _unpack_half(yg_ref[1, k])], axis=1)
        acc = acc + w8[:, k:k + 1] * yk
    o_ref[...] = _layer_norm(alpha * x + acc, g_ref[...], b_ref[...])


def _combine(x, yg, w8, sg, su, sd, g, b, tm, alpha):
    n_tok, d = x.shape
    const2 = lambda i: (0, 0)
    return pl.pallas_call(
        functools.partial(_combine_kernel, alpha=alpha),
        out_shape=jax.ShapeDtypeStruct((n_tok, d), F32),
        grid=(n_tok // tm,),
        in_specs=[
            pl.BlockSpec((tm, d), lambda i: (i, 0)),
            pl.BlockSpec((2, TOP_K, tm, HALF // 2), lambda i: (0, 0, i, 0)),
            pl.BlockSpec((TOP_K, tm), lambda i: (0, i)),
            pl.BlockSpec(sg.shape, const2),
            pl.BlockSpec(su.shape, const2),
            pl.BlockSpec(sd.shape, const2),
            pl.BlockSpec((1, d), const2),
            pl.BlockSpec((1, d), const2),
        ],
        out_specs=pl.BlockSpec((tm, d), lambda i: (i, 0)),
        compiler_params=_cparams("parallel"),
        name="moe_combine",
    )(x, yg, w8, sg, su, sd, g, b)


def _sc_mesh():
    return plsc.VectorSubcoreMesh(core_axis_name="c", subcore_axis_name="s")


def _sc_gather_rows(table, idx):
    n = idx.shape[0]
    d = table.shape[1]

    @functools.partial(pl.kernel, out_type=jax.ShapeDtypeStruct((n, d), table.dtype),
                       mesh=_sc_mesh())
    def gather(table_hbm, idx_hbm, out_hbm):
        def body(idx_vmem, out_vmem):
            pltpu.sync_copy(table_hbm.at[idx_vmem.at[0]], out_vmem)

        pltpu.emit_pipeline(
            body, grid=(n // SC_WINDOW,),
            in_specs=[pl.BlockSpec((1, SC_WINDOW), index_map=lambda i: (0, i))],
            out_specs=[pl.BlockSpec((SC_WINDOW, d), index_map=lambda i: (i, 0))],
            core_axis_name=("c", "s"), dimension_semantics=(pltpu.PARALLEL,), trace_scopes=False,
        )(idx_hbm, out_hbm)

    return gather(table, idx.reshape(1, n))


def _sc_scatter_rows(rows, idx, n_out, src_block):
    n = idx.shape[0]
    d = rows.shape[1]

    @functools.partial(pl.kernel, out_type=jax.ShapeDtypeStruct((n_out, d), rows.dtype),
                       mesh=_sc_mesh())
    def scatter(rows_hbm, idx_hbm, out_hbm):
        def body(rows_vmem, idx_vmem):
            pltpu.sync_copy(rows_vmem, out_hbm.at[idx_vmem.at[0]])

        pltpu.emit_pipeline(
            body, grid=(n // SC_WINDOW,),
            in_specs=[pl.BlockSpec((SC_WINDOW, d), index_map=lambda i: (src_block(i), 0)),
                      pl.BlockSpec((1, SC_WINDOW), index_map=lambda i: (0, i))],
            out_specs=[],
            core_axis_name=("c", "s"), dimension_semantics=(pltpu.PARALLEL,), trace_scopes=False,
        )(rows_hbm, idx_hbm)

    return scatter(rows, idx.reshape(1, n))


def _moe_dispatch(x, rth, rtl, rbias, tm):
    n_tok, d = x.shape
    n_rows = TOP_K * n_tok + N_EXPERTS * ROW_TILE
    n_tiles = n_rows // ROW_TILE
    xp, e8, r8, w8, cnt = _route(x, rth, rtl, rbias, tm)
    counts = cnt[:, 0].astype(jnp.int32)
    padded = (counts + ROW_TILE - 1) // ROW_TILE * ROW_TILE
    ends = jnp.cumsum(padded)
    start = ends - padded
    n_used = (ends[-1:] // ROW_TILE).astype(jnp.int32)
    tile_ids = jnp.arange(n_tiles, dtype=jnp.int32)
    tile_expert = jnp.minimum(
        jnp.sum((tile_ids[:, None] >= (ends // ROW_TILE)[None, :]).astype(jnp.int32), axis=1),
        N_EXPERTS - 1)
    idx = _positions(start, e8, r8, n_rows, _pick_tile(n_tok, 2048)).reshape(-1)
    blocks_per_half = n_tok // SC_WINDOW
    xs = _sc_scatter_rows(
        xp.reshape(2 * n_tok, HALF // 2), idx, 2 * n_rows,
        lambda i: (i // (TOP_K * blocks_per_half)) * blocks_per_half + i % blocks_per_half)
    return xs.reshape(2, n_rows, HALF // 2), idx, tile_expert, n_used, w8


def _moe_experts(xs, idx, tile_expert, n_used, eg, eu, ed):
    n_rows = xs.shape[1]
    ys = _experts(tile_expert, n_used, xs, eg, eu, ed)
    yg = _sc_gather_rows(ys.reshape(2 * n_rows, HALF // 2), idx)
    return yg.reshape(2, TOP_K, idx.shape[0] // (2 * TOP_K), HALF // 2)


def _rope_tables(seq):
    t = jnp.arange(seq)
    row = (t // GRID_W).astype(F32)
    colp = (t % GRID_W).astype(F32)
    inv = ROPE_THETA ** (-jnp.arange(0, ROPE_AXIS_DIM, 2, dtype=F32) / ROPE_AXIS_DIM)
    ang = jnp.concatenate([row[:, None] * inv[None], colp[:, None] * inv[None]], axis=-1)
    ang = jnp.repeat(ang, 2, axis=-1)
    sign = jnp.where(jnp.arange(HEAD_DIM) % 2 == 0, -1.0, 1.0).astype(F32)
    cos = jnp.tile(jnp.cos(ang), (1, 2))
    sin = jnp.tile(jnp.sin(ang) * sign[None], (1, 2))
    return cos, sin


def _gqa_head_order():
    order = []
    for kvp in range(N_KV_HEADS // 2):
        base = 2 * GQA_GROUP * kvp
        for g in range(GQA_GROUP):
            order += [base + g, base + GQA_GROUP + g]
    return np.array(order)


def _pick_tile(n, target):
    t = min(n, target)
    while n % t:
        t //= 2
    return t


def kernel(x_prompt, x_sample, mem_prompt, mem_sample, a_w_in, a_lambda, a_subln, b_w_in, b_rpb,
           c_w_in, c_q_norm, c_k_norm, w_mem_kv, w_out, ln1_g, ln1_b, router, router_bias,
           e_gate, e_up, e_down, s_gate, s_up, s_down, ln2_g, ln2_b):
    depth = w_out.shape[0]
    alpha = (2.0 * depth) ** 0.25
    seq = x_prompt.shape[1]
    d = x_prompt.shape[2]
    assert x_sample.shape[1] == seq and seq % (NA_ROWS * GRID_W) == 0
    n_mem = mem_prompt.shape[1]
    batches = [x_prompt.shape[0], x_sample.shape[0]]
    xs_ = [x_prompt.reshape(-1, d), x_sample.reshape(-1, d)]
    mems = [mem_prompt.reshape(-1, d), mem_sample.reshape(-1, d)]

    tm = _pick_tile(seq, 512)
    tk = _pick_tile(seq, 512)
    head_order = _gqa_head_order()
    col_order = (head_order[:, None] * HEAD_DIM + np.arange(HEAD_DIM)[None]).reshape(-1)
    cos, sin = _rope_tables(seq)
    slopes = _alibi_slopes(N_MIX_HEADS).reshape(N_MIX_PAIRS, 2)
    slope_tab = np.zeros((N_MIX_PAIRS, 8, LANES), np.float32)
    slope_tab[:, 0:2, :] = slopes[:, :, None]
    slope_tab = jnp.asarray(slope_tab) * LOG2E
    ones_row = functools.partial(jnp.ones, dtype=F32)
    mem_q_scale = jnp.full((MEM_WIDTH,), HEAD_DIM ** -0.5, F32)

    for i in range(depth):
        mixer, occ = i % N_MIXERS, i // N_MIXERS
        w_o = w_out[i]
        rope = None
        if mixer == 0:
            w_in = a_w_in[occ]
            q_scale = DIFF_QK_DIM ** -0.5 * LOG2E
            gain = jnp.tile(a_subln[occ].astype(F32), 2)[None]
            lam = a_lambda[occ].astype(F32)
        elif mixer == 1:
            w_in = b_w_in[occ]
            q_scale = HEAD_DIM ** -0.5
            na_bias = _na_bias_table(b_rpb[occ])
        else:
            w_in = c_w_in[occ]
            w_in = jnp.concatenate([w_in[:, :MIX_WIDTH][:, col_order], w_in[:, MIX_WIDTH:]], axis=1)
            w_o = jnp.concatenate([w_o[:MIX_WIDTH][col_order], w_o[MIX_WIDTH:]], axis=0)
            q_scale = HEAD_DIM ** -0.5 * LOG2E
            qg = jnp.tile(c_q_norm[occ].astype(F32), 2)[None, None]
            kg = jnp.tile(c_k_norm[occ].astype(F32), 2)[None, None]
            gains = jnp.concatenate([jnp.tile(qg, (N_MIX_PAIRS, 1, 1)),
                                     jnp.tile(kg, (N_KV_HEADS // 2, 1, 1))], axis=0)
            rope = (gains, cos, sin)
        colscale = jnp.concatenate([
            jnp.full((MIX_WIDTH,), q_scale, F32),
            ones_row((w_in.shape[1] - MIX_WIDTH - MEM_WIDTH,)), mem_q_scale])[None]
        w_in = w_in.astype(BF16)
        w_o = w_o.astype(BF16)
        w_kv = w_mem_kv[i].astype(BF16)
        rt = router[i].T.astype(F32)
        rth = rt.astype(BF16)
        rtl = (rt - rth.astype(F32)).astype(BF16)
        rbias = router_bias[i].astype(F32)[:, None]
        eg, eu, ed = e_gate[i].astype(BF16), e_up[i].astype(BF16), e_down[i].astype(BF16)
        sg, su, sd = s_gate[i].astype(BF16), s_up[i].astype(BF16), s_down[i].astype(BF16)

        dispatched = []
        for s in range(len(xs_)):
            x, batch = xs_[s], batches[s]
            heads = _in_projection(x, w_in, colscale, tm, seq, rope=rope)
            if mixer == 0:
                mix = _diff_attention(heads, slope_tab, lam, gain, batch, seq,
                                      _pick_tile(seq, 256), _pick_tile(seq, 1024),
                                      _diff_lambda_init(i))
            elif mixer == 1:
                mix = _na_attention(heads, na_bias, batch, seq)
            else:
                mix = _gqa_attention(heads, batch, seq, _pick_tile(seq, 512), tk)
            kv = _matmul(mems[s], w_kv, _pick_tile(batch * n_mem, 512))
            kv = kv.reshape(batch, n_mem, 2 * MEM_WIDTH)
            x = _out_projection(mix, heads, kv, x, w_o, ln1_g[i][None], ln1_b[i][None],
                                batch, seq, tm, alpha)
            xs_[s] = x
            dispatched.append(_moe_dispatch(x, rth, rtl, rbias, tm))
        gathered = [_moe_experts(*disp[:4], eg, eu, ed) for disp in dispatched]
        for s in range(len(xs_)):
            xs_[s] = _combine(xs_[s], gathered[s], dispatched[s][4], sg, su, sd,
                              ln2_g[i][None], ln2_b[i][None], tm, alpha)

    return tuple(x.reshape(b, seq, d) for x, b in zip(xs_, batches))
```

```python
import functools
import math

import numpy as np
import jax
import jax.numpy as jnp
from jax import lax
from jax.experimental import pallas as pl
from jax.experimental.pallas import tpu as pltpu
from jax.experimental.pallas import tpu_sc as plsc

F32 = jnp.float32
BF16 = jnp.bfloat16

D_MODEL = 1024
HEAD_DIM = 64
LANES = 128
N_MIX_HEADS = 12
N_MIX_PAIRS = N_MIX_HEADS // 2
N_MEM_HEADS = 4
N_MEM_PAIRS = N_MEM_HEADS // 2
MIX_WIDTH = N_MIX_HEADS * HEAD_DIM
MEM_WIDTH = N_MEM_HEADS * HEAD_DIM
N_MIXERS = 3
DIFF_QK_DIM = HEAD_DIM // 2
GRID_W = 64
NA_ROWS = 8
NA_COLS = 16
N_KV_HEADS = 4
GQA_GROUP = N_MIX_HEADS // N_KV_HEADS
ROPE_THETA = 10000.0
ROPE_AXIS_DIM = HEAD_DIM // 2
N_EXPERTS = 64
TOP_K = 8
N_GROUPS = 8
TOPK_GROUPS = 4
PER_GROUP = N_EXPERTS // N_GROUPS
D_EXPERT = 256
ROUTED_SCALE = 2.5
LN_EPS = 1e-5
RMS_EPS = 1e-6
NEG_BIG = -1e30
LOG2E = math.log2(math.e)

VMEM_LIMIT = 48 * 1024 * 1024

NT_DIMS = (((1,), (1,)), ((), ()))
TN_DIMS = (((0,), (0,)), ((), ()))


def _cparams(*sem):
    return pltpu.CompilerParams(dimension_semantics=sem, vmem_limit_bytes=VMEM_LIMIT)


def _alibi_slopes(n):
    def pow2_slopes(m):
        start = 2.0 ** (-8.0 / m)
        return [start ** (i + 1) for i in range(m)]
    if math.log2(n).is_integer():
        s = pow2_slopes(n)
    else:
        c = 2 ** math.floor(math.log2(n))
        s = pow2_slopes(c) + pow2_slopes(2 * c)[0::2][: n - c]
    return np.array(s, np.float32)


def _diff_lambda_init(layer_idx):
    return 0.8 - 0.6 * math.exp(-0.3 * layer_idx)


def _lane_iota(shape):
    return lax.broadcasted_iota(jnp.int32, shape, len(shape) - 1)


def _layer_norm(y, g, b):
    mu = jnp.mean(y, axis=-1, keepdims=True)
    yc = y - mu
    var = jnp.mean(yc * yc, axis=-1, keepdims=True)
    return yc * lax.rsqrt(var + LN_EPS) * g + b


def _head_rms_scale(x):
    lo = _lane_iota(x.shape) < HEAD_DIM
    xx = x * x
    ss_lo = jnp.sum(jnp.where(lo, xx, 0.0), axis=-1, keepdims=True)
    ss_hi = jnp.sum(jnp.where(lo, 0.0, xx), axis=-1, keepdims=True)
    inv = 1.0 / HEAD_DIM
    return jnp.where(lo, lax.rsqrt(ss_lo * inv + RMS_EPS), lax.rsqrt(ss_hi * inv + RMS_EPS))


def _inproj_kernel(x_ref, w_ref, cs_ref, o_ref):
    r = jnp.dot(x_ref[...].astype(BF16), w_ref[...], preferred_element_type=F32)
    r = r * cs_ref[...]
    for p in range(o_ref.shape[0]):
        o_ref[p] = r[:, p * LANES:(p + 1) * LANES].astype(BF16)


def _inproj_rope_kernel(x_ref, w_ref, cs_ref, g_ref, cos_ref, sin_ref, o_ref, *, n_rope_pairs):
    r = jnp.dot(x_ref[...].astype(BF16), w_ref[...], preferred_element_type=F32)
    cs = cs_ref[...]
    cos = cos_ref[...]
    sin = sin_ref[...]
    even = (_lane_iota(cos.shape) % 2) == 0
    for p in range(o_ref.shape[0]):
        xp = r[:, p * LANES:(p + 1) * LANES]
        if p < n_rope_pairs:
            xn = xp * _head_rms_scale(xp) * g_ref[p]
            partner = jnp.where(even, pltpu.roll(xn, LANES - 1, 1), pltpu.roll(xn, 1, 1))
            xp = xn * cos + partner * sin
        o_ref[p] = (xp * cs[:, p * LANES:(p + 1) * LANES]).astype(BF16)


def _in_projection(x, w, colscale, tm, seq, rope=None):
    n_tok, d = x.shape
    width = w.shape[1]
    n_pairs = width // LANES
    in_specs = [
        pl.BlockSpec((tm, d), lambda i: (i, 0)),
        pl.BlockSpec((d, width), lambda i: (0, 0)),
        pl.BlockSpec((1, width), lambda i: (0, 0)),
    ]
    args = [x, w, colscale]
    if rope is None:
        body = _inproj_kernel
    else:
        gains, cos, sin = rope
        tiles_per_seq = seq // tm
        body = functools.partial(_inproj_rope_kernel, n_rope_pairs=gains.shape[0])
        in_specs += [
            pl.BlockSpec(gains.shape, lambda i: (0, 0, 0)),
            pl.BlockSpec((tm, LANES), lambda i: (i % tiles_per_seq, 0)),
            pl.BlockSpec((tm, LANES), lambda i: (i % tiles_per_seq, 0)),
        ]
        args += [gains, cos, sin]
    return pl.pallas_call(
        body,
        out_shape=jax.ShapeDtypeStruct((n_pairs, n_tok, LANES), BF16),
        grid=(n_tok // tm,),
        in_specs=in_specs,
        out_specs=pl.BlockSpec((n_pairs, tm, LANES), lambda i: (0, i, 0)),
        compiler_params=_cparams("parallel"),
        name="in_projection",
    )(*args)


def _matmul_kernel(x_ref, w_ref, o_ref):
    o_ref[...] = jnp.dot(x_ref[...].astype(BF16), w_ref[...],
                         preferred_element_type=F32).astype(o_ref.dtype)


def _matmul(x, w, tm):
    m, k = x.shape
    n = w.shape[1]
    return pl.pallas_call(
        _matmul_kernel,
        out_shape=jax.ShapeDtypeStruct((m, n), BF16),
        grid=(m // tm,),
        in_specs=[pl.BlockSpec((tm, k), lambda i: (i, 0)), pl.BlockSpec((k, n), lambda i: (0, 0))],
        out_specs=pl.BlockSpec((tm, n), lambda i: (i, 0)),
        compiler_params=_cparams("parallel"),
        name="mem_kv_projection",
    )(x, w)


FLASH_UNROLL = 2


def _stack_masked_q(q_ref, qs_ref, tq):
    n_qpairs = q_ref.shape[0]
    per_pair = qs_ref.shape[0] // tq // n_qpairs
    lanes_per_map = LANES // per_pair
    for t in range(n_qpairs):
        q = q_ref[t].astype(F32)
        part = _lane_iota(q.shape) // lanes_per_map
        for a in range(per_pair):
            r0 = (t * per_pair + a) * tq
            qs_ref[r0:r0 + tq] = jnp.where(part == a, q, 0.0).astype(BF16)


def _softmax_accumulate(s, v, m, l, acc_ref):
    m_new = jnp.maximum(m, jnp.max(s, axis=0, keepdims=True))
    alpha = jnp.exp2(m - m_new)
    p = jnp.exp2(s - m_new)
    l_new = alpha * l + jnp.sum(p, axis=0, keepdims=True)
    pv = lax.dot_general(v, p.astype(BF16), TN_DIMS, preferred_element_type=F32)
    acc_ref[...] = alpha * acc_ref[...] + pv
    return m_new, l_new


def _diff_attn_kernel(q_ref, k_ref, v_ref, slope_ref, lam_ref, g_ref, o_ref, qs_ref, acc_ref,
                      *, tq, tk, seq, lambda_init):
    i = pl.program_id(2)
    n_maps = 4
    w = n_maps * tq
    _stack_masked_q(q_ref, qs_ref, tq)
    acc_ref[...] = jnp.zeros(acc_ref.shape, F32)
    slopes = slope_ref[0]
    rel0 = (lax.broadcasted_iota(jnp.int32, (tk, tq), 0)
            - lax.broadcasted_iota(jnp.int32, (tk, tq), 1) - i * tq)

    def chunk(j, carry):
        m, l = carry
        off = pl.multiple_of(j * tk, tk)
        k = k_ref[0, pl.ds(off, tk), :]
        v = v_ref[0, pl.ds(off, tk), :]
        s = lax.dot_general(k, qs_ref[...], NT_DIMS, preferred_element_type=F32)
        dist = jnp.abs(rel0 + j * tk).astype(F32)
        s = jnp.concatenate(
            [s[:, a * tq:(a + 1) * tq] - slopes[a // 2:a // 2 + 1, 0:1] * dist
             for a in range(n_maps)], axis=1)
        return _softmax_accumulate(s, v, m, l, acc_ref)

    carry = (jnp.full((1, w), -jnp.inf, F32), jnp.zeros((1, w), F32))
    m, l = lax.fori_loop(0, seq // tk, chunk, carry, unroll=FLASH_UNROLL)

    o = acc_ref[...] / l
    lam = lam_ref[...]
    lam_full = (jnp.exp(jnp.sum(lam[0:1] * lam[1:2], axis=-1, keepdims=True))
                - jnp.exp(jnp.sum(lam[2:3] * lam[3:4], axis=-1, keepdims=True)) + lambda_init)
    h0 = o[:, 0:tq] - lam_full * o[:, tq:2 * tq]
    h1 = o[:, 2 * tq:3 * tq] - lam_full * o[:, 3 * tq:4 * tq]
    row = lax.broadcasted_iota(jnp.int32, (LANES, tq), 0)
    o = jnp.where(row < HEAD_DIM, h0, h1).T
    o = o * _head_rms_scale(o) * g_ref[...] * (1.0 - lambda_init)
    o_ref[0] = o.astype(BF16)


def _diff_attention(heads, slopes, lam, gain, batch, seq, tq, tk, lambda_init):
    n_tok = heads.shape[1]
    nq = seq // tq
    body = functools.partial(_diff_attn_kernel, tq=tq, tk=tk, seq=seq, lambda_init=lambda_init)
    return pl.pallas_call(
        body,
        out_shape=jax.ShapeDtypeStruct((N_MIX_PAIRS, n_tok, LANES), BF16),
        grid=(batch, N_MIX_PAIRS, nq),
        in_specs=[
            pl.BlockSpec((1, tq, LANES), lambda b, p, i: (p, b * nq + i, 0)),
            pl.BlockSpec((1, seq, LANES), lambda b, p, i: (N_MIX_PAIRS + p, b, 0)),
            pl.BlockSpec((1, seq, LANES), lambda b, p, i: (2 * N_MIX_PAIRS + p, b, 0)),
            pl.BlockSpec((1, 8, LANES), lambda b, p, i: (p, 0, 0)),
            pl.BlockSpec(lam.shape, lambda b, p, i: (0, 0)),
            pl.BlockSpec((1, LANES), lambda b, p, i: (0, 0)),
        ],
        out_specs=pl.BlockSpec((1, tq, LANES), lambda b, p, i: (p, b * nq + i, 0)),
        scratch_shapes=[
            pltpu.VMEM((4 * tq, LANES), BF16),
            pltpu.VMEM((LANES, 4 * tq), F32),
        ],
        compiler_params=_cparams("parallel", "parallel", "parallel"),
        name="diff_attention",
    )(heads, heads, heads, slopes, lam, gain)


def _gqa_attn_kernel(q_ref, k_ref, v_ref, o_ref, qs_ref, acc_ref, *, tq, tk, seq):
    n_qpairs = q_ref.shape[0]
    w = 2 * n_qpairs * tq
    _stack_masked_q(q_ref, qs_ref, tq)
    acc_ref[...] = jnp.zeros(acc_ref.shape, F32)

    def chunk(j, carry):
        m, l = carry
        off = pl.multiple_of(j * tk, tk)
        k = k_ref[0, pl.ds(off, tk), :]
        v = v_ref[0, pl.ds(off, tk), :]
        s = lax.dot_general(k, qs_ref[...], NT_DIMS, preferred_element_type=F32)
        return _softmax_accumulate(s, v, m, l, acc_ref)

    carry = (jnp.full((1, w), -jnp.inf, F32), jnp.zeros((1, w), F32))
    m, l = lax.fori_loop(0, seq // tk, chunk, carry, unroll=FLASH_UNROLL)
    o = acc_ref[...] / l
    row = lax.broadcasted_iota(jnp.int32, (LANES, tq), 0)
    for t in range(n_qpairs):
        lo = o[:, (2 * t) * tq:(2 * t + 1) * tq]
        hi = o[:, (2 * t + 1) * tq:(2 * t + 2) * tq]
        o_ref[t] = jnp.where(row < HEAD_DIM, lo, hi).T.astype(BF16)


def _gqa_attention(heads, batch, seq, tq, tk):
    n_tok = heads.shape[1]
    nq = seq // tq
    n_kv_pairs = N_KV_HEADS // 2
    qp = N_MIX_PAIRS // n_kv_pairs
    body = functools.partial(_gqa_attn_kernel, tq=tq, tk=tk, seq=seq)
    return pl.pallas_call(
        body,
        out_shape=jax.ShapeDtypeStruct((N_MIX_PAIRS, n_tok, LANES), BF16),
        grid=(batch, n_kv_pairs, nq),
        in_specs=[
            pl.BlockSpec((qp, tq, LANES), lambda b, p, i: (p, b * nq + i, 0)),
            pl.BlockSpec((1, seq, LANES), lambda b, p, i: (N_MIX_PAIRS + p, b, 0)),
            pl.BlockSpec((1, seq, LANES), lambda b, p, i: (N_MIX_PAIRS + n_kv_pairs + p, b, 0)),
        ],
        out_specs=pl.BlockSpec((qp, tq, LANES), lambda b, p, i: (p, b * nq + i, 0)),
        scratch_shapes=[
            pltpu.VMEM((2 * qp * tq, LANES), BF16),
            pltpu.VMEM((LANES, 2 * qp * tq), F32),
        ],
        compiler_params=_cparams("parallel", "parallel", "parallel"),
        name="gqa_attention",
    )(heads, heads, heads)


def _na_attn_kernel(q_ref, kp_ref, kc_ref, kn_ref, vp_ref, vc_ref, vn_ref, bias_ref, o_ref,
                    kbuf_ref, vbuf_ref, *, n_grid_rows):
    rb = pl.program_id(2)
    blk = NA_ROWS * GRID_W
    win = NA_ROWS * GRID_W
    kbuf_ref[0:blk] = kp_ref[0]
    kbuf_ref[blk:2 * blk] = kc_ref[0]
    kbuf_ref[2 * blk:3 * blk] = kn_ref[0]
    vbuf_ref[0:blk] = vp_ref[0]
    vbuf_ref[blk:2 * blk] = vc_ref[0]
    vbuf_ref[2 * blk:3 * blk] = vn_ref[0]
    lo = _lane_iota((GRID_W, LANES)) < HEAD_DIM
    for u in range(NA_ROWS):
        r = rb * NA_ROWS + u
        rs = jnp.clip(r - NA_ROWS // 2, 0, n_grid_rows - NA_ROWS)
        off = pl.multiple_of((rs - (rb - 1) * NA_ROWS) * GRID_W, GRID_W)
        ro0 = rs - r + NA_ROWS - 1
        kw = kbuf_ref[pl.ds(off, win)]
        vw = vbuf_ref[pl.ds(off, win)]
        q = q_ref[0, u * GRID_W:(u + 1) * GRID_W].astype(F32)
        qs = jnp.concatenate([jnp.where(lo, q, 0.0), jnp.where(lo, 0.0, q)], axis=0).astype(BF16)
        s = lax.dot_general(qs, kw, NT_DIMS, preferred_element_type=F32)
        bias = jnp.concatenate(
            [jnp.concatenate([bias_ref[h, ro0 + 2 * t] for t in range(NA_ROWS // 2)], axis=1)
             for h in range(2)], axis=0)
        s = s + bias
        m = jnp.max(s, axis=-1, keepdims=True)
        p = jnp.exp(s - m)
        l = jnp.sum(p, axis=-1, keepdims=True)
        o = jnp.dot(p.astype(BF16), vw, preferred_element_type=F32) / l
        o_ref[0, u * GRID_W:(u + 1) * GRID_W] = jnp.where(lo, o[0:GRID_W], o[GRID_W:]).astype(BF16)


def _na_attention(heads, bias, batch, seq):
    n_tok = heads.shape[1]
    n_grid_rows = seq // GRID_W
    nrb = n_grid_rows // NA_ROWS
    blk = NA_ROWS * GRID_W

    def kv_spec(first_pair, shift):
        return pl.BlockSpec(
            (1, blk, LANES),
            lambda b, p, rb: (first_pair + p, b * nrb + jnp.clip(rb + shift, 0, nrb - 1), 0))

    body = functools.partial(_na_attn_kernel, n_grid_rows=n_grid_rows)
    return pl.pallas_call(
        body,
        out_shape=jax.ShapeDtypeStruct((N_MIX_PAIRS, n_tok, LANES), BF16),
        grid=(batch, N_MIX_PAIRS, nrb),
        in_specs=[
            pl.BlockSpec((1, blk, LANES), lambda b, p, rb: (p, b * nrb + rb, 0)),
            kv_spec(N_MIX_PAIRS, -1), kv_spec(N_MIX_PAIRS, 0), kv_spec(N_MIX_PAIRS, 1),
            kv_spec(2 * N_MIX_PAIRS, -1), kv_spec(2 * N_MIX_PAIRS, 0), kv_spec(2 * N_MIX_PAIRS, 1),
            pl.BlockSpec((2,) + bias.shape[1:], lambda b, p, rb: (p, 0, 0, 0)),
        ],
        out_specs=pl.BlockSpec((1, blk, LANES), lambda b, p, rb: (p, b * nrb + rb, 0)),
        scratch_shapes=[pltpu.VMEM((3 * blk, LANES), BF16), pltpu.VMEM((3 * blk, LANES), BF16)],
        compiler_params=_cparams("parallel", "parallel", "arbitrary"),
        name="neighbourhood_attention",
    )(heads, heads, heads, heads, heads, heads, heads, bias)


def _na_bias_table(rpb):
    col = jnp.arange(GRID_W)
    col_start = jnp.clip(col - NA_COLS // 2, 0, GRID_W - NA_COLS)
    col_mask = (col[None, :] >= col_start[:, None]) & (col[None, :] < col_start[:, None] + NA_COLS)
    col_bias_idx = jnp.clip(col[None, :] - col[:, None] + NA_COLS - 1, 0, 2 * NA_COLS - 2)
    tiles = rpb.astype(F32)[:, :, col_bias_idx]
    tiles = jnp.where(col_mask[None, None], tiles, NEG_BIG)
    return jnp.concatenate([tiles[:, :-1], tiles[:, 1:]], axis=-1)


def _outproj_kernel(mix_ref, qm_ref, kv_ref, x_ref, w_ref, g_ref, b_ref, o_ref, *, alpha):
    tm = x_ref.shape[0]
    lo = _lane_iota((tm, LANES)) < HEAD_DIM
    parts = [mix_ref[p] for p in range(mix_ref.shape[0])]
    kv = kv_ref[0]
    for t in range(N_MEM_PAIRS):
        q = qm_ref[t].astype(F32)
        qs = jnp.concatenate([jnp.where(lo, q, 0.0), jnp.where(lo, 0.0, q)], axis=0).astype(BF16)
        kt = kv[:, t * LANES:(t + 1) * LANES]
        vt = kv[:, MEM_WIDTH + t * LANES:MEM_WIDTH + (t + 1) * LANES]
        s = lax.dot_general(qs, kt, NT_DIMS, preferred_element_type=F32)
        m = jnp.max(s, axis=-1, keepdims=True)
        p = jnp.exp(s - m)
        l = jnp.sum(p, axis=-1, keepdims=True)
        o = jnp.dot(p.astype(BF16), vt, preferred_element_type=F32) / l
        parts.append(jnp.where(lo, o[0:tm], o[tm:]).astype(BF16))
    attn = jnp.concatenate(parts, axis=1)
    h = jnp.dot(attn, w_ref[...], preferred_element_type=F32)
    o_ref[...] = _layer_norm(alpha * x_ref[...] + h, g_ref[...], b_ref[...])


def _out_projection(mix, heads, kv, x, w_out, g, b, batch, seq, tm, alpha):
    n_tok, d = x.shape
    nt = seq // tm
    n_pairs = heads.shape[0]
    n_mem = kv.shape[1]
    body = functools.partial(_outproj_kernel, alpha=alpha)
    return pl.pallas_call(
        body,
        out_shape=jax.ShapeDtypeStruct((n_tok, d), F32),
        grid=(batch, nt),
        in_specs=[
            pl.BlockSpec((N_MIX_PAIRS, tm, LANES), lambda bi, i: (0, bi * nt + i, 0)),
            pl.BlockSpec((N_MEM_PAIRS, tm, LANES),
                         lambda bi, i: (n_pairs // N_MEM_PAIRS - 1, bi * nt + i, 0)),
            pl.BlockSpec((1, n_mem, 2 * MEM_WIDTH), lambda bi, i: (bi, 0, 0)),
            pl.BlockSpec((tm, d), lambda bi, i: (bi * nt + i, 0)),
            pl.BlockSpec(w_out.shape, lambda bi, i: (0, 0)),
            pl.BlockSpec((1, d), lambda bi, i: (0, 0)),
            pl.BlockSpec((1, d), lambda bi, i: (0, 0)),
        ],
        out_specs=pl.BlockSpec((tm, d), lambda bi, i: (bi * nt + i, 0)),
        compiler_params=_cparams("parallel", "parallel"),
        name="mem_attention_out_projection",
    )(mix, heads, kv, x, w_out, g, b)


ROW_TILE = 512
SC_WINDOW = 128
HALF = D_MODEL // 2


def _pack_half(v):
    q = HALF // 2
    return pltpu.pack_elementwise([v[:, :q], v[:, q:]], packed_dtype=BF16)


def _unpack_half(w):
    return jnp.concatenate(
        [pltpu.unpack_elementwise(w, index=h, packed_dtype=BF16, unpacked_dtype=F32)
         for h in range(2)], axis=1)

def _first_index_of_max(vals, index, sentinel, axes):
    mx = vals
    for ax in axes:
        mx = jnp.max(mx, axis=ax, keepdims=True)
    idx = jnp.where(vals == mx, index, sentinel)
    for ax in axes:
        idx = jnp.min(idx, axis=ax, keepdims=True)
    return mx, idx


def _route_kernel(x_ref, rth_ref, rtl_ref, rbias_ref, tri_ref,
                  xp_ref, e8_ref, r8_ref, w8_ref, cnt_ref, run_ref):
    @pl.when(pl.program_id(0) == 0)
    def _():
        run_ref[...] = jnp.zeros(run_ref.shape, F32)

    x = x_ref[...]
    tm = x.shape[0]
    xp_ref[0] = _pack_half(x[:, :HALF])
    xp_ref[1] = _pack_half(x[:, HALF:])
    xh = x.astype(BF16)
    xl = (x - xh.astype(F32)).astype(BF16)
    rth = rth_ref[...]
    logits = (lax.dot_general(rth, xh, NT_DIMS, preferred_element_type=F32)
              + lax.dot_general(rth, xl, NT_DIMS, preferred_element_type=F32)
              + lax.dot_general(rtl_ref[...], xh, NT_DIMS, preferred_element_type=F32))
    scores = jax.nn.sigmoid(logits)
    choice = scores + rbias_ref[...]
    shape3 = (N_GROUPS, PER_GROUP, tm)
    c3 = choice.reshape(shape3)
    s3 = scores.reshape(shape3)
    e_in_g = lax.broadcasted_iota(jnp.int32, shape3, 1).astype(F32)
    g_idx = lax.broadcasted_iota(jnp.int32, (N_GROUPS, 1, tm), 0).astype(F32)
    e_idx = lax.broadcasted_iota(jnp.int32, shape3, 0).astype(F32) * PER_GROUP + e_in_g
    m1, i1 = _first_index_of_max(c3, e_in_g, float(PER_GROUP), (1,))
    m2 = jnp.max(jnp.where(e_in_g == i1, -jnp.inf, c3), axis=1, keepdims=True)
    cur = m1 + m2
    gsel = jnp.zeros(cur.shape, F32)
    for _ in range(TOPK_GROUPS):
        _, gi = _first_index_of_max(cur, g_idx, float(N_GROUPS), (0,))
        hit = g_idx == gi
        gsel = jnp.where(hit, 1.0, gsel)
        cur = jnp.where(hit, -jnp.inf, cur)
    cur = jnp.where(gsel > 0.0, c3, -jnp.inf)
    sel = jnp.zeros(shape3, F32)
    picks = []
    for _ in range(TOP_K):
        _, ei = _first_index_of_max(cur, e_idx, float(N_EXPERTS), (1, 0))
        hit = e_idx == ei
        sel = jnp.where(hit, 1.0, sel)
        cur = jnp.where(hit, -jnp.inf, cur)
        picks.append(ei)
    w = sel * s3
    denom = jnp.sum(jnp.sum(w, axis=1, keepdims=True), axis=0, keepdims=True)
    gates3 = w / denom * ROUTED_SCALE
    sel2 = sel.reshape(N_EXPERTS, tm)
    before = jnp.dot(sel2.astype(BF16), tri_ref[...], preferred_element_type=F32)
    rank3 = (before + run_ref[:, 0:1]).reshape(shape3)
    run_ref[...] = run_ref[...] + jnp.sum(sel2, axis=1, keepdims=True)
    cnt_ref[...] = run_ref[...]

    def pick(values, hit):
        return jnp.sum(jnp.sum(jnp.where(hit, values, 0.0), axis=1, keepdims=True), axis=0)

    for k, ei in enumerate(picks):
        hit = e_idx == ei
        e8_ref[k:k + 1, :] = ei.reshape(1, tm).astype(jnp.int32)
        r8_ref[k:k + 1, :] = pick(rank3, hit).astype(jnp.int32)
        w8_ref[k:k + 1, :] = pick(gates3, hit)


def _route(x, rth, rtl, rbias, tm):
    n_tok, d = x.shape
    tri = (jnp.arange(tm)[:, None] < jnp.arange(tm)[None, :]).astype(BF16)
    const2 = lambda i: (0, 0)
    return pl.pallas_call(
        _route_kernel,
        out_shape=(
            jax.ShapeDtypeStruct((2, n_tok, HALF // 2), jnp.int32),
            jax.ShapeDtypeStruct((TOP_K, n_tok), jnp.int32),
            jax.ShapeDtypeStruct((TOP_K, n_tok), jnp.int32),
            jax.ShapeDtypeStruct((TOP_K, n_tok), F32),
            jax.ShapeDtypeStruct((N_EXPERTS, LANES), F32),
        ),
        grid=(n_tok // tm,),
        in_specs=[
            pl.BlockSpec((tm, d), lambda i: (i, 0)),
            pl.BlockSpec(rth.shape, const2),
            pl.BlockSpec(rtl.shape, const2),
            pl.BlockSpec(rbias.shape, const2),
            pl.BlockSpec((tm, tm), const2),
        ],
        out_specs=(
            pl.BlockSpec((2, tm, HALF // 2), lambda i: (0, i, 0)),
            pl.BlockSpec((TOP_K, tm), lambda i: (0, i)),
            pl.BlockSpec((TOP_K, tm), lambda i: (0, i)),
            pl.BlockSpec((TOP_K, tm), lambda i: (0, i)),
            pl.BlockSpec((N_EXPERTS, LANES), const2),
        ),
        scratch_shapes=[pltpu.VMEM((N_EXPERTS, LANES), F32)],
        compiler_params=_cparams("arbitrary"),
        name="moe_router",
    )(x, rth, rtl, rbias, tri)


def _positions_kernel(start_ref, e8_ref, r8_ref, idx_ref, *, n_rows):
    e8 = e8_ref[...]
    pos = r8_ref[...]
    for e in range(N_EXPERTS):
        pos = pos + jnp.where(e8 == e, start_ref[e], 0)
    idx_ref[0] = pos
    idx_ref[1] = pos + n_rows


def _positions(start, e8, r8, n_rows, tn):
    n_tok = e8.shape[1]
    grid_spec = pltpu.PrefetchScalarGridSpec(
        num_scalar_prefetch=1,
        grid=(n_tok // tn,),
        in_specs=[pl.BlockSpec((TOP_K, tn), lambda i, st: (0, i)),
                  pl.BlockSpec((TOP_K, tn), lambda i, st: (0, i))],
        out_specs=pl.BlockSpec((2, TOP_K, tn), lambda i, st: (0, 0, i)),
    )
    return pl.pallas_call(
        functools.partial(_positions_kernel, n_rows=n_rows),
        out_shape=jax.ShapeDtypeStruct((2, TOP_K, n_tok), jnp.int32),
        grid_spec=grid_spec,
        compiler_params=_cparams("parallel"),
        name="moe_positions",
    )(start, e8, r8)


def _experts_kernel(te_ref, used_ref, xs_ref, wg_ref, wu_ref, wd_ref, ys_ref):
    @pl.when(pl.program_id(0) < used_ref[0])
    def _():
        x = jnp.concatenate([_unpack_half(xs_ref[0]), _unpack_half(xs_ref[1])], axis=1).astype(BF16)
        h = (jax.nn.silu(jnp.dot(x, wg_ref[0], preferred_element_type=F32))
             * jnp.dot(x, wu_ref[0], preferred_element_type=F32))
        y = jnp.dot(h.astype(BF16), wd_ref[0], preferred_element_type=F32)
        ys_ref[0] = _pack_half(y[:, :HALF])
        ys_ref[1] = _pack_half(y[:, HALF:])


def _experts(tile_expert, n_used, xs, eg, eu, ed):
    n_rows = xs.shape[1]
    _, d, f = eg.shape
    grid_spec = pltpu.PrefetchScalarGridSpec(
        num_scalar_prefetch=2,
        grid=(n_rows // ROW_TILE,),
        in_specs=[
            pl.BlockSpec((2, ROW_TILE, HALF // 2), lambda i, te, nu: (0, i, 0)),
            pl.BlockSpec((1, d, f), lambda i, te, nu: (te[i], 0, 0)),
            pl.BlockSpec((1, d, f), lambda i, te, nu: (te[i], 0, 0)),
            pl.BlockSpec((1, f, d), lambda i, te, nu: (te[i], 0, 0)),
        ],
        out_specs=pl.BlockSpec((2, ROW_TILE, HALF // 2), lambda i, te, nu: (0, i, 0)),
    )
    return pl.pallas_call(
        _experts_kernel,
        out_shape=jax.ShapeDtypeStruct(xs.shape, jnp.int32),
        grid_spec=grid_spec,
        compiler_params=_cparams("parallel"),
        name="moe_experts",
    )(tile_expert, n_used, xs, eg, eu, ed)


def _combine_kernel(x_ref, yg_ref, w8_ref, sg_ref, su_ref, sd_ref, g_ref, b_ref, o_ref, *, alpha):
    x = x_ref[...]
    tm = x.shape[0]
    xh = x.astype(BF16)
    hs = (jax.nn.silu(jnp.dot(xh, sg_ref[...], preferred_element_type=F32))
          * jnp.dot(xh, su_ref[...], preferred_element_type=F32))
    acc = jnp.dot(hs.astype(BF16), sd_ref[...], preferred_element_type=F32)
    w8 = jnp.concatenate([w8_ref[...], jnp.zeros((LANES - TOP_K, tm), F32)], axis=0).T
    for k in range(TOP_K):
        yk = jnp.concatenate([_unpack_half(yg_ref[0, k]), _unpack_half(yg_ref[1, k])], axis=1)
        acc = acc + w8[:, k:k + 1] * yk
    o_ref[...] = _layer_norm(alpha * x + acc, g_ref[...], b_ref[...])


def _combine(x, yg, w8, sg, su, sd, g, b, tm, alpha):
    n_tok, d = x.shape
    const2 = lambda i: (0, 0)
    return pl.pallas_call(
        functools.partial(_combine_kernel, alpha=alpha),
        out_shape=jax.ShapeDtypeStruct((n_tok, d), F32),
        grid=(n_tok // tm,),
        in_specs=[
            pl.BlockSpec((tm, d), lambda i: (i, 0)),
            pl.BlockSpec((2, TOP_K, tm, HALF // 2), lambda i: (0, 0, i, 0)),
            pl.BlockSpec((TOP_K, tm), lambda i: (0, i)),
            pl.BlockSpec(sg.shape, const2),
            pl.BlockSpec(su.shape, const2),
            pl.BlockSpec(sd.shape, const2),
            pl.BlockSpec((1, d), const2),
            pl.BlockSpec((1, d), const2),
        ],
        out_specs=pl.BlockSpec((tm, d), lambda i: (i, 0)),
        compiler_params=_cparams("parallel"),
        name="moe_combine",
    )(x, yg, w8, sg, su, sd, g, b)


def _sc_mesh():
    return plsc.VectorSubcoreMesh(core_axis_name="c", subcore_axis_name="s")


def _sc_gather_rows(table, idx):
    n = idx.shape[0]
    d = table.shape[1]

    @functools.partial(pl.kernel, out_type=jax.ShapeDtypeStruct((n, d), table.dtype),
                       mesh=_sc_mesh())
    def gather(table_hbm, idx_hbm, out_hbm):
        def body(idx_vmem, out_vmem):
            pltpu.sync_copy(table_hbm.at[idx_vmem.at[0]], out_vmem)

        pltpu.emit_pipeline(
            body, grid=(n // SC_WINDOW,),
            in_specs=[pl.BlockSpec((1, SC_WINDOW), index_map=lambda i: (0, i))],
            out_specs=[pl.BlockSpec((SC_WINDOW, d), index_map=lambda i: (i, 0))],
            core_axis_name=("c", "s"), dimension_semantics=(pltpu.PARALLEL,), trace_scopes=False,
        )(idx_hbm, out_hbm)

    return gather(table, idx.reshape(1, n))


def _sc_scatter_rows(rows, idx, n_out, src_block):
    n = idx.shape[0]
    d = rows.shape[1]

    @functools.partial(pl.kernel, out_type=jax.ShapeDtypeStruct((n_out, d), rows.dtype),
                       mesh=_sc_mesh())
    def scatter(rows_hbm, idx_hbm, out_hbm):
        def body(rows_vmem, idx_vmem):
            pltpu.sync_copy(rows_vmem, out_hbm.at[idx_vmem.at[0]])

        pltpu.emit_pipeline(
            body, grid=(n // SC_WINDOW,),
            in_specs=[pl.BlockSpec((SC_WINDOW, d), index_map=lambda i: (src_block(i), 0)),
                      pl.BlockSpec((1, SC_WINDOW), index_map=lambda i: (0, i))],
            out_specs=[],
            core_axis_name=("c", "s"), dimension_semantics=(pltpu.PARALLEL,), trace_scopes=False,
        )(rows_hbm, idx_hbm)

    return scatter(rows, idx.reshape(1, n))


def _moe_dispatch(x, rth, rtl, rbias, tm):
    n_tok, d = x.shape
    n_rows = TOP_K * n_tok + N_EXPERTS * ROW_TILE
    n_tiles = n_rows // ROW_TILE
    xp, e8, r8, w8, cnt = _route(x, rth, rtl, rbias, tm)
    counts = cnt[:, 0].astype(jnp.int32)
    padded = (counts + ROW_TILE - 1) // ROW_TILE * ROW_TILE
    ends = jnp.cumsum(padded)
    start = ends - padded
    n_used = (ends[-1:] // ROW_TILE).astype(jnp.int32)
    tile_ids = jnp.arange(n_tiles, dtype=jnp.int32)
    tile_expert = jnp.minimum(
        jnp.sum((tile_ids[:, None] >= (ends // ROW_TILE)[None, :]).astype(jnp.int32), axis=1),
        N_EXPERTS - 1)
    idx = _positions(start, e8, r8, n_rows, _pick_tile(n_tok, 2048)).reshape(-1)
    blocks_per_half = n_tok // SC_WINDOW
    xs = _sc_scatter_rows(
        xp.reshape(2 * n_tok, HALF // 2), idx, 2 * n_rows,
        lambda i: (i // (TOP_K * blocks_per_half)) * blocks_per_half + i % blocks_per_half)
    return xs.reshape(2, n_rows, HALF // 2), idx, tile_expert, n_used, w8


def _moe_experts(xs, idx, tile_expert, n_used, eg, eu, ed):
    n_rows = xs.shape[1]
    ys = _experts(tile_expert, n_used, xs, eg, eu, ed)
    yg = _sc_gather_rows(ys.reshape(2 * n_rows, HALF // 2), idx)
    return yg.reshape(2, TOP_K, idx.shape[0] // (2 * TOP_K), HALF // 2)


def _rope_tables(seq):
    t = jnp.arange(seq)
    row = (t // GRID_W).astype(F32)
    colp = (t % GRID_W).astype(F32)
    inv = ROPE_THETA ** (-jnp.arange(0, ROPE_AXIS_DIM, 2, dtype=F32) / ROPE_AXIS_DIM)
    ang = jnp.concatenate([row[:, None] * inv[None], colp[:, None] * inv[None]], axis=-1)
    ang = jnp.repeat(ang, 2, axis=-1)
    sign = jnp.where(jnp.arange(HEAD_DIM) % 2 == 0, -1.0, 1.0).astype(F32)
    cos = jnp.tile(jnp.cos(ang), (1, 2))
    sin = jnp.tile(jnp.sin(ang) * sign[None], (1, 2))
    return cos, sin


def _gqa_head_order():
    order = []
    for kvp in range(N_KV_HEADS // 2):
        base = 2 * GQA_GROUP * kvp
        for g in range(GQA_GROUP):
            order += [base + g, base + GQA_GROUP + g]
    return np.array(order)


def _pick_tile(n, target):
    t = min(n, target)
    while n % t:
        t //= 2
    return t


def kernel(x_prompt, x_sample, mem_prompt, mem_sample, a_w_in, a_lambda, a_subln, b_w_in, b_rpb,
           c_w_in, c_q_norm, c_k_norm, w_mem_kv, w_out, ln1_g, ln1_b, router, router_bias,
           e_gate, e_up, e_down, s_gate, s_up, s_down, ln2_g, ln2_b):
    depth = w_out.shape[0]
    alpha = (2.0 * depth) ** 0.25
    seq = x_prompt.shape[1]
    d = x_prompt.shape[2]
    assert x_sample.shape[1] == seq and seq % (NA_ROWS * GRID_W) == 0
    n_mem = mem_prompt.shape[1]
    batches = [x_prompt.shape[0], x_sample.shape[0]]
    xs_ = [x_prompt.reshape(-1, d), x_sample.reshape(-1, d)]
    mems = [mem_prompt.reshape(-1, d), mem_sample.reshape(-1, d)]

    tm = _pick_tile(seq, 512)
    tk = _pick_tile(seq, 512)
    head_order = _gqa_head_order()
    col_order = (head_order[:, None] * HEAD_DIM + np.arange(HEAD_DIM)[None]).reshape(-1)
    cos, sin = _rope_tables(seq)
    slopes = _alibi_slopes(N_MIX_HEADS).reshape(N_MIX_PAIRS, 2)
    slope_tab = np.zeros((N_MIX_PAIRS, 8, LANES), np.float32)
    slope_tab[:, 0:2, :] = slopes[:, :, None]
    slope_tab = jnp.asarray(slope_tab) * LOG2E
    ones_row = functools.partial(jnp.ones, dtype=F32)
    mem_q_scale = jnp.full((MEM_WIDTH,), HEAD_DIM ** -0.5, F32)

    for i in range(depth):
        mixer, occ = i % N_MIXERS, i // N_MIXERS
        w_o = w_out[i]
        rope = None
        if mixer == 0:
            w_in = a_w_in[occ]
            q_scale = DIFF_QK_DIM ** -0.5 * LOG2E
            gain = jnp.tile(a_subln[occ].astype(F32), 2)[None]
            lam = a_lambda[occ].astype(F32)
        elif mixer == 1:
            w_in = b_w_in[occ]
            q_scale = HEAD_DIM ** -0.5
            na_bias = _na_bias_table(b_rpb[occ])
        else:
            w_in = c_w_in[occ]
            w_in = jnp.concatenate([w_in[:, :MIX_WIDTH][:, col_order], w_in[:, MIX_WIDTH:]], axis=1)
            w_o = jnp.concatenate([w_o[:MIX_WIDTH][col_order], w_o[MIX_WIDTH:]], axis=0)
            q_scale = HEAD_DIM ** -0.5 * LOG2E
            qg = jnp.tile(c_q_norm[occ].astype(F32), 2)[None, None]
            kg = jnp.tile(c_k_norm[occ].astype(F32), 2)[None, None]
            gains = jnp.concatenate([jnp.tile(qg, (N_MIX_PAIRS, 1, 1)),
                                     jnp.tile(kg, (N_KV_HEADS // 2, 1, 1))], axis=0)
            rope = (gains, cos, sin)
        colscale = jnp.concatenate([
            jnp.full((MIX_WIDTH,), q_scale, F32),
            ones_row((w_in.shape[1] - MIX_WIDTH - MEM_WIDTH,)), mem_q_scale])[None]
        w_in = w_in.astype(BF16)
        w_o = w_o.astype(BF16)
        w_kv = w_mem_kv[i].astype(BF16)
        rt = router[i].T.astype(F32)
        rth = rt.astype(BF16)
        rtl = (rt - rth.astype(F32)).astype(BF16)
        rbias = router_bias[i].astype(F32)[:, None]
        eg, eu, ed = e_gate[i].astype(BF16), e_up[i].astype(BF16), e_down[i].astype(BF16)
        sg, su, sd = s_gate[i].astype(BF16), s_up[i].astype(BF16), s_down[i].astype(BF16)

        dispatched = []
        for s in range(len(xs_)):
            x, batch = xs_[s], batches[s]
            heads = _in_projection(x, w_in, colscale, tm, seq, rope=rope)
            if mixer == 0:
                mix = _diff_attention(heads, slope_tab, lam, gain, batch, seq,
                                      _pick_tile(seq, 256), _pick_tile(seq, 1024),
                                      _diff_lambda_init(i))
            elif mixer == 1:
                mix = _na_attention(heads, na_bias, batch, seq)
            else:
                mix = _gqa_attention(heads, batch, seq, _pick_tile(seq, 512), tk)
            kv = _matmul(mems[s], w_kv, _pick_tile(batch * n_mem, 512))
            kv = kv.reshape(batch, n_mem, 2 * MEM_WIDTH)
            x = _out_projection(mix, heads, kv, x, w_o, ln1_g[i][None], ln1_b[i][None],
                                batch, seq, tm, alpha)
            xs_[s] = x
            dispatched.append(_moe_dispatch(x, rth, rtl, rbias, tm))
        gathered = [_moe_experts(*disp[:4], eg, eu, ed) for disp in dispatched]
        for s in range(len(xs_)):
            xs_[s] = _combine(xs_[s], gathered[s], dispatched[s][4], sg, su, sd,
                              ln2_g[i][None], ln2_b[i][None], tm, alpha)

    return tuple(x.reshape(b, seq, d) for x, b in zip(xs_, batches))
```

```python
import functools
import math

import numpy as np
import jax
import jax.numpy as jnp
from jax import lax
from jax.experimental import pallas as pl
from jax.experimental.pallas import tpu as pltpu
from jax.experimental.pallas import tpu_sc as plsc

F32 = jnp.float32
BF16 = jnp.bfloat16

D_MODEL = 1024
HEAD_DIM = 64
LANES = 128
N_MIX_HEADS = 12
N_MIX_PAIRS = N_MIX_HEADS // 2
N_MEM_HEADS = 4
N_MEM_PAIRS = N_MEM_HEADS // 2
MIX_WIDTH = N_MIX_HEADS * HEAD_DIM
MEM_WIDTH = N_MEM_HEADS * HEAD_DIM
N_MIXERS = 3
DIFF_QK_DIM = HEAD_DIM // 2
GRID_W = 64
NA_ROWS = 8
NA_COLS = 16
N_KV_HEADS = 4
GQA_GROUP = N_MIX_HEADS // N_KV_HEADS
ROPE_THETA = 10000.0
ROPE_AXIS_DIM = HEAD_DIM // 2
N_EXPERTS = 64
TOP_K = 8
N_GROUPS = 8
TOPK_GROUPS = 4
PER_GROUP = N_EXPERTS // N_GROUPS
D_EXPERT = 256
ROUTED_SCALE = 2.5
LN_EPS = 1e-5
RMS_EPS = 1e-6
NEG_BIG = -1e30
LOG2E = math.log2(math.e)

VMEM_LIMIT = 48 * 1024 * 1024

NT_DIMS = (((1,), (1,)), ((), ()))
TN_DIMS = (((0,), (0,)), ((), ()))


def _cparams(*sem):
    return pltpu.CompilerParams(dimension_semantics=sem, vmem_limit_bytes=VMEM_LIMIT)


def _alibi_slopes(n):
    def pow2_slopes(m):
        start = 2.0 ** (-8.0 / m)
        return [start ** (i + 1) for i in range(m)]
    if math.log2(n).is_integer():
        s = pow2_slopes(n)
    else:
        c = 2 ** math.floor(math.log2(n))
        s = pow2_slopes(c) + pow2_slopes(2 * c)[0::2][: n - c]
    return np.array(s, np.float32)


def _diff_lambda_init(layer_idx):
    return 0.8 - 0.6 * math.exp(-0.3 * layer_idx)


def _lane_iota(shape):
    return lax.broadcasted_iota(jnp.int32, shape, len(shape) - 1)


def _layer_norm(y, g, b):
    mu = jnp.mean(y, axis=-1, keepdims=True)
    yc = y - mu
    var = jnp.mean(yc * yc, axis=-1, keepdims=True)
    return yc * lax.rsqrt(var + LN_EPS) * g + b


def _head_rms_scale(x):
    lo = _lane_iota(x.shape) < HEAD_DIM
    xx = x * x
    ss_lo = jnp.sum(jnp.where(lo, xx, 0.0), axis=-1, keepdims=True)
    ss_hi = jnp.sum(jnp.where(lo, 0.0, xx), axis=-1, keepdims=True)
    inv = 1.0 / HEAD_DIM
    return jnp.where(lo, lax.rsqrt(ss_lo * inv + RMS_EPS), lax.rsqrt(ss_hi * inv + RMS_EPS))


def _inproj_kernel(x_ref, w_ref, cs_ref, o_ref):
    r = jnp.dot(x_ref[...].astype(BF16), w_ref[...], preferred_element_type=F32)
    r = r * cs_ref[...]
    for p in range(o_ref.shape[0]):
        o_ref[p] = r[:, p * LANES:(p + 1) * LANES].astype(BF16)


def _inproj_rope_kernel(x_ref, w_ref, cs_ref, g_ref, cos_ref, sin_ref, o_ref, *, n_rope_pairs):
    r = jnp.dot(x_ref[...].astype(BF16), w_ref[...], preferred_element_type=F32)
    cs = cs_ref[...]
    cos = cos_ref[...]
    sin = sin_ref[...]
    even = (_lane_iota(cos.shape) % 2) == 0
    for p in range(o_ref.shape[0]):
        xp = r[:, p * LANES:(p + 1) * LANES]
        if p < n_rope_pairs:
            xn = xp * _head_rms_scale(xp) * g_ref[p]
            partner = jnp.where(even, pltpu.roll(xn, LANES - 1, 1), pltpu.roll(xn, 1, 1))
            xp = xn * cos + partner * sin
        o_ref[p] = (xp * cs[:, p * LANES:(p + 1) * LANES]).astype(BF16)


def _in_projection(x, w, colscale, tm, seq, rope=None):
    n_tok, d = x.shape
    width = w.shape[1]
    n_pairs = width // LANES
    in_specs = [
        pl.BlockSpec((tm, d), lambda i: (i, 0)),
        pl.BlockSpec((d, width), lambda i: (0, 0)),
        pl.BlockSpec((1, width), lambda i: (0, 0)),
    ]
    args = [x, w, colscale]
    if rope is None:
        body = _inproj_kernel
    else:
        gains, cos, sin = rope
        tiles_per_seq = seq // tm
        body = functools.partial(_inproj_rope_kernel, n_rope_pairs=gains.shape[0])
        in_specs += [
            pl.BlockSpec(gains.shape, lambda i: (0, 0, 0)),
            pl.BlockSpec((tm, LANES), lambda i: (i % tiles_per_seq, 0)),
            pl.BlockSpec((tm, LANES), lambda i: (i % tiles_per_seq, 0)),
        ]
        args += [gains, cos, sin]
    return pl.pallas_call(
        body,
        out_shape=jax.ShapeDtypeStruct((n_pairs, n_tok, LANES), BF16),
        grid=(n_tok // tm,),
        in_specs=in_specs,
        out_specs=pl.BlockSpec((n_pairs, tm, LANES), lambda i: (0, i, 0)),
        compiler_params=_cparams("parallel"),
        name="in_projection",
    )(*args)


def _matmul_kernel(x_ref, w_ref, o_ref):
    o_ref[...] = jnp.dot(x_ref[...].astype(BF16), w_ref[...],
                         preferred_element_type=F32).astype(o_ref.dtype)


def _matmul(x, w, tm):
    m, k = x.shape
    n = w.shape[1]
    return pl.pallas_call(
        _matmul_kernel,
        out_shape=jax.ShapeDtypeStruct((m, n), BF16),
        grid=(m // tm,),
        in_specs=[pl.BlockSpec((tm, k), lambda i: (i, 0)), pl.BlockSpec((k, n), lambda i: (0, 0))],
        out_specs=pl.BlockSpec((tm, n), lambda i: (i, 0)),
        compiler_params=_cparams("parallel"),
        name="mem_kv_projection",
    )(x, w)


FLASH_UNROLL = 2


def _stack_masked_q(q_ref, qs_ref, tq):
    n_qpairs = q_ref.shape[0]
    per_pair = qs_ref.shape[0] // tq // n_qpairs
    lanes_per_map = LANES // per_pair
    for t in range(n_qpairs):
        q = q_ref[t].astype(F32)
        part = _lane_iota(q.shape) // lanes_per_map
        for a in range(per_pair):
            r0 = (t * per_pair + a) * tq
            qs_ref[r0:r0 + tq] = jnp.where(part == a, q, 0.0).astype(BF16)


def _softmax_accumulate(s, v, m, l, acc_ref):
    m_new = jnp.maximum(m, jnp.max(s, axis=0, keepdims=True))
    alpha = jnp.exp2(m - m_new)
    p = jnp.exp2(s - m_new)
    l_new = alpha * l + jnp.sum(p, axis=0, keepdims=True)
    pv = lax.dot_general(v, p.astype(BF16), TN_DIMS, preferred_element_type=F32)
    acc_ref[...] = alpha * acc_ref[...] + pv
    return m_new, l_new


def _diff_attn_kernel(q_ref, k_ref, v_ref, slope_ref, lam_ref, g_ref, o_ref, qs_ref, acc_ref,
                      *, tq, tk, seq, lambda_init):
    i = pl.program_id(2)
    n_maps = 4
    w = n_maps * tq
    _stack_masked_q(q_ref, qs_ref, tq)
    acc_ref[...] = jnp.zeros(acc_ref.shape, F32)
    slopes = slope_ref[0]
    rel0 = (lax.broadcasted_iota(jnp.int32, (tk, tq), 0)
            - lax.broadcasted_iota(jnp.int32, (tk, tq), 1) - i * tq)

    def chunk(j, carry):
        m, l = carry
        off = pl.multiple_of(j * tk, tk)
        k = k_ref[0, pl.ds(off, tk), :]
        v = v_ref[0, pl.ds(off, tk), :]
        s = lax.dot_general(k, qs_ref[...], NT_DIMS, preferred_element_type=F32)
        dist = jnp.abs(rel0 + j * tk).astype(F32)
        s = jnp.concatenate(
            [s[:, a * tq:(a + 1) * tq] - slopes[a // 2:a // 2 + 1, 0:1] * dist
             for a in range(n_maps)], axis=1)
        return _softmax_accumulate(s, v, m, l, acc_ref)

    carry = (jnp.full((1, w), -jnp.inf, F32), jnp.zeros((1, w), F32))
    m, l = lax.fori_loop(0, seq // tk, chunk, carry, unroll=FLASH_UNROLL)

    o = acc_ref[...] / l
    lam = lam_ref[...]
    lam_full = (jnp.exp(jnp.sum(lam[0:1] * lam[1:2], axis=-1, keepdims=True))
                - jnp.exp(jnp.sum(lam[2:3] * lam[3:4], axis=-1, keepdims=True)) + lambda_init)
    h0 = o[:, 0:tq] - lam_full * o[:, tq:2 * tq]
    h1 = o[:, 2 * tq:3 * tq] - lam_full * o[:, 3 * tq:4 * tq]
    row = lax.broadcasted_iota(jnp.int32, (LANES, tq), 0)
    o = jnp.where(row < HEAD_DIM, h0, h1).T
    o = o * _head_rms_scale(o) * g_ref[...] * (1.0 - lambda_init)
    o_ref[0] = o.astype(BF16)


def _diff_attention(heads, slopes, lam, gain, batch, seq, tq, tk, lambda_init):
    n_tok = heads.shape[1]
    nq = seq // tq
    body = functools.partial(_diff_attn_kernel, tq=tq, tk=tk, seq=seq, lambda_init=lambda_init)
    return pl.pallas_call(
        body,
        out_shape=jax.ShapeDtypeStruct((N_MIX_PAIRS, n_tok, LANES), BF16),
        grid=(batch, N_MIX_PAIRS, nq),
        in_specs=[
            pl.BlockSpec((1, tq, LANES), lambda b, p, i: (p, b * nq + i, 0)),
            pl.BlockSpec((1, seq, LANES), lambda b, p, i: (N_MIX_PAIRS + p, b, 0)),
            pl.BlockSpec((1, seq, LANES), lambda b, p, i: (2 * N_MIX_PAIRS + p, b, 0)),
            pl.BlockSpec((1, 8, LANES), lambda b, p, i: (p, 0, 0)),
            pl.BlockSpec(lam.shape, lambda b, p, i: (0, 0)),
            pl.BlockSpec((1, LANES), lambda b, p, i: (0, 0)),
        ],
        out_specs=pl.BlockSpec((1, tq, LANES), lambda b, p, i: (p, b * nq + i, 0)),
        scratch_shapes=[
            pltpu.VMEM((4 * tq, LANES), BF16),
            pltpu.VMEM((LANES, 4 * tq), F32),
        ],
        compiler_params=_cparams("parallel", "parallel", "parallel"),
        name="diff_attention",
    )(heads, heads, heads, slopes, lam, gain)


def _gqa_attn_kernel(q_ref, k_ref, v_ref, o_ref, qs_ref, acc_ref, *, tq, tk, seq):
    n_qpairs = q_ref.shape[0]
    w = 2 * n_qpairs * tq
    _stack_masked_q(q_ref, qs_ref, tq)
    acc_ref[...] = jnp.zeros(acc_ref.shape, F32)

    def chunk(j, carry):
        m, l = carry
        off = pl.multiple_of(j * tk, tk)
        k = k_ref[0, pl.ds(off, tk), :]
        v = v_ref[0, pl.ds(off, tk), :]
        s = lax.dot_general(k, qs_ref[...], NT_DIMS, preferred_element_type=F32)
        return _softmax_accumulate(s, v, m, l, acc_ref)

    carry = (jnp.full((1, w), -jnp.inf, F32), jnp.zeros((1, w), F32))
    m, l = lax.fori_loop(0, seq // tk, chunk, carry, unroll=FLASH_UNROLL)
    o = acc_ref[...] / l
    row = lax.broadcasted_iota(jnp.int32, (LANES, tq), 0)
    for t in range(n_qpairs):
        lo = o[:, (2 * t) * tq:(2 * t + 1) * tq]
        hi = o[:, (2 * t + 1) * tq:(2 * t + 2) * tq]
        o_ref[t] = jnp.where(row < HEAD_DIM, lo, hi).T.astype(BF16)


def _gqa_attention(heads, batch, seq, tq, tk):
    n_tok = heads.shape[1]
    nq = seq // tq
    n_kv_pairs = N_KV_HEADS // 2
    qp = N_MIX_PAIRS // n_kv_pairs
    body = functools.partial(_gqa_attn_kernel, tq=tq, tk=tk, seq=seq)
    return pl.pallas_call(
        body,
        out_shape=jax.ShapeDtypeStruct((N_MIX_PAIRS, n_tok, LANES), BF16),
        grid=(batch, n_kv_pairs, nq),
        in_specs=[
            pl.BlockSpec((qp, tq, LANES), lambda b, p, i: (p, b * nq + i, 0)),
            pl.BlockSpec((1, seq, LANES), lambda b, p, i: (N_MIX_PAIRS + p, b, 0)),
            pl.BlockSpec((1, seq, LANES), lambda b, p, i: (N_MIX_PAIRS + n_kv_pairs + p, b, 0)),
        ],
        out_specs=pl.BlockSpec((qp, tq, LANES), lambda b, p, i: (p, b * nq + i, 0)),
        scratch_shapes=[
            pltpu.VMEM((2 * qp * tq, LANES), BF16),
            pltpu.VMEM((LANES, 2 * qp * tq), F32),
        ],
        compiler_params=_cparams("parallel", "parallel", "parallel"),
        name="gqa_attention",
    )(heads, heads, heads)


def _na_attn_kernel(q_ref, kp_ref, kc_ref, kn_ref, vp_ref, vc_ref, vn_ref, bias_ref, o_ref,
                    kbuf_ref, vbuf_ref, *, n_grid_rows):
    rb = pl.program_id(2)
    blk = NA_ROWS * GRID_W
    win = NA_ROWS * GRID_W
    kbuf_ref[0:blk] = kp_ref[0]
    kbuf_ref[blk:2 * blk] = kc_ref[0]
    kbuf_ref[2 * blk:3 * blk] = kn_ref[0]
    vbuf_ref[0:blk] = vp_ref[0]
    vbuf_ref[blk:2 * blk] = vc_ref[0]
    vbuf_ref[2 * blk:3 * blk] = vn_ref[0]
    lo = _lane_iota((GRID_W, LANES)) < HEAD_DIM
    for u in range(NA_ROWS):
        r = rb * NA_ROWS + u
        rs = jnp.clip(r - NA_ROWS // 2, 0, n_grid_rows - NA_ROWS)
        off = pl.multiple_of((rs - (rb - 1) * NA_ROWS) * GRID_W, GRID_W)
        ro0 = rs - r + NA_ROWS - 1
        kw = kbuf_ref[pl.ds(off, win)]
        vw = vbuf_ref[pl.ds(off, win)]
        q = q_ref[0, u * GRID_W:(u + 1) * GRID_W].astype(F32)
        qs = jnp.concatenate([jnp.where(lo, q, 0.0), jnp.where(lo, 0.0, q)], axis=0).astype(BF16)
        s = lax.dot_general(qs, kw, NT_DIMS, preferred_element_type=F32)
        bias = jnp.concatenate(
            [jnp.concatenate([bias_ref[h, ro0 + 2 * t] for t in range(NA_ROWS // 2)], axis=1)
             for h in range(2)], axis=0)
        s = s + bias
        m = jnp.max(s, axis=-1, keepdims=True)
        p = jnp.exp(s - m)
        l = jnp.sum(p, axis=-1, keepdims=True)
        o = jnp.dot(p.astype(BF16), vw, preferred_element_type=F32) / l
        o_ref[0, u * GRID_W:(u + 1) * GRID_W] = jnp.where(lo, o[0:GRID_W], o[GRID_W:]).astype(BF16)


def _na_attention(heads, bias, batch, seq):
    n_tok = heads.shape[1]
    n_grid_rows = seq // GRID_W
    nrb = n_grid_rows // NA_ROWS
    blk = NA_ROWS * GRID_W

    def kv_spec(first_pair, shift):
        return pl.BlockSpec(
            (1, blk, LANES),
            lambda b, p, rb: (first_pair + p, b * nrb + jnp.clip(rb + shift, 0, nrb - 1), 0))

    body = functools.partial(_na_attn_kernel, n_grid_rows=n_grid_rows)
    return pl.pallas_call(
        body,
        out_shape=jax.ShapeDtypeStruct((N_MIX_PAIRS, n_tok, LANES), BF16),
        grid=(batch, N_MIX_PAIRS, nrb),
        in_specs=[
            pl.BlockSpec((1, blk, LANES), lambda b, p, rb: (p, b * nrb + rb, 0)),
            kv_spec(N_MIX_PAIRS, -1), kv_spec(N_MIX_PAIRS, 0), kv_spec(N_MIX_PAIRS, 1),
            kv_spec(2 * N_MIX_PAIRS, -1), kv_spec(2 * N_MIX_PAIRS, 0), kv_spec(2 * N_MIX_PAIRS, 1),
            pl.BlockSpec((2,) + bias.shape[1:], lambda b, p, rb: (p, 0, 0, 0)),
        ],
        out_specs=pl.BlockSpec((1, blk, LANES), lambda b, p, rb: (p, b * nrb + rb, 0)),
        scratch_shapes=[pltpu.VMEM((3 * blk, LANES), BF16), pltpu.VMEM((3 * blk, LANES), BF16)],
        compiler_params=_cparams("parallel", "parallel", "arbitrary"),
        name="neighbourhood_attention",
    )(heads, heads, heads, heads, heads, heads, heads, bias)


def _na_bias_table(rpb):
    col = jnp.arange(GRID_W)
    col_start = jnp.clip(col - NA_COLS // 2, 0, GRID_W - NA_COLS)
    col_mask = (col[None, :] >= col_start[:, None]) & (col[None, :] < col_start[:, None] + NA_COLS)
    col_bias_idx = jnp.clip(col[None, :] - col[:, None] + NA_COLS - 1, 0, 2 * NA_COLS - 2)
    tiles = rpb.astype(F32)[:, :, col_bias_idx]
    tiles = jnp.where(col_mask[None, None], tiles, NEG_BIG)
    return jnp.concatenate([tiles[:, :-1], tiles[:, 1:]], axis=-1)


def _outproj_kernel(mix_ref, qm_ref, kv_ref, x_ref, w_ref, g_ref, b_ref, o_ref, *, alpha):
    tm = x_ref.shape[0]
    lo = _lane_iota((tm, LANES)) < HEAD_DIM
    parts = [mix_ref[p] for p in range(mix_ref.shape[0])]
    kv = kv_ref[0]
    for t in range(N_MEM_PAIRS):
        q = qm_ref[t].astype(F32)
        qs = jnp.concatenate([jnp.where(lo, q, 0.0), jnp.where(lo, 0.0, q)], axis=0).astype(BF16)
        kt = kv[:, t * LANES:(t + 1) * LANES]
        vt = kv[:, MEM_WIDTH + t * LANES:MEM_WIDTH + (t + 1) * LANES]
        s = lax.dot_general(qs, kt, NT_DIMS, preferred_element_type=F32)
        m = jnp.max(s, axis=-1, keepdims=True)
        p = jnp.exp(s - m)
        l = jnp.sum(p, axis=-1, keepdims=True)
        o = jnp.dot(p.astype(BF16), vt, preferred_element_type=F32) / l
        parts.append(jnp.where(lo, o[0:tm], o[tm:]).astype(BF16))
    attn = jnp.concatenate(parts, axis=1)
    h = jnp.dot(attn, w_ref[...], preferred_element_type=F32)
    o_ref[...] = _layer_norm(alpha * x_ref[...] + h, g_ref[...], b_ref[...])


def _out_projection(mix, heads, kv, x, w_out, g, b, batch, seq, tm, alpha):
    n_tok, d = x.shape
    nt = seq // tm
    n_pairs = heads.shape[0]
    n_mem = kv.shape[1]
    body = functools.partial(_outproj_kernel, alpha=alpha)
    return pl.pallas_call(
        body,
        out_shape=jax.ShapeDtypeStruct((n_tok, d), F32),
        grid=(batch, nt),
        in_specs=[
            pl.BlockSpec((N_MIX_PAIRS, tm, LANES), lambda bi, i: (0, bi * nt + i, 0)),
            pl.BlockSpec((N_MEM_PAIRS, tm, LANES),
                         lambda bi, i: (n_pairs // N_MEM_PAIRS - 1, bi * nt + i, 0)),
            pl.BlockSpec((1, n_mem, 2 * MEM_WIDTH), lambda bi, i: (bi, 0, 0)),
            pl.BlockSpec((tm, d), lambda bi, i: (bi * nt + i, 0)),
            pl.BlockSpec(w_out.shape, lambda bi, i: (0, 0)),
            pl.BlockSpec((1, d), lambda bi, i: (0, 0)),
            pl.BlockSpec((1, d), lambda bi, i: (0, 0)),
        ],
        out_specs=pl.BlockSpec((tm, d), lambda bi, i: (bi * nt + i, 0)),
        compiler_params=_cparams("parallel", "parallel"),
        name="mem_attention_out_projection",
    )(mix, heads, kv, x, w_out, g, b)


ROW_TILE = 1024
SC_WINDOW = 128
HALF = D_MODEL // 2


def _pack_half(v):
    q = HALF // 2
    return pltpu.pack_elementwise([v[:, :q], v[:, q:]], packed_dtype=BF16)


def _unpack_half(w):
    return jnp.concatenate(
        [pltpu.unpack_elementwise(w, index=h, packed_dtype=BF16, unpacked_dtype=F32)
         for h in range(2)], axis=1)

def _first_index_of_max(vals, index, sentinel, axes):
    mx = vals
    for ax in axes:
        mx = jnp.max(mx, axis=ax, keepdims=True)
    idx = jnp.where(vals == mx, index, sentinel)
    for ax in axes:
        idx = jnp.min(idx, axis=ax, keepdims=True)
    return mx, idx


def _route_kernel(x_ref, rth_ref, rtl_ref, rbias_ref, tri_ref,
                  xp_ref, e8_ref, r8_ref, w8_ref, cnt_ref, run_ref):
    @pl.when(pl.program_id(0) == 0)
    def _():
        run_ref[...] = jnp.zeros(run_ref.shape, F32)

    x = x_ref[...]
    tm = x.shape[0]
    xp_ref[0] = _pack_half(x[:, :HALF])
    xp_ref[1] = _pack_half(x[:, HALF:])
    xh = x.astype(BF16)
    xl = (x - xh.astype(F32)).astype(BF16)
    rth = rth_ref[...]
    logits = (lax.dot_general(rth, xh, NT_DIMS, preferred_element_type=F32)
              + lax.dot_general(rth, xl, NT_DIMS, preferred_element_type=F32)
              + lax.dot_general(rtl_ref[...], xh, NT_DIMS, preferred_element_type=F32))
    scores = jax.nn.sigmoid(logits)
    choice = scores + rbias_ref[...]
    shape3 = (N_GROUPS, PER_GROUP, tm)
    c3 = choice.reshape(shape3)
    s3 = scores.reshape(shape3)
    e_in_g = lax.broadcasted_iota(jnp.int32, shape3, 1).astype(F32)
    g_idx = lax.broadcasted_iota(jnp.int32, (N_GROUPS, 1, tm), 0).astype(F32)
    e_idx = lax.broadcasted_iota(jnp.int32, shape3, 0).astype(F32) * PER_GROUP + e_in_g
    m1, i1 = _first_index_of_max(c3, e_in_g, float(PER_GROUP), (1,))
    m2 = jnp.max(jnp.where(e_in_g == i1, -jnp.inf, c3), axis=1, keepdims=True)
    cur = m1 + m2
    gsel = jnp.zeros(cur.shape, F32)
    for _ in range(TOPK_GROUPS):
        _, gi = _first_index_of_max(cur, g_idx, float(N_GROUPS), (0,))
        hit = g_idx == gi
        gsel = jnp.where(hit, 1.0, gsel)
        cur = jnp.where(hit, -jnp.inf, cur)
    cur = jnp.where(gsel > 0.0, c3, -jnp.inf)
    sel = jnp.zeros(shape3, F32)
    picks = []
    for _ in range(TOP_K):
        _, ei = _first_index_of_max(cur, e_idx, float(N_EXPERTS), (1, 0))
        hit = e_idx == ei
        sel = jnp.where(hit, 1.0, sel)
        cur = jnp.where(hit, -jnp.inf, cur)
        picks.append(ei)
    w = sel * s3
    denom = jnp.sum(jnp.sum(w, axis=1, keepdims=True), axis=0, keepdims=True)
    gates3 = w / denom * ROUTED_SCALE
    sel2 = sel.reshape(N_EXPERTS, tm)
    before = jnp.dot(sel2.astype(BF16), tri_ref[...], preferred_element_type=F32)
    rank3 = (before + run_ref[:, 0:1]).reshape(shape3)
    run_ref[...] = run_ref[...] + jnp.sum(sel2, axis=1, keepdims=True)
    cnt_ref[...] = run_ref[...]

    def pick(values, hit):
        return jnp.sum(jnp.sum(jnp.where(hit, values, 0.0), axis=1, keepdims=True), axis=0)

    for k, ei in enumerate(picks):
        hit = e_idx == ei
        e8_ref[k:k + 1, :] = ei.reshape(1, tm).astype(jnp.int32)
        r8_ref[k:k + 1, :] = pick(rank3, hit).astype(jnp.int32)
        w8_ref[k:k + 1, :] = pick(gates3, hit)


def _route(x, rth, rtl, rbias, tm):
    n_tok, d = x.shape
    tri = (jnp.arange(tm)[:, None] < jnp.arange(tm)[None, :]).astype(BF16)
    const2 = lambda i: (0, 0)
    return pl.pallas_call(
        _route_kernel,
        out_shape=(
            jax.ShapeDtypeStruct((2, n_tok, HALF // 2), jnp.int32),
            jax.ShapeDtypeStruct((TOP_K, n_tok), jnp.int32),
            jax.ShapeDtypeStruct((TOP_K, n_tok), jnp.int32),
            jax.ShapeDtypeStruct((TOP_K, n_tok), F32),
            jax.ShapeDtypeStruct((N_EXPERTS, LANES), F32),
        ),
        grid=(n_tok // tm,),
        in_specs=[
            pl.BlockSpec((tm, d), lambda i: (i, 0)),
            pl.BlockSpec(rth.shape, const2),
            pl.BlockSpec(rtl.shape, const2),
            pl.BlockSpec(rbias.shape, const2),
            pl.BlockSpec((tm, tm), const2),
        ],
        out_specs=(
            pl.BlockSpec((2, tm, HALF // 2), lambda i: (0, i, 0)),
            pl.BlockSpec((TOP_K, tm), lambda i: (0, i)),
            pl.BlockSpec((TOP_K, tm), lambda i: (0, i)),
            pl.BlockSpec((TOP_K, tm), lambda i: (0, i)),
            pl.BlockSpec((N_EXPERTS, LANES), const2),
        ),
        scratch_shapes=[pltpu.VMEM((N_EXPERTS, LANES), F32)],
        compiler_params=_cparams("arbitrary"),
        name="moe_router",
    )(x, rth, rtl, rbias, tri)


def _positions_kernel(start_ref, e8_ref, r8_ref, idx_ref, *, n_rows):
    e8 = e8_ref[...]
    pos = r8_ref[...]
    for e in range(N_EXPERTS):
        pos = pos + jnp.where(e8 == e, start_ref[e], 0)
    idx_ref[0] = pos
    idx_ref[1] = pos + n_rows


def _positions(start, e8, r8, n_rows, tn):
    n_tok = e8.shape[1]
    grid_spec = pltpu.PrefetchScalarGridSpec(
        num_scalar_prefetch=1,
        grid=(n_tok // tn,),
        in_specs=[pl.BlockSpec((TOP_K, tn), lambda i, st: (0, i)),
                  pl.BlockSpec((TOP_K, tn), lambda i, st: (0, i))],
        out_specs=pl.BlockSpec((2, TOP_K, tn), lambda i, st: (0, 0, i)),
    )
    return pl.pallas_call(
        functools.partial(_positions_kernel, n_rows=n_rows),
        out_shape=jax.ShapeDtypeStruct((2, TOP_K, n_tok), jnp.int32),
        grid_spec=grid_spec,
        compiler_params=_cparams("parallel"),
        name="moe_positions",
    )(start, e8, r8)


def _experts_kernel(te_ref, used_ref, xs_ref, wg_ref, wu_ref, wd_ref, ys_ref):
    @pl.when(pl.program_id(0) < used_ref[0])
    def _():
        x = jnp.concatenate([_unpack_half(xs_ref[0]), _unpack_half(xs_ref[1])], axis=1).astype(BF16)
        h = (jax.nn.silu(jnp.dot(x, wg_ref[0], preferred_element_type=F32))
             * jnp.dot(x, wu_ref[0], preferred_element_type=F32))
        y = jnp.dot(h.astype(BF16), wd_ref[0], preferred_element_type=F32)
        ys_ref[0] = _pack_half(y[:, :HALF])
        ys_ref[1] = _pack_half(y[:, HALF:])


def _experts(tile_expert, n_used, xs, eg, eu, ed):
    n_rows = xs.shape[1]
    _, d, f = eg.shape
    grid_spec = pltpu.PrefetchScalarGridSpec(
        num_scalar_prefetch=2,
        grid=(n_rows // ROW_TILE,),
        in_specs=[
            pl.BlockSpec((2, ROW_TILE, HALF // 2), lambda i, te, nu: (0, i, 0)),
            pl.BlockSpec((1, d, f), lambda i, te, nu: (te[i], 0, 0)),
            pl.BlockSpec((1, d, f), lambda i, te, nu: (te[i], 0, 0)),
            pl.BlockSpec((1, f, d), lambda i, te, nu: (te[i], 0, 0)),
        ],
        out_specs=pl.BlockSpec((2, ROW_TILE, HALF // 2), lambda i, te, nu: (0, i, 0)),
    )
    return pl.pallas_call(
        _experts_kernel,
        out_shape=jax.ShapeDtypeStruct(xs.shape, jnp.int32),
        grid_spec=grid_spec,
        compiler_params=_cparams("parallel"),
        name="moe_experts",
    )(tile_expert, n_used, xs, eg, eu, ed)


def _combine_kernel(x_ref, yg_ref, w8_ref, sg_ref, su_ref, sd_ref, g_ref, b_ref, o_ref, *, alpha):
    x = x_ref[...]
    tm = x.shape[0]
    xh = x.astype(BF16)
    hs = (jax.nn.silu(jnp.dot(xh, sg_ref[...], preferred_element_type=F32))
          * jnp.dot(xh, su_ref[...], preferred_element_type=F32))
    acc = jnp.dot(hs.astype(BF16), sd_ref[...], preferred_element_type=F32)
    w8 = jnp.concatenate([w8_ref[...], jnp.zeros((LANES - TOP_K, tm), F32)], axis=0).T
    for k in range(TOP_K):
        yk = jnp.concatenate([_unpack_half(yg_ref[0, k]), _unpack_half(yg_ref[1, k])], axis=1)
        acc = acc + w8[:, k:k + 1] * yk
    o_ref[...] = _layer_norm(alpha * x + acc, g_ref[...], b_ref[...])


def _combine(x, yg, w8, sg, su, sd, g, b, tm, alpha):
    n_tok, d = x.shape
    const2 = lambda i: (0, 0)
    return pl.pallas_call(
        functools.partial(_combine_kernel, alpha=alpha),
        out_shape=jax.ShapeDtypeStruct((n_tok, d), F32),
        grid=(n_tok // tm,),
        in_specs=[
            pl.BlockSpec((tm, d), lambda i: (i, 0)),
            pl.BlockSpec((2, TOP_K, tm, HALF // 2), lambda i: (0, 0, i, 0)),
            pl.BlockSpec((TOP_K, tm), lambda i: (0, i)),
            pl.BlockSpec(sg.shape, const2),
            pl.BlockSpec(su.shape, const2),
            pl.BlockSpec(sd.shape, const2),
            pl.BlockSpec((1, d), const2),
            pl.BlockSpec((1, d), const2),
        ],
        out_specs=pl.BlockSpec((tm, d), lambda i: (i, 0)),
        compiler_params=_cparams("parallel"),
        name="moe_combine",
    )(x, yg, w8, sg, su, sd, g, b)


def _sc_mesh():
    return plsc.VectorSubcoreMesh(core_axis_name="c", subcore_axis_name="s")


def _sc_gather_rows(table, idx):
    n = idx.shape[0]
    d = table.shape[1]

    @functools.partial(pl.kernel, out_type=jax.ShapeDtypeStruct((n, d), table.dtype),
                       mesh=_sc_mesh())
    def gather(table_hbm, idx_hbm, out_hbm):
        def body(idx_vmem, out_vmem):
            pltpu.sync_copy(table_hbm.at[idx_vmem.at[0]], out_vmem)

        pltpu.emit_pipeline(
            body, grid=(n // SC_WINDOW,),
            in_specs=[pl.BlockSpec((1, SC_WINDOW), index_map=lambda i: (0, i))],
            out_specs=[pl.BlockSpec((SC_WINDOW, d), index_map=lambda i: (i, 0))],
            core_axis_name=("c", "s"), dimension_semantics=(pltpu.PARALLEL,), trace_scopes=False,
        )(idx_hbm, out_hbm)

    return gather(table, idx.reshape(1, n))


def _sc_scatter_rows(rows, idx, n_out, src_block):
    n = idx.shape[0]
    d = rows.shape[1]

    @functools.partial(pl.kernel, out_type=jax.ShapeDtypeStruct((n_out, d), rows.dtype),
                       mesh=_sc_mesh())
    def scatter(rows_hbm, idx_hbm, out_hbm):
        def body(rows_vmem, idx_vmem):
            pltpu.sync_copy(rows_vmem, out_hbm.at[idx_vmem.at[0]])

        pltpu.emit_pipeline(
            body, grid=(n // SC_WINDOW,),
            in_specs=[pl.BlockSpec((SC_WINDOW, d), index_map=lambda i: (src_block(i), 0)),
                      pl.BlockSpec((1, SC_WINDOW), index_map=lambda i: (0, i))],
            out_specs=[],
            core_axis_name=("c", "s"), dimension_semantics=(pltpu.PARALLEL,), trace_scopes=False,
        )(rows_hbm, idx_hbm)

    return scatter(rows, idx.reshape(1, n))


def _moe_dispatch(x, rth, rtl, rbias, tm):
    n_tok, d = x.shape
    n_rows = TOP_K * n_tok + N_EXPERTS * ROW_TILE
    n_tiles = n_rows // ROW_TILE
    xp, e8, r8, w8, cnt = _route(x, rth, rtl, rbias, tm)
    counts = cnt[:, 0].astype(jnp.int32)
    padded = (counts + ROW_TILE - 1) // ROW_TILE * ROW_TILE
    ends = jnp.cumsum(padded)
    start = ends - padded
    n_used = (ends[-1:] // ROW_TILE).astype(jnp.int32)
    tile_ids = jnp.arange(n_tiles, dtype=jnp.int32)
    tile_expert = jnp.minimum(
        jnp.sum((tile_ids[:, None] >= (ends // ROW_TILE)[None, :]).astype(jnp.int32), axis=1),
        N_EXPERTS - 1)
    idx = _positions(start, e8, r8, n_rows, _pick_tile(n_tok, 2048)).reshape(-1)
    blocks_per_half = n_tok // SC_WINDOW
    xs = _sc_scatter_rows(
        xp.reshape(2 * n_tok, HALF // 2), idx, 2 * n_rows,
        lambda i: (i // (TOP_K * blocks_per_half)) * blocks_per_half + i % blocks_per_half)
    return xs.reshape(2, n_rows, HALF // 2), idx, tile_expert, n_used, w8


def _moe_experts(xs, idx, tile_expert, n_used, eg, eu, ed):
    n_rows = xs.shape[1]
    ys = _experts(tile_expert, n_used, xs, eg, eu, ed)
    yg = _sc_gather_rows(ys.reshape(2 * n_rows, HALF // 2), idx)
    return yg.reshape(2, TOP_K, idx.shape[0] // (2 * TOP_K), HALF // 2)


def _rope_tables(seq):
    t = jnp.arange(seq)
    row = (t // GRID_W).astype(F32)
    colp = (t % GRID_W).astype(F32)
    inv = ROPE_THETA ** (-jnp.arange(0, ROPE_AXIS_DIM, 2, dtype=F32) / ROPE_AXIS_DIM)
    ang = jnp.concatenate([row[:, None] * inv[None], colp[:, None] * inv[None]], axis=-1)
    ang = jnp.repeat(ang, 2, axis=-1)
    sign = jnp.where(jnp.arange(HEAD_DIM) % 2 == 0, -1.0, 1.0).astype(F32)
    cos = jnp.tile(jnp.cos(ang), (1, 2))
    sin = jnp.tile(jnp.sin(ang) * sign[None], (1, 2))
    return cos, sin


def _gqa_head_order():
    order = []
    for kvp in range(N_KV_HEADS // 2):
        base = 2 * GQA_GROUP * kvp
        for g in range(GQA_GROUP):
            order += [base + g, base + GQA_GROUP + g]
    return np.array(order)


def _pick_tile(n, target):
    t = min(n, target)
    while n % t:
        t //= 2
    return t


def kernel(x_prompt, x_sample, mem_prompt, mem_sample, a_w_in, a_lambda, a_subln, b_w_in, b_rpb,
           c_w_in, c_q_norm, c_k_norm, w_mem_kv, w_out, ln1_g, ln1_b, router, router_bias,
           e_gate, e_up, e_down, s_gate, s_up, s_down, ln2_g, ln2_b):
    depth = w_out.shape[0]
    alpha = (2.0 * depth) ** 0.25
    seq = x_prompt.shape[1]
    d = x_prompt.shape[2]
    assert x_sample.shape[1] == seq and seq % (NA_ROWS * GRID_W) == 0
    n_mem = mem_prompt.shape[1]
    batches = [x_prompt.shape[0], x_sample.shape[0]]
    xs_ = [x_prompt.reshape(-1, d), x_sample.reshape(-1, d)]
    mems = [mem_prompt.reshape(-1, d), mem_sample.reshape(-1, d)]

    tm = _pick_tile(seq, 512)
    tk = _pick_tile(seq, 512)
    head_order = _gqa_head_order()
    col_order = (head_order[:, None] * HEAD_DIM + np.arange(HEAD_DIM)[None]).reshape(-1)
    cos, sin = _rope_tables(seq)
    slopes = _alibi_slopes(N_MIX_HEADS).reshape(N_MIX_PAIRS, 2)
    slope_tab = np.zeros((N_MIX_PAIRS, 8, LANES), np.float32)
    slope_tab[:, 0:2, :] = slopes[:, :, None]
    slope_tab = jnp.asarray(slope_tab) * LOG2E
    ones_row = functools.partial(jnp.ones, dtype=F32)
    mem_q_scale = jnp.full((MEM_WIDTH,), HEAD_DIM ** -0.5, F32)

    for i in range(depth):
        mixer, occ = i % N_MIXERS, i // N_MIXERS
        w_o = w_out[i]
        rope = None
        if mixer == 0:
            w_in = a_w_in[occ]
            q_scale = DIFF_QK_DIM ** -0.5 * LOG2E
            gain = jnp.tile(a_subln[occ].astype(F32), 2)[None]
            lam = a_lambda[occ].astype(F32)
        elif mixer == 1:
            w_in = b_w_in[occ]
            q_scale = HEAD_DIM ** -0.5
            na_bias = _na_bias_table(b_rpb[occ])
        else:
            w_in = c_w_in[occ]
            w_in = jnp.concatenate([w_in[:, :MIX_WIDTH][:, col_order], w_in[:, MIX_WIDTH:]], axis=1)
            w_o = jnp.concatenate([w_o[:MIX_WIDTH][col_order], w_o[MIX_WIDTH:]], axis=0)
            q_scale = HEAD_DIM ** -0.5 * LOG2E
            qg = jnp.tile(c_q_norm[occ].astype(F32), 2)[None, None]
            kg = jnp.tile(c_k_norm[occ].astype(F32), 2)[None, None]
            gains = jnp.concatenate([jnp.tile(qg, (N_MIX_PAIRS, 1, 1)),
                                     jnp.tile(kg, (N_KV_HEADS // 2, 1, 1))], axis=0)
            rope = (gains, cos, sin)
        colscale = jnp.concatenate([
            jnp.full((MIX_WIDTH,), q_scale, F32),
            ones_row((w_in.shape[1] - MIX_WIDTH - MEM_WIDTH,)), mem_q_scale])[None]
        w_in = w_in.astype(BF16)
        w_o = w_o.astype(BF16)
        w_kv = w_mem_kv[i].astype(BF16)
        rt = router[i].T.astype(F32)
        rth = rt.astype(BF16)
        rtl = (rt - rth.astype(F32)).astype(BF16)
        rbias = router_bias[i].astype(F32)[:, None]
        eg, eu, ed = e_gate[i].astype(BF16), e_up[i].astype(BF16), e_down[i].astype(BF16)
        sg, su, sd = s_gate[i].astype(BF16), s_up[i].astype(BF16), s_down[i].astype(BF16)

        dispatched = []
        for s in range(len(xs_)):
            x, batch = xs_[s], batches[s]
            heads = _in_projection(x, w_in, colscale, tm, seq, rope=rope)
            if mixer == 0:
                mix = _diff_attention(heads, slope_tab, lam, gain, batch, seq,
                                      _pick_tile(seq, 256), _pick_tile(seq, 1024),
                                      _diff_lambda_init(i))
            elif mixer == 1:
                mix = _na_attention(heads, na_bias, batch, seq)
            else:
                mix = _gqa_attention(heads, batch, seq, _pick_tile(seq, 512), tk)
            kv = _matmul(mems[s], w_kv, _pick_tile(batch * n_mem, 512))
            kv = kv.reshape(batch, n_mem, 2 * MEM_WIDTH)
            x = _out_projection(mix, heads, kv, x, w_o, ln1_g[i][None], ln1_b[i][None],
                                batch, seq, tm, alpha)
            xs_[s] = x
            dispatched.append(_moe_dispatch(x, rth, rtl, rbias, tm))
        gathered = [_moe_experts(*disp[:4], eg, eu, ed) for disp in dispatched]
        for s in range(len(xs_)):
            xs_[s] = _combine(xs_[s], gathered[s], dispatched[s][4], sg, su, sd,
                              ln2_g[i][None], ln2_b[i][None], tm, alpha)

    return tuple(x.reshape(b, seq, d) for x, b in zip(xs_, batches))
```

```python
import functools
import math

import numpy as np
import jax
import jax.numpy as jnp
from jax import lax
from jax.experimental import pallas as pl
from jax.experimental.pallas import tpu as pltpu
from jax.experimental.pallas import tpu_sc as plsc

F32 = jnp.float32
BF16 = jnp.bfloat16

D_MODEL = 1024
HEAD_DIM = 64
LANES = 128
N_MIX_HEADS = 12
N_MIX_PAIRS = N_MIX_HEADS // 2
N_MEM_HEADS = 4
N_MEM_PAIRS = N_MEM_HEADS // 2
MIX_WIDTH = N_MIX_HEADS * HEAD_DIM
MEM_WIDTH = N_MEM_HEADS * HEAD_DIM
N_MIXERS = 3
DIFF_QK_DIM = HEAD_DIM // 2
GRID_W = 64
NA_ROWS = 8
NA_COLS = 16
N_KV_HEADS = 4
GQA_GROUP = N_MIX_HEADS // N_KV_HEADS
ROPE_THETA = 10000.0
ROPE_AXIS_DIM = HEAD_DIM // 2
N_EXPERTS = 64
TOP_K = 8
N_GROUPS = 8
TOPK_GROUPS = 4
PER_GROUP = N_EXPERTS // N_GROUPS
D_EXPERT = 256
ROUTED_SCALE = 2.5
LN_EPS = 1e-5
RMS_EPS = 1e-6
NEG_BIG = -1e30
LOG2E = math.log2(math.e)

VMEM_LIMIT = 48 * 1024 * 1024

NT_DIMS = (((1,), (1,)), ((), ()))
TN_DIMS = (((0,), (0,)), ((), ()))


def _cparams(*sem):
    return pltpu.CompilerParams(dimension_semantics=sem, vmem_limit_bytes=VMEM_LIMIT)


def _alibi_slopes(n):
    def pow2_slopes(m):
        start = 2.0 ** (-8.0 / m)
        return [start ** (i + 1) for i in range(m)]
    if math.log2(n).is_integer():
        s = pow2_slopes(n)
    else:
        c = 2 ** math.floor(math.log2(n))
        s = pow2_slopes(c) + pow2_slopes(2 * c)[0::2][: n - c]
    return np.array(s, np.float32)


def _diff_lambda_init(layer_idx):
    return 0.8 - 0.6 * math.exp(-0.3 * layer_idx)


def _lane_iota(shape):
    return lax.broadcasted_iota(jnp.int32, shape, len(shape) - 1)


def _layer_norm(y, g, b):
    mu = jnp.mean(y, axis=-1, keepdims=True)
    yc = y - mu
    var = jnp.mean(yc * yc, axis=-1, keepdims=True)
    return yc * lax.rsqrt(var + LN_EPS) * g + b


def _head_rms_scale(x):
    lo = _lane_iota(x.shape) < HEAD_DIM
    xx = x * x
    ss_lo = jnp.sum(jnp.where(lo, xx, 0.0), axis=-1, keepdims=True)
    ss_hi = jnp.sum(jnp.where(lo, 0.0, xx), axis=-1, keepdims=True)
    inv = 1.0 / HEAD_DIM
    return jnp.where(lo, lax.rsqrt(ss_lo * inv + RMS_EPS), lax.rsqrt(ss_hi * inv + RMS_EPS))


def _inproj_kernel(x_ref, w_ref, cs_ref, o_ref):
    r = jnp.dot(x_ref[...].astype(BF16), w_ref[...], preferred_element_type=F32)
    r = r * cs_ref[...]
    for p in range(o_ref.shape[0]):
        o_ref[p] = r[:, p * LANES:(p + 1) * LANES].astype(BF16)


def _inproj_rope_kernel(x_ref, w_ref, cs_ref, g_ref, cos_ref, sin_ref, o_ref, *, n_rope_pairs):
    r = jnp.dot(x_ref[...].astype(BF16), w_ref[...], preferred_element_type=F32)
    cs = cs_ref[...]
    cos = cos_ref[...]
    sin = sin_ref[...]
    even = (_lane_iota(cos.shape) % 2) == 0
    for p in range(o_ref.shape[0]):
        xp = r[:, p * LANES:(p + 1) * LANES]
        if p < n_rope_pairs:
            xn = xp * _head_rms_scale(xp) * g_ref[p]
            partner = jnp.where(even, pltpu.roll(xn, LANES - 1, 1), pltpu.roll(xn, 1, 1))
            xp = xn * cos + partner * sin
        o_ref[p] = (xp * cs[:, p * LANES:(p + 1) * LANES]).astype(BF16)


def _in_projection(x, w, colscale, tm, seq, rope=None):
    n_tok, d = x.shape
    width = w.shape[1]
    n_pairs = width // LANES
    in_specs = [
        pl.BlockSpec((tm, d), lambda i: (i, 0)),
        pl.BlockSpec((d, width), lambda i: (0, 0)),
        pl.BlockSpec((1, width), lambda i: (0, 0)),
    ]
    args = [x, w, colscale]
    if rope is None:
        body = _inproj_kernel
    else:
        gains, cos, sin = rope
        tiles_per_seq = seq // tm
        body = functools.partial(_inproj_rope_kernel, n_rope_pairs=gains.shape[0])
        in_specs += [
            pl.BlockSpec(gains.shape, lambda i: (0, 0, 0)),
            pl.BlockSpec((tm, LANES), lambda i: (i % tiles_per_seq, 0)),
            pl.BlockSpec((tm, LANES), lambda i: (i % tiles_per_seq, 0)),
        ]
        args += [gains, cos, sin]
    return pl.pallas_call(
        body,
        out_shape=jax.ShapeDtypeStruct((n_pairs, n_tok, LANES), BF16),
        grid=(n_tok // tm,),
        in_specs=in_specs,
        out_specs=pl.BlockSpec((n_pairs, tm, LANES), lambda i: (0, i, 0)),
        compiler_params=_cparams("parallel"),
        name="in_projection",
    )(*args)


def _matmul_kernel(x_ref, w_ref, o_ref):
    o_ref[...] = jnp.dot(x_ref[...].astype(BF16), w_ref[...],
                         preferred_element_type=F32).astype(o_ref.dtype)


def _matmul(x, w, tm):
    m, k = x.shape
    n = w.shape[1]
    return pl.pallas_call(
        _matmul_kernel,
        out_shape=jax.ShapeDtypeStruct((m, n), BF16),
        grid=(m // tm,),
        in_specs=[pl.BlockSpec((tm, k), lambda i: (i, 0)), pl.BlockSpec((k, n), lambda i: (0, 0))],
        out_specs=pl.BlockSpec((tm, n), lambda i: (i, 0)),
        compiler_params=_cparams("parallel"),
        name="mem_kv_projection",
    )(x, w)


FLASH_UNROLL = 2


def _stack_masked_q(q_ref, qs_ref, tq):
    n_qpairs = q_ref.shape[0]
    per_pair = qs_ref.shape[0] // tq // n_qpairs
    lanes_per_map = LANES // per_pair
    for t in range(n_qpairs):
        q = q_ref[t].astype(F32)
        part = _lane_iota(q.shape) // lanes_per_map
        for a in range(per_pair):
            r0 = (t * per_pair + a) * tq
            qs_ref[r0:r0 + tq] = jnp.where(part == a, q, 0.0).astype(BF16)


def _softmax_accumulate(s, v, m, l, acc_ref):
    m_new = jnp.maximum(m, jnp.max(s, axis=0, keepdims=True))
    alpha = jnp.exp2(m - m_new)
    p = jnp.exp2(s - m_new)
    l_new = alpha * l + jnp.sum(p, axis=0, keepdims=True)
    pv = lax.dot_general(v, p.astype(BF16), TN_DIMS, preferred_element_type=F32)
    acc_ref[...] = alpha * acc_ref[...] + pv
    return m_new, l_new


def _diff_attn_kernel(q_ref, k_ref, v_ref, slope_ref, lam_ref, g_ref, o_ref, qs_ref, acc_ref,
                      *, tq, tk, seq, lambda_init):
    i = pl.program_id(2)
    n_maps = 4
    w = n_maps * tq
    _stack_masked_q(q_ref, qs_ref, tq)
    acc_ref[...] = jnp.zeros(acc_ref.shape, F32)
    slopes = slope_ref[0]
    rel0 = (lax.broadcasted_iota(jnp.int32, (tk, tq), 0)
            - lax.broadcasted_iota(jnp.int32, (tk, tq), 1) - i * tq)

    def chunk(j, carry):
        m, l = carry
        off = pl.multiple_of(j * tk, tk)
        k = k_ref[0, pl.ds(off, tk), :]
        v = v_ref[0, pl.ds(off, tk), :]
        s = lax.dot_general(k, qs_ref[...], NT_DIMS, preferred_element_type=F32)
        dist = jnp.abs(rel0 + j * tk).astype(F32)
        s = jnp.concatenate(
            [s[:, a * tq:(a + 1) * tq] - slopes[a // 2:a // 2 + 1, 0:1] * dist
             for a in range(n_maps)], axis=1)
        return _softmax_accumulate(s, v, m, l, acc_ref)

    carry = (jnp.full((1, w), -jnp.inf, F32), jnp.zeros((1, w), F32))
    m, l = lax.fori_loop(0, seq // tk, chunk, carry, unroll=FLASH_UNROLL)

    o = acc_ref[...] / l
    lam = lam_ref[...]
    lam_full = (jnp.exp(jnp.sum(lam[0:1] * lam[1:2], axis=-1, keepdims=True))
                - jnp.exp(jnp.sum(lam[2:3] * lam[3:4], axis=-1, keepdims=True)) + lambda_init)
    h0 = o[:, 0:tq] - lam_full * o[:, tq:2 * tq]
    h1 = o[:, 2 * tq:3 * tq] - lam_full * o[:, 3 * tq:4 * tq]
    row = lax.broadcasted_iota(jnp.int32, (LANES, tq), 0)
    o = jnp.where(row < HEAD_DIM, h0, h1).T
    o = o * _head_rms_scale(o) * g_ref[...] * (1.0 - lambda_init)
    o_ref[0] = o.astype(BF16)


def _diff_attention(heads, slopes, lam, gain, batch, seq, tq, tk, lambda_init):
    n_tok = heads.shape[1]
    nq = seq // tq
    body = functools.partial(_diff_attn_kernel, tq=tq, tk=tk, seq=seq, lambda_init=lambda_init)
    return pl.pallas_call(
        body,
        out_shape=jax.ShapeDtypeStruct((N_MIX_PAIRS, n_tok, LANES), BF16),
        grid=(batch, N_MIX_PAIRS, nq),
        in_specs=[
            pl.BlockSpec((1, tq, LANES), lambda b, p, i: (p, b * nq + i, 0)),
            pl.BlockSpec((1, seq, LANES), lambda b, p, i: (N_MIX_PAIRS + p, b, 0)),
            pl.BlockSpec((1, seq, LANES), lambda b, p, i: (2 * N_MIX_PAIRS + p, b, 0)),
            pl.BlockSpec((1, 8, LANES), lambda b, p, i: (p, 0, 0)),
            pl.BlockSpec(lam.shape, lambda b, p, i: (0, 0)),
            pl.BlockSpec((1, LANES), lambda b, p, i: (0, 0)),
        ],
        out_specs=pl.BlockSpec((1, tq, LANES), lambda b, p, i: (p, b * nq + i, 0)),
        scratch_shapes=[
            pltpu.VMEM((4 * tq, LANES), BF16),
            pltpu.VMEM((LANES, 4 * tq), F32),
        ],
        compiler_params=_cparams("parallel", "parallel", "parallel"),
        name="diff_attention",
    )(heads, heads, heads, slopes, lam, gain)


def _gqa_attn_kernel(q_ref, k_ref, v_ref, o_ref, qs_ref, acc_ref, *, tq, tk, seq):
    n_qpairs = q_ref.shape[0]
    w = 2 * n_qpairs * tq
    _stack_masked_q(q_ref, qs_ref, tq)
    acc_ref[...] = jnp.zeros(acc_ref.shape, F32)

    def chunk(j, carry):
        m, l = carry
        off = pl.multiple_of(j * tk, tk)
        k = k_ref[0, pl.ds(off, tk), :]
        v = v_ref[0, pl.ds(off, tk), :]
        s = lax.dot_general(k, qs_ref[...], NT_DIMS, preferred_element_type=F32)
        return _softmax_accumulate(s, v, m, l, acc_ref)

    carry = (jnp.full((1, w), -jnp.inf, F32), jnp.zeros((1, w), F32))
    m, l = lax.fori_loop(0, seq // tk, chunk, carry, unroll=FLASH_UNROLL)
    o = acc_ref[...] / l
    row = lax.broadcasted_iota(jnp.int32, (LANES, tq), 0)
    for t in range(n_qpairs):
        lo = o[:, (2 * t) * tq:(2 * t + 1) * tq]
        hi = o[:, (2 * t + 1) * tq:(2 * t + 2) * tq]
        o_ref[t] = jnp.where(row < HEAD_DIM, lo, hi).T.astype(BF16)


def _gqa_attention(heads, batch, seq, tq, tk):
    n_tok = heads.shape[1]
    nq = seq // tq
    n_kv_pairs = N_KV_HEADS // 2
    qp = N_MIX_PAIRS // n_kv_pairs
    body = functools.partial(_gqa_attn_kernel, tq=tq, tk=tk, seq=seq)
    return pl.pallas_call(
        body,
        out_shape=jax.ShapeDtypeStruct((N_MIX_PAIRS, n_tok, LANES), BF16),
        grid=(batch, n_kv_pairs, nq),
        in_specs=[
            pl.BlockSpec((qp, tq, LANES), lambda b, p, i: (p, b * nq + i, 0)),
            pl.BlockSpec((1, seq, LANES), lambda b, p, i: (N_MIX_PAIRS + p, b, 0)),
            pl.BlockSpec((1, seq, LANES), lambda b, p, i: (N_MIX_PAIRS + n_kv_pairs + p, b, 0)),
        ],
        out_specs=pl.BlockSpec((qp, tq, LANES), lambda b, p, i: (p, b * nq + i, 0)),
        scratch_shapes=[
            pltpu.VMEM((2 * qp * tq, LANES), BF16),
            pltpu.VMEM((LANES, 2 * qp * tq), F32),
        ],
        compiler_params=_cparams("parallel", "parallel", "parallel"),
        name="gqa_attention",
    )(heads, heads, heads)


def _na_attn_kernel(q_ref, kp_ref, kc_ref, kn_ref, vp_ref, vc_ref, vn_ref, bias_ref, o_ref,
                    kbuf_ref, vbuf_ref, *, n_grid_rows):
    rb = pl.program_id(2)
    blk = NA_ROWS * GRID_W
    win = NA_ROWS * GRID_W
    kbuf_ref[0:blk] = kp_ref[0]
    kbuf_ref[blk:2 * blk] = kc_ref[0]
    kbuf_ref[2 * blk:3 * blk] = kn_ref[0]
    vbuf_ref[0:blk] = vp_ref[0]
    vbuf_ref[blk:2 * blk] = vc_ref[0]
    vbuf_ref[2 * blk:3 * blk] = vn_ref[0]
    lo = _lane_iota((GRID_W, LANES)) < HEAD_DIM
    offs, scores = [], []
    for u in range(NA_ROWS):
        r = rb * NA_ROWS + u
        rs = jnp.clip(r - NA_ROWS // 2, 0, n_grid_rows - NA_ROWS)
        off = pl.multiple_of((rs - (rb - 1) * NA_ROWS) * GRID_W, GRID_W)
        ro0 = rs - r + NA_ROWS - 1
        q = q_ref[0, u * GRID_W:(u + 1) * GRID_W].astype(F32)
        qs = jnp.concatenate([jnp.where(lo, q, 0.0), jnp.where(lo, 0.0, q)], axis=0).astype(BF16)
        s = lax.dot_general(qs, kbuf_ref[pl.ds(off, win)], NT_DIMS,
                            preferred_element_type=F32)
        bias = jnp.concatenate(
            [jnp.concatenate([bias_ref[h, ro0 + 2 * t] for t in range(NA_ROWS // 2)], axis=1)
             for h in range(2)], axis=0)
        offs.append(off)
        scores.append(s + bias)
    probs = []
    for s in scores:
        p = jnp.exp(s - jnp.max(s, axis=-1, keepdims=True))
        probs.append((p.astype(BF16), jnp.sum(p, axis=-1, keepdims=True)))
    for u, (p, l) in enumerate(probs):
        o = jnp.dot(p, vbuf_ref[pl.ds(offs[u], win)], preferred_element_type=F32) / l
        o_ref[0, u * GRID_W:(u + 1) * GRID_W] = jnp.where(lo, o[0:GRID_W], o[GRID_W:]).astype(BF16)


def _na_attention(heads, bias, batch, seq):
    n_tok = heads.shape[1]
    n_grid_rows = seq // GRID_W
    nrb = n_grid_rows // NA_ROWS
    blk = NA_ROWS * GRID_W

    def kv_spec(first_pair, shift):
        return pl.BlockSpec(
            (1, blk, LANES),
            lambda b, p, rb: (first_pair + p, b * nrb + jnp.clip(rb + shift, 0, nrb - 1), 0))

    body = functools.partial(_na_attn_kernel, n_grid_rows=n_grid_rows)
    return pl.pallas_call(
        body,
        out_shape=jax.ShapeDtypeStruct((N_MIX_PAIRS, n_tok, LANES), BF16),
        grid=(batch, N_MIX_PAIRS, nrb),
        in_specs=[
            pl.BlockSpec((1, blk, LANES), lambda b, p, rb: (p, b * nrb + rb, 0)),
            kv_spec(N_MIX_PAIRS, -1), kv_spec(N_MIX_PAIRS, 0), kv_spec(N_MIX_PAIRS, 1),
            kv_spec(2 * N_MIX_PAIRS, -1), kv_spec(2 * N_MIX_PAIRS, 0), kv_spec(2 * N_MIX_PAIRS, 1),
            pl.BlockSpec((2,) + bias.shape[1:], lambda b, p, rb: (p, 0, 0, 0)),
        ],
        out_specs=pl.BlockSpec((1, blk, LANES), lambda b, p, rb: (p, b * nrb + rb, 0)),
        scratch_shapes=[pltpu.VMEM((3 * blk, LANES), BF16), pltpu.VMEM((3 * blk, LANES), BF16)],
        compiler_params=_cparams("parallel", "parallel", "arbitrary"),
        name="neighbourhood_attention",
    )(heads, heads, heads, heads, heads, heads, heads, bias)


def _na_bias_table(rpb):
    col = jnp.arange(GRID_W)
    col_start = jnp.clip(col - NA_COLS // 2, 0, GRID_W - NA_COLS)
    col_mask = (col[None, :] >= col_start[:, None]) & (col[None, :] < col_start[:, None] + NA_COLS)
    col_bias_idx = jnp.clip(col[None, :] - col[:, None] + NA_COLS - 1, 0, 2 * NA_COLS - 2)
    tiles = rpb.astype(F32)[:, :, col_bias_idx]
    tiles = jnp.where(col_mask[None, None], tiles, NEG_BIG)
    return jnp.concatenate([tiles[:, :-1], tiles[:, 1:]], axis=-1)


def _outproj_kernel(mix_ref, qm_ref, kv_ref, x_ref, w_ref, g_ref, b_ref, o_ref, *, alpha):
    tm = x_ref.shape[0]
    lo = _lane_iota((tm, LANES)) < HEAD_DIM
    parts = [mix_ref[p] for p in range(mix_ref.shape[0])]
    kv = kv_ref[0]
    scores = []
    for t in range(N_MEM_PAIRS):
        q = qm_ref[t].astype(F32)
        qs = jnp.concatenate([jnp.where(lo, q, 0.0), jnp.where(lo, 0.0, q)], axis=0).astype(BF16)
        kt = kv[:, t * LANES:(t + 1) * LANES]
        scores.append(lax.dot_general(qs, kt, NT_DIMS, preferred_element_type=F32))
    probs = []
    for s in scores:
        p = jnp.exp(s - jnp.max(s, axis=-1, keepdims=True))
        probs.append((p.astype(BF16), jnp.sum(p, axis=-1, keepdims=True)))
    for t, (p, l) in enumerate(probs):
        vt = kv[:, MEM_WIDTH + t * LANES:MEM_WIDTH + (t + 1) * LANES]
        o = jnp.dot(p, vt, preferred_element_type=F32) / l
        parts.append(jnp.where(lo, o[0:tm], o[tm:]).astype(BF16))
    attn = jnp.concatenate(parts, axis=1)
    h = jnp.dot(attn, w_ref[...], preferred_element_type=F32)
    o_ref[...] = _layer_norm(alpha * x_ref[...] + h, g_ref[...], b_ref[...])


def _out_projection(mix, heads, kv, x, w_out, g, b, batch, seq, tm, alpha):
    n_tok, d = x.shape
    nt = seq // tm
    n_pairs = heads.shape[0]
    n_mem = kv.shape[1]
    body = functools.partial(_outproj_kernel, alpha=alpha)
    return pl.pallas_call(
        body,
        out_shape=jax.ShapeDtypeStruct((n_tok, d), F32),
        grid=(batch, nt),
        in_specs=[
            pl.BlockSpec((N_MIX_PAIRS, tm, LANES), lambda bi, i: (0, bi * nt + i, 0)),
            pl.BlockSpec((N_MEM_PAIRS, tm, LANES),
                         lambda bi, i: (n_pairs // N_MEM_PAIRS - 1, bi * nt + i, 0)),
            pl.BlockSpec((1, n_mem, 2 * MEM_WIDTH), lambda bi, i: (bi, 0, 0)),
            pl.BlockSpec((tm, d), lambda bi, i: (bi * nt + i, 0)),
            pl.BlockSpec(w_out.shape, lambda bi, i: (0, 0)),
            pl.BlockSpec((1, d), lambda bi, i: (0, 0)),
            pl.BlockSpec((1, d), lambda bi, i: (0, 0)),
        ],
        out_specs=pl.BlockSpec((tm, d), lambda bi, i: (bi * nt + i, 0)),
        compiler_params=_cparams("parallel", "parallel"),
        name="mem_attention_out_projection",
    )(mix, heads, kv, x, w_out, g, b)


ROW_TILE = 1024
SC_WINDOW = 128
HALF = D_MODEL // 2


def _pack_half(v):
    q = HALF // 2
    return pltpu.pack_elementwise([v[:, :q], v[:, q:]], packed_dtype=BF16)


def _unpack_half(w):
    return jnp.concatenate(
        [pltpu.unpack_elementwise(w, index=h, packed_dtype=BF16, unpacked_dtype=F32)
         for h in range(2)], axis=1)

def _first_index_of_max(vals, index, sentinel, axes):
    mx = vals
    for ax in axes:
        mx = jnp.max(mx, axis=ax, keepdims=True)
    idx = jnp.where(vals == mx, index, sentinel)
    for ax in axes:
        idx = jnp.min(idx, axis=ax, keepdims=True)
    return mx, idx


def _route_kernel(x_ref, rth_ref, rtl_ref, rbias_ref, tri_ref,
                  xp_ref, e8_ref, r8_ref, w8_ref, cnt_ref, run_ref):
    @pl.when(pl.program_id(0) == 0)
    def _():
        run_ref[...] = jnp.zeros(run_ref.shape, F32)

    x = x_ref[...]
    tm = x.shape[0]
    xp_ref[0] = _pack_half(x[:, :HALF])
    xp_ref[1] = _pack_half(x[:, HALF:])
    xh = x.astype(BF16)
    xl = (x - xh.astype(F32)).astype(BF16)
    rth = rth_ref[...]
    logits = (lax.dot_general(rth, xh, NT_DIMS, preferred_element_type=F32)
              + lax.dot_general(rth, xl, NT_DIMS, preferred_element_type=F32)
              + lax.dot_general(rtl_ref[...], xh, NT_DIMS, preferred_element_type=F32))
    scores = jax.nn.sigmoid(logits)
    choice = scores + rbias_ref[...]
    shape3 = (N_GROUPS, PER_GROUP, tm)
    c3 = choice.reshape(shape3)
    s3 = scores.reshape(shape3)
    e_in_g = lax.broadcasted_iota(jnp.int32, shape3, 1).astype(F32)
    g_idx = lax.broadcasted_iota(jnp.int32, (N_GROUPS, 1, tm), 0).astype(F32)
    e_idx = lax.broadcasted_iota(jnp.int32, shape3, 0).astype(F32) * PER_GROUP + e_in_g
    m1, i1 = _first_index_of_max(c3, e_in_g, float(PER_GROUP), (1,))
    m2 = jnp.max(jnp.where(e_in_g == i1, -jnp.inf, c3), axis=1, keepdims=True)
    cur = m1 + m2
    gsel = jnp.zeros(cur.shape, F32)
    for _ in range(TOPK_GROUPS):
        _, gi = _first_index_of_max(cur, g_idx, float(N_GROUPS), (0,))
        hit = g_idx == gi
        gsel = jnp.where(hit, 1.0, gsel)
        cur = jnp.where(hit, -jnp.inf, cur)
    cur = jnp.where(gsel > 0.0, c3, -jnp.inf)
    sel = jnp.zeros(shape3, F32)
    picks = []
    for _ in range(TOP_K):
        _, ei = _first_index_of_max(cur, e_idx, float(N_EXPERTS), (1, 0))
        hit = e_idx == ei
        sel = jnp.where(hit, 1.0, sel)
        cur = jnp.where(hit, -jnp.inf, cur)
        picks.append(ei)
    w = sel * s3
    denom = jnp.sum(jnp.sum(w, axis=1, keepdims=True), axis=0, keepdims=True)
    gates3 = w / denom * ROUTED_SCALE
    sel2 = sel.reshape(N_EXPERTS, tm)
    before = jnp.dot(sel2.astype(BF16), tri_ref[...], preferred_element_type=F32)
    rank3 = (before + run_ref[:, 0:1]).reshape(shape3)
    run_ref[...] = run_ref[...] + jnp.sum(sel2, axis=1, keepdims=True)
    cnt_ref[...] = run_ref[...]

    def pick(values, hit):
        return jnp.sum(jnp.sum(jnp.where(hit, values, 0.0), axis=1, keepdims=True), axis=0)

    for k, ei in enumerate(picks):
        hit = e_idx == ei
        e8_ref[k:k + 1, :] = ei.reshape(1, tm).astype(jnp.int32)
        r8_ref[k:k + 1, :] = pick(rank3, hit).astype(jnp.int32)
        w8_ref[k:k + 1, :] = pick(gates3, hit)


def _route(x, rth, rtl, rbias, tm):
    n_tok, d = x.shape
    tri = (jnp.arange(tm)[:, None] < jnp.arange(tm)[None, :]).astype(BF16)
    const2 = lambda i: (0, 0)
    return pl.pallas_call(
        _route_kernel,
        out_shape=(
            jax.ShapeDtypeStruct((2, n_tok, HALF // 2), jnp.int32),
            jax.ShapeDtypeStruct((TOP_K, n_tok), jnp.int32),
            jax.ShapeDtypeStruct((TOP_K, n_tok), jnp.int32),
            jax.ShapeDtypeStruct((TOP_K, n_tok), F32),
            jax.ShapeDtypeStruct((N_EXPERTS, LANES), F32),
        ),
        grid=(n_tok // tm,),
        in_specs=[
            pl.BlockSpec((tm, d), lambda i: (i, 0)),
            pl.BlockSpec(rth.shape, const2),
            pl.BlockSpec(rtl.shape, const2),
            pl.BlockSpec(rbias.shape, const2),
            pl.BlockSpec((tm, tm), const2),
        ],
        out_specs=(
            pl.BlockSpec((2, tm, HALF // 2), lambda i: (0, i, 0)),
            pl.BlockSpec((TOP_K, tm), lambda i: (0, i)),
            pl.BlockSpec((TOP_K, tm), lambda i: (0, i)),
            pl.BlockSpec((TOP_K, tm), lambda i: (0, i)),
            pl.BlockSpec((N_EXPERTS, LANES), const2),
        ),
        scratch_shapes=[pltpu.VMEM((N_EXPERTS, LANES), F32)],
        compiler_params=_cparams("arbitrary"),
        name="moe_router",
    )(x, rth, rtl, rbias, tri)


def _positions_kernel(start_ref, e8_ref, r8_ref, idx_ref, *, n_rows):
    e8 = e8_ref[...]
    pos = r8_ref[...]
    for e in range(N_EXPERTS):
        pos = pos + jnp.where(e8 == e, start_ref[e], 0)
    idx_ref[0] = pos
    idx_ref[1] = pos + n_rows


def _positions(start, e8, r8, n_rows, tn):
    n_tok = e8.shape[1]
    grid_spec = pltpu.PrefetchScalarGridSpec(
        num_scalar_prefetch=1,
        grid=(n_tok // tn,),
        in_specs=[pl.BlockSpec((TOP_K, tn), lambda i, st: (0, i)),
                  pl.BlockSpec((TOP_K, tn), lambda i, st: (0, i))],
        out_specs=pl.BlockSpec((2, TOP_K, tn), lambda i, st: (0, 0, i)),
    )
    return pl.pallas_call(
        functools.partial(_positions_kernel, n_rows=n_rows),
        out_shape=jax.ShapeDtypeStruct((2, TOP_K, n_tok), jnp.int32),
        grid_spec=grid_spec,
        compiler_params=_cparams("parallel"),
        name="moe_positions",
    )(start, e8, r8)


def _experts_kernel(te_ref, used_ref, xs_ref, wg_ref, wu_ref, wd_ref, ys_ref):
    @pl.when(pl.program_id(0) < used_ref[0])
    def _():
        x = jnp.concatenate([_unpack_half(xs_ref[0]), _unpack_half(xs_ref[1])], axis=1).astype(BF16)
        h = (jax.nn.silu(jnp.dot(x, wg_ref[0], preferred_element_type=F32))
             * jnp.dot(x, wu_ref[0], preferred_element_type=F32))
        y = jnp.dot(h.astype(BF16), wd_ref[0], preferred_element_type=F32)
        ys_ref[0] = _pack_half(y[:, :HALF])
        ys_ref[1] = _pack_half(y[:, HALF:])


def _experts(tile_expert, n_used, xs, eg, eu, ed):
    n_rows = xs.shape[1]
    _, d, f = eg.shape
    grid_spec = pltpu.PrefetchScalarGridSpec(
        num_scalar_prefetch=2,
        grid=(n_rows // ROW_TILE,),
        in_specs=[
            pl.BlockSpec((2, ROW_TILE, HALF // 2), lambda i, te, nu: (0, i, 0)),
            pl.BlockSpec((1, d, f), lambda i, te, nu: (te[i], 0, 0)),
            pl.BlockSpec((1, d, f), lambda i, te, nu: (te[i], 0, 0)),
            pl.BlockSpec((1, f, d), lambda i, te, nu: (te[i], 0, 0)),
        ],
        out_specs=pl.BlockSpec((2, ROW_TILE, HALF // 2), lambda i, te, nu: (0, i, 0)),
    )
    return pl.pallas_call(
        _experts_kernel,
        out_shape=jax.ShapeDtypeStruct(xs.shape, jnp.int32),
        grid_spec=grid_spec,
        compiler_params=_cparams("parallel"),
        name="moe_experts",
    )(tile_expert, n_used, xs, eg, eu, ed)


def _combine_kernel(x_ref, yg_ref, w8_ref, sg_ref, su_ref, sd_ref, g_ref, b_ref, o_ref, *, alpha):
    x = x_ref[...]
    tm = x.shape[0]
    xh = x.astype(BF16)
    hs = (jax.nn.silu(jnp.dot(xh, sg_ref[...], preferred_element_type=F32))
          * jnp.dot(xh, su_ref[...], preferred_element_type=F32))
    acc = jnp.dot(hs.astype(BF16), sd_ref[...], preferred_element_type=F32)
    w8 = jnp.concatenate([w8_ref[...], jnp.zeros((LANES - TOP_K, tm), F32)], axis=0).T
    for k in range(TOP_K):
        yk = jnp.concatenate([_unpack_half(yg_ref[0, k]), _unpack_half(yg_ref[1, k])], axis=1)
        acc = acc + w8[:, k:k + 1] * yk
    o_ref[...] = _layer_norm(alpha * x + acc, g_ref[...], b_ref[...])


def _combine(x, yg, w8, sg, su, sd, g, b, tm, alpha):
    n_tok, d = x.shape
    const2 = lambda i: (0, 0)
    return pl.pallas_call(
        functools.partial(_combine_kernel, alpha=alpha),
        out_shape=jax.ShapeDtypeStruct((n_tok, d), F32),
        grid=(n_tok // tm,),
        in_specs=[
            pl.BlockSpec((tm, d), lambda i: (i, 0)),
            pl.BlockSpec((2, TOP_K, tm, HALF // 2), lambda i: (0, 0, i, 0)),
            pl.BlockSpec((TOP_K, tm), lambda i: (0, i)),
            pl.BlockSpec(sg.shape, const2),
            pl.BlockSpec(su.shape, const2),
            pl.BlockSpec(sd.shape, const2),
            pl.BlockSpec((1, d), const2),
            pl.BlockSpec((1, d), const2),
        ],
        out_specs=pl.BlockSpec((tm, d), lambda i: (i, 0)),
        compiler_params=_cparams("parallel"),
        name="moe_combine",
    )(x, yg, w8, sg, su, sd, g, b)


def _sc_mesh():
    return plsc.VectorSubcoreMesh(core_axis_name="c", subcore_axis_name="s")


def _sc_gather_rows(table, idx):
    n = idx.shape[0]
    d = table.shape[1]

    @functools.partial(pl.kernel, out_type=jax.ShapeDtypeStruct((n, d), table.dtype),
                       mesh=_sc_mesh())
    def gather(table_hbm, idx_hbm, out_hbm):
        def body(idx_vmem, out_vmem):
            pltpu.sync_copy(table_hbm.at[idx_vmem.at[0]], out_vmem)

        pltpu.emit_pipeline(
            body, grid=(n // SC_WINDOW,),
            in_specs=[pl.BlockSpec((1, SC_WINDOW), index_map=lambda i: (0, i))],
            out_specs=[pl.BlockSpec((SC_WINDOW, d), index_map=lambda i: (i, 0))],
            core_axis_name=("c", "s"), dimension_semantics=(pltpu.PARALLEL,), trace_scopes=False,
        )(idx_hbm, out_hbm)

    return gather(table, idx.reshape(1, n))


def _sc_scatter_rows(rows, idx, n_out, src_block):
    n = idx.shape[0]
    d = rows.shape[1]

    @functools.partial(pl.kernel, out_type=jax.ShapeDtypeStruct((n_out, d), rows.dtype),
                       mesh=_sc_mesh())
    def scatter(rows_hbm, idx_hbm, out_hbm):
        def body(rows_vmem, idx_vmem):
            pltpu.sync_copy(rows_vmem, out_hbm.at[idx_vmem.at[0]])

        pltpu.emit_pipeline(
            body, grid=(n // SC_WINDOW,),
            in_specs=[pl.BlockSpec((SC_WINDOW, d), index_map=lambda i: (src_block(i), 0)),
                      pl.BlockSpec((1, SC_WINDOW), index_map=lambda i: (0, i))],
            out_specs=[],
            core_axis_name=("c", "s"), dimension_semantics=(pltpu.PARALLEL,), trace_scopes=False,
        )(rows_hbm, idx_hbm)

    return scatter(rows, idx.reshape(1, n))


def _moe_dispatch(x, rth, rtl, rbias, tm):
    n_tok, d = x.shape
    n_rows = TOP_K * n_tok + N_EXPERTS * ROW_TILE
    n_tiles = n_rows // ROW_TILE
    xp, e8, r8, w8, cnt = _route(x, rth, rtl, rbias, tm)
    counts = cnt[:, 0].astype(jnp.int32)
    padded = (counts + ROW_TILE - 1) // ROW_TILE * ROW_TILE
    ends = jnp.cumsum(padded)
    start = ends - padded
    n_used = (ends[-1:] // ROW_TILE).astype(jnp.int32)
    tile_ids = jnp.arange(n_tiles, dtype=jnp.int32)
    tile_expert = jnp.minimum(
        jnp.sum((tile_ids[:, None] >= (ends // ROW_TILE)[None, :]).astype(jnp.int32), axis=1),
        N_EXPERTS - 1)
    idx = _positions(start, e8, r8, n_rows, _pick_tile(n_tok, 2048)).reshape(-1)
    blocks_per_half = n_tok // SC_WINDOW
    xs = _sc_scatter_rows(
        xp.reshape(2 * n_tok, HALF // 2), idx, 2 * n_rows,
        lambda i: (i // (TOP_K * blocks_per_half)) * blocks_per_half + i % blocks_per_half)
    return xs.reshape(2, n_rows, HALF // 2), idx, tile_expert, n_used, w8


def _moe_experts(xs, idx, tile_expert, n_used, eg, eu, ed):
    n_rows = xs.shape[1]
    ys = _experts(tile_expert, n_used, xs, eg, eu, ed)
    yg = _sc_gather_rows(ys.reshape(2 * n_rows, HALF // 2), idx)
    return yg.reshape(2, TOP_K, idx.shape[0] // (2 * TOP_K), HALF // 2)


def _rope_tables(seq):
    t = jnp.arange(seq)
    row = (t // GRID_W).astype(F32)
    colp = (t % GRID_W).astype(F32)
    inv = ROPE_THETA ** (-jnp.arange(0, ROPE_AXIS_DIM, 2, dtype=F32) / ROPE_AXIS_DIM)
    ang = jnp.concatenate([row[:, None] * inv[None], colp[:, None] * inv[None]], axis=-1)
    ang = jnp.repeat(ang, 2, axis=-1)
    sign = jnp.where(jnp.arange(HEAD_DIM) % 2 == 0, -1.0, 1.0).astype(F32)
    cos = jnp.tile(jnp.cos(ang), (1, 2))
    sin = jnp.tile(jnp.sin(ang) * sign[None], (1, 2))
    return cos, sin


def _gqa_head_order():
    order = []
    for kvp in range(N_KV_HEADS // 2):
        base = 2 * GQA_GROUP * kvp
        for g in range(GQA_GROUP):
            order += [base + g, base + GQA_GROUP + g]
    return np.array(order)


def _pick_tile(n, target):
    t = min(n, target)
    while n % t:
        t //= 2
    return t


def kernel(x_prompt, x_sample, mem_prompt, mem_sample, a_w_in, a_lambda, a_subln, b_w_in, b_rpb,
           c_w_in, c_q_norm, c_k_norm, w_mem_kv, w_out, ln1_g, ln1_b, router, router_bias,
           e_gate, e_up, e_down, s_gate, s_up, s_down, ln2_g, ln2_b):
    depth = w_out.shape[0]
    alpha = (2.0 * depth) ** 0.25
    seq = x_prompt.shape[1]
    d = x_prompt.shape[2]
    assert x_sample.shape[1] == seq and seq % (NA_ROWS * GRID_W) == 0
    n_mem = mem_prompt.shape[1]
    batches = [x_prompt.shape[0], x_sample.shape[0]]
    xs_ = [x_prompt.reshape(-1, d), x_sample.reshape(-1, d)]
    mems = [mem_prompt.reshape(-1, d), mem_sample.reshape(-1, d)]

    tm = _pick_tile(seq, 512)
    tk = _pick_tile(seq, 512)
    head_order = _gqa_head_order()
    col_order = (head_order[:, None] * HEAD_DIM + np.arange(HEAD_DIM)[None]).reshape(-1)
    cos, sin = _rope_tables(seq)
    slopes = _alibi_slopes(N_MIX_HEADS).reshape(N_MIX_PAIRS, 2)
    slope_tab = np.zeros((N_MIX_PAIRS, 8, LANES), np.float32)
    slope_tab[:, 0:2, :] = slopes[:, :, None]
    slope_tab = jnp.asarray(slope_tab) * LOG2E
    ones_row = functools.partial(jnp.ones, dtype=F32)
    mem_q_scale = jnp.full((MEM_WIDTH,), HEAD_DIM ** -0.5, F32)

    for i in range(depth):
        mixer, occ = i % N_MIXERS, i // N_MIXERS
        w_o = w_out[i]
        rope = None
        if mixer == 0:
            w_in = a_w_in[occ]
            q_scale = DIFF_QK_DIM ** -0.5 * LOG2E
            gain = jnp.tile(a_subln[occ].astype(F32), 2)[None]
            lam = a_lambda[occ].astype(F32)
        elif mixer == 1:
            w_in = b_w_in[occ]
            q_scale = HEAD_DIM ** -0.5
            na_bias = _na_bias_table(b_rpb[occ])
        else:
            w_in = c_w_in[occ]
            w_in = jnp.concatenate([w_in[:, :MIX_WIDTH][:, col_order], w_in[:, MIX_WIDTH:]], axis=1)
            w_o = jnp.concatenate([w_o[:MIX_WIDTH][col_order], w_o[MIX_WIDTH:]], axis=0)
            q_scale = HEAD_DIM ** -0.5 * LOG2E
            qg = jnp.tile(c_q_norm[occ].astype(F32), 2)[None, None]
            kg = jnp.tile(c_k_norm[occ].astype(F32), 2)[None, None]
            gains = jnp.concatenate([jnp.tile(qg, (N_MIX_PAIRS, 1, 1)),
                                     jnp.tile(kg, (N_KV_HEADS // 2, 1, 1))], axis=0)
            rope = (gains, cos, sin)
        colscale = jnp.concatenate([
            jnp.full((MIX_WIDTH,), q_scale, F32),
            ones_row((w_in.shape[1] - MIX_WIDTH - MEM_WIDTH,)), mem_q_scale])[None]
        w_in = w_in.astype(BF16)
        w_o = w_o.astype(BF16)
        w_kv = w_mem_kv[i].astype(BF16)
        rt = router[i].T.astype(F32)
        rth = rt.astype(BF16)
        rtl = (rt - rth.astype(F32)).astype(BF16)
        rbias = router_bias[i].astype(F32)[:, None]
        eg, eu, ed = e_gate[i].astype(BF16), e_up[i].astype(BF16), e_down[i].astype(BF16)
        sg, su, sd = s_gate[i].astype(BF16), s_up[i].astype(BF16), s_down[i].astype(BF16)

        dispatched = []
        for s in range(len(xs_)):
            x, batch = xs_[s], batches[s]
            heads = _in_projection(x, w_in, colscale, tm, seq, rope=rope)
            if mixer == 0:
                mix = _diff_attention(heads, slope_tab, lam, gain, batch, seq,
                                      _pick_tile(seq, 256), _pick_tile(seq, 1024),
                                      _diff_lambda_init(i))
            elif mixer == 1:
                mix = _na_attention(heads, na_bias, batch, seq)
            else:
                mix = _gqa_attention(heads, batch, seq, _pick_tile(seq, 512), tk)
            kv = _matmul(mems[s], w_kv, _pick_tile(batch * n_mem, 512))
            kv = kv.reshape(batch, n_mem, 2 * MEM_WIDTH)
            x = _out_projection(mix, heads, kv, x, w_o, ln1_g[i][None], ln1_b[i][None],
                                batch, seq, tm, alpha)
            xs_[s] = x
            dispatched.append(_moe_dispatch(x, rth, rtl, rbias, tm))
        gathered = [_moe_experts(*disp[:4], eg, eu, ed) for disp in dispatched]
        for s in range(len(xs_)):
            xs_[s] = _combine(xs_[s], gathered[s], dispatched[s][4], sg, su, sd,
                              ln2_g[i][None], ln2_b[i][None], tm, alpha)

    return tuple(x.reshape(b, seq, d) for x, b in zip(xs_, batches))
```

```python
import functools
import math

import numpy as np
import jax
import jax.numpy as jnp
from jax import lax
from jax.experimental import pallas as pl
from jax.experimental.pallas import tpu as pltpu
from jax.experimental.pallas import tpu_sc as plsc

F32 = jnp.float32
BF16 = jnp.bfloat16

D_MODEL = 1024
HEAD_DIM = 64
LANES = 128
N_MIX_HEADS = 12
N_MIX_PAIRS = N_MIX_HEADS // 2
N_MEM_HEADS = 4
N_MEM_PAIRS = N_MEM_HEADS // 2
MIX_WIDTH = N_MIX_HEADS * HEAD_DIM
MEM_WIDTH = N_MEM_HEADS * HEAD_DIM
N_MIXERS = 3
DIFF_QK_DIM = HEAD_DIM // 2
GRID_W = 64
NA_ROWS = 8
NA_COLS = 16
N_KV_HEADS = 4
GQA_GROUP = N_MIX_HEADS // N_KV_HEADS
ROPE_THETA = 10000.0
ROPE_AXIS_DIM = HEAD_DIM // 2
N_EXPERTS = 64
TOP_K = 8
N_GROUPS = 8
TOPK_GROUPS = 4
PER_GROUP = N_EXPERTS // N_GROUPS
D_EXPERT = 256
ROUTED_SCALE = 2.5
LN_EPS = 1e-5
RMS_EPS = 1e-6
NEG_BIG = -1e30
LOG2E = math.log2(math.e)

VMEM_LIMIT = 48 * 1024 * 1024

NT_DIMS = (((1,), (1,)), ((), ()))
TN_DIMS = (((0,), (0,)), ((), ()))


def _cparams(*sem):
    return pltpu.CompilerParams(dimension_semantics=sem, vmem_limit_bytes=VMEM_LIMIT)


def _alibi_slopes(n):
    def pow2_slopes(m):
        start = 2.0 ** (-8.0 / m)
        return [start ** (i + 1) for i in range(m)]
    if math.log2(n).is_integer():
        s = pow2_slopes(n)
    else:
        c = 2 ** math.floor(math.log2(n))
        s = pow2_slopes(c) + pow2_slopes(2 * c)[0::2][: n - c]
    return np.array(s, np.float32)


def _diff_lambda_init(layer_idx):
    return 0.8 - 0.6 * math.exp(-0.3 * layer_idx)


def _lane_iota(shape):
    return lax.broadcasted_iota(jnp.int32, shape, len(shape) - 1)


def _layer_norm(y, g, b):
    mu = jnp.mean(y, axis=-1, keepdims=True)
    yc = y - mu
    var = jnp.mean(yc * yc, axis=-1, keepdims=True)
    return yc * lax.rsqrt(var + LN_EPS) * g + b


def _head_rms_scale(x):
    lo = _lane_iota(x.shape) < HEAD_DIM
    xx = x * x
    ss_lo = jnp.sum(jnp.where(lo, xx, 0.0), axis=-1, keepdims=True)
    ss_hi = jnp.sum(jnp.where(lo, 0.0, xx), axis=-1, keepdims=True)
    inv = 1.0 / HEAD_DIM
    return jnp.where(lo, lax.rsqrt(ss_lo * inv + RMS_EPS), lax.rsqrt(ss_hi * inv + RMS_EPS))


def _inproj_kernel(x_ref, w_ref, cs_ref, o_ref):
    r = jnp.dot(x_ref[...].astype(BF16), w_ref[...], preferred_element_type=F32)
    r = r * cs_ref[...]
    for p in range(o_ref.shape[0]):
        o_ref[p] = r[:, p * LANES:(p + 1) * LANES].astype(BF16)


def _inproj_rope_kernel(x_ref, w_ref, cs_ref, g_ref, cos_ref, sin_ref, o_ref, *, n_rope_pairs):
    r = jnp.dot(x_ref[...].astype(BF16), w_ref[...], preferred_element_type=F32)
    cs = cs_ref[...]
    cos = cos_ref[...]
    sin = sin_ref[...]
    even = (_lane_iota(cos.shape) % 2) == 0
    for p in range(o_ref.shape[0]):
        xp = r[:, p * LANES:(p + 1) * LANES]
        if p < n_rope_pairs:
            xn = xp * _head_rms_scale(xp) * g_ref[p]
            partner = jnp.where(even, pltpu.roll(xn, LANES - 1, 1), pltpu.roll(xn, 1, 1))
            xp = xn * cos + partner * sin
        o_ref[p] = (xp * cs[:, p * LANES:(p + 1) * LANES]).astype(BF16)


def _in_projection(x, w, colscale, tm, seq, rope=None):
    n_tok, d = x.shape
    width = w.shape[1]
    n_pairs = width // LANES
    in_specs = [
        pl.BlockSpec((tm, d), lambda i: (i, 0)),
        pl.BlockSpec((d, width), lambda i: (0, 0)),
        pl.BlockSpec((1, width), lambda i: (0, 0)),
    ]
    args = [x, w, colscale]
    if rope is None:
        body = _inproj_kernel
    else:
        gains, cos, sin = rope
        tiles_per_seq = seq // tm
        body = functools.partial(_inproj_rope_kernel, n_rope_pairs=gains.shape[0])
        in_specs += [
            pl.BlockSpec(gains.shape, lambda i: (0, 0, 0)),
            pl.BlockSpec((tm, LANES), lambda i: (i % tiles_per_seq, 0)),
            pl.BlockSpec((tm, LANES), lambda i: (i % tiles_per_seq, 0)),
        ]
        args += [gains, cos, sin]
    return pl.pallas_call(
        body,
        out_shape=jax.ShapeDtypeStruct((n_pairs, n_tok, LANES), BF16),
        grid=(n_tok // tm,),
        in_specs=in_specs,
        out_specs=pl.BlockSpec((n_pairs, tm, LANES), lambda i: (0, i, 0)),
        compiler_params=_cparams("parallel"),
        name="in_projection",
    )(*args)


def _matmul_kernel(x_ref, w_ref, o_ref):
    o_ref[...] = jnp.dot(x_ref[...].astype(BF16), w_ref[...],
                         preferred_element_type=F32).astype(o_ref.dtype)


def _matmul(x, w, tm):
    m, k = x.shape
    n = w.shape[1]
    return pl.pallas_call(
        _matmul_kernel,
        out_shape=jax.ShapeDtypeStruct((m, n), BF16),
        grid=(m // tm,),
        in_specs=[pl.BlockSpec((tm, k), lambda i: (i, 0)), pl.BlockSpec((k, n), lambda i: (0, 0))],
        out_specs=pl.BlockSpec((tm, n), lambda i: (i, 0)),
        compiler_params=_cparams("parallel"),
        name="mem_kv_projection",
    )(x, w)


FLASH_UNROLL = 2


def _stack_masked_q(q_ref, qs_ref, tq):
    n_qpairs = q_ref.shape[0]
    per_pair = qs_ref.shape[0] // tq // n_qpairs
    lanes_per_map = LANES // per_pair
    for t in range(n_qpairs):
        q = q_ref[t].astype(F32)
        part = _lane_iota(q.shape) // lanes_per_map
        for a in range(per_pair):
            r0 = (t * per_pair + a) * tq
            qs_ref[r0:r0 + tq] = jnp.where(part == a, q, 0.0).astype(BF16)


def _softmax_accumulate(s, v, m, l, acc_ref):
    m_new = jnp.maximum(m, jnp.max(s, axis=0, keepdims=True))
    alpha = jnp.exp2(m - m_new)
    p = jnp.exp2(s - m_new)
    l_new = alpha * l + jnp.sum(p, axis=0, keepdims=True)
    pv = lax.dot_general(v, p.astype(BF16), TN_DIMS, preferred_element_type=F32)
    acc_ref[...] = alpha * acc_ref[...] + pv
    return m_new, l_new


def _diff_attn_kernel(q_ref, k_ref, v_ref, slope_ref, lam_ref, g_ref, o_ref, qs_ref, acc_ref,
                      *, tq, tk, seq, lambda_init):
    i = pl.program_id(2)
    n_maps = 4
    w = n_maps * tq
    _stack_masked_q(q_ref, qs_ref, tq)
    acc_ref[...] = jnp.zeros(acc_ref.shape, F32)
    slopes = slope_ref[0]
    rel0 = (lax.broadcasted_iota(jnp.int32, (tk, tq), 0)
            - lax.broadcasted_iota(jnp.int32, (tk, tq), 1) - i * tq)

    def chunk(j, carry):
        m, l = carry
        off = pl.multiple_of(j * tk, tk)
        k = k_ref[0, pl.ds(off, tk), :]
        v = v_ref[0, pl.ds(off, tk), :]
        s = lax.dot_general(k, qs_ref[...], NT_DIMS, preferred_element_type=F32)
        dist = jnp.abs(rel0 + j * tk).astype(F32)
        s = jnp.concatenate(
            [s[:, a * tq:(a + 1) * tq] - slopes[a // 2:a // 2 + 1, 0:1] * dist
             for a in range(n_maps)], axis=1)
        return _softmax_accumulate(s, v, m, l, acc_ref)

    carry = (jnp.full((1, w), -jnp.inf, F32), jnp.zeros((1, w), F32))
    m, l = lax.fori_loop(0, seq // tk, chunk, carry, unroll=FLASH_UNROLL)

    o = acc_ref[...] / l
    lam = lam_ref[...]
    lam_full = (jnp.exp(jnp.sum(lam[0:1] * lam[1:2], axis=-1, keepdims=True))
                - jnp.exp(jnp.sum(lam[2:3] * lam[3:4], axis=-1, keepdims=True)) + lambda_init)
    h0 = o[:, 0:tq] - lam_full * o[:, tq:2 * tq]
    h1 = o[:, 2 * tq:3 * tq] - lam_full * o[:, 3 * tq:4 * tq]
    row = lax.broadcasted_iota(jnp.int32, (LANES, tq), 0)
    o = jnp.where(row < HEAD_DIM, h0, h1).T
    o = o * _head_rms_scale(o) * g_ref[...] * (1.0 - lambda_init)
    o_ref[0] = o.astype(BF16)


def _diff_attention(heads, slopes, lam, gain, batch, seq, tq, tk, lambda_init):
    n_tok = heads.shape[1]
    nq = seq // tq
    body = functools.partial(_diff_attn_kernel, tq=tq, tk=tk, seq=seq, lambda_init=lambda_init)
    return pl.pallas_call(
        body,
        out_shape=jax.ShapeDtypeStruct((N_MIX_PAIRS, n_tok, LANES), BF16),
        grid=(batch, N_MIX_PAIRS, nq),
        in_specs=[
            pl.BlockSpec((1, tq, LANES), lambda b, p, i: (p, b * nq + i, 0)),
            pl.BlockSpec((1, seq, LANES), lambda b, p, i: (N_MIX_PAIRS + p, b, 0)),
            pl.BlockSpec((1, seq, LANES), lambda b, p, i: (2 * N_MIX_PAIRS + p, b, 0)),
            pl.BlockSpec((1, 8, LANES), lambda b, p, i: (p, 0, 0)),
            pl.BlockSpec(lam.shape, lambda b, p, i: (0, 0)),
            pl.BlockSpec((1, LANES), lambda b, p, i: (0, 0)),
        ],
        out_specs=pl.BlockSpec((1, tq, LANES), lambda b, p, i: (p, b * nq + i, 0)),
        scratch_shapes=[
            pltpu.VMEM((4 * tq, LANES), BF16),
            pltpu.VMEM((LANES, 4 * tq), F32),
        ],
        compiler_params=_cparams("parallel", "parallel", "parallel"),
        name="diff_attention",
    )(heads, heads, heads, slopes, lam, gain)


def _gqa_attn_kernel(q_ref, k_ref, v_ref, o_ref, qs_ref, acc_ref, *, tq, tk, seq):
    n_qpairs = q_ref.shape[0]
    w = 2 * n_qpairs * tq
    _stack_masked_q(q_ref, qs_ref, tq)
    acc_ref[...] = jnp.zeros(acc_ref.shape, F32)

    def chunk(j, carry):
        m, l = carry
        off = pl.multiple_of(j * tk, tk)
        k = k_ref[0, pl.ds(off, tk), :]
        v = v_ref[0, pl.ds(off, tk), :]
        s = lax.dot_general(k, qs_ref[...], NT_DIMS, preferred_element_type=F32)
        return _softmax_accumulate(s, v, m, l, acc_ref)

    carry = (jnp.full((1, w), -jnp.inf, F32), jnp.zeros((1, w), F32))
    m, l = lax.fori_loop(0, seq // tk, chunk, carry, unroll=FLASH_UNROLL)
    o = acc_ref[...] / l
    row = lax.broadcasted_iota(jnp.int32, (LANES, tq), 0)
    for t in range(n_qpairs):
        lo = o[:, (2 * t) * tq:(2 * t + 1) * tq]
        hi = o[:, (2 * t + 1) * tq:(2 * t + 2) * tq]
        o_ref[t] = jnp.where(row < HEAD_DIM, lo, hi).T.astype(BF16)


def _gqa_attention(heads, batch, seq, tq, tk):
    n_tok = heads.shape[1]
    nq = seq // tq
    n_kv_pairs = N_KV_HEADS // 2
    qp = N_MIX_PAIRS // n_kv_pairs
    body = functools.partial(_gqa_attn_kernel, tq=tq, tk=tk, seq=seq)
    return pl.pallas_call(
        body,
        out_shape=jax.ShapeDtypeStruct((N_MIX_PAIRS, n_tok, LANES), BF16),
        grid=(batch, n_kv_pairs, nq),
        in_specs=[
            pl.BlockSpec((qp, tq, LANES), lambda b, p, i: (p, b * nq + i, 0)),
            pl.BlockSpec((1, seq, LANES), lambda b, p, i: (N_MIX_PAIRS + p, b, 0)),
            pl.BlockSpec((1, seq, LANES), lambda b, p, i: (N_MIX_PAIRS + n_kv_pairs + p, b, 0)),
        ],
        out_specs=pl.BlockSpec((qp, tq, LANES), lambda b, p, i: (p, b * nq + i, 0)),
        scratch_shapes=[
            pltpu.VMEM((2 * qp * tq, LANES), BF16),
            pltpu.VMEM((LANES, 2 * qp * tq), F32),
        ],
        compiler_params=_cparams("parallel", "parallel", "parallel"),
        name="gqa_attention",
    )(heads, heads, heads)


def _na_attn_kernel(q_ref, kp_ref, kc_ref, kn_ref, vp_ref, vc_ref, vn_ref, bias_ref, o_ref,
                    kbuf_ref, vbuf_ref, *, n_grid_rows):
    rb = pl.program_id(2)
    blk = NA_ROWS * GRID_W
    win = NA_ROWS * GRID_W
    kbuf_ref[0:blk] = kp_ref[0]
    kbuf_ref[blk:2 * blk] = kc_ref[0]
    kbuf_ref[2 * blk:3 * blk] = kn_ref[0]
    vbuf_ref[0:blk] = vp_ref[0]
    vbuf_ref[blk:2 * blk] = vc_ref[0]
    vbuf_ref[2 * blk:3 * blk] = vn_ref[0]
    lo = _lane_iota((GRID_W, LANES)) < HEAD_DIM
    offs, scores = [], []
    for u in range(NA_ROWS):
        r = rb * NA_ROWS + u
        rs = jnp.clip(r - NA_ROWS // 2, 0, n_grid_rows - NA_ROWS)
        off = pl.multiple_of((rs - (rb - 1) * NA_ROWS) * GRID_W, GRID_W)
        ro0 = rs - r + NA_ROWS - 1
        q = q_ref[0, u * GRID_W:(u + 1) * GRID_W].astype(F32)
        qs = jnp.concatenate([jnp.where(lo, q, 0.0), jnp.where(lo, 0.0, q)], axis=0).astype(BF16)
        s = lax.dot_general(qs, kbuf_ref[pl.ds(off, win)], NT_DIMS,
                            preferred_element_type=F32)
        bias = jnp.concatenate(
            [jnp.concatenate([bias_ref[h, ro0 + 2 * t] for t in range(NA_ROWS // 2)], axis=1)
             for h in range(2)], axis=0)
        offs.append(off)
        scores.append(s + bias)
    probs = []
    for s in scores:
        p = jnp.exp(s - jnp.max(s, axis=-1, keepdims=True))
        probs.append((p.astype(BF16), jnp.sum(p, axis=-1, keepdims=True)))
    for u, (p, l) in enumerate(probs):
        o = jnp.dot(p, vbuf_ref[pl.ds(offs[u], win)], preferred_element_type=F32) / l
        o_ref[0, u * GRID_W:(u + 1) * GRID_W] = jnp.where(lo, o[0:GRID_W], o[GRID_W:]).astype(BF16)


def _na_attention(heads, bias, batch, seq):
    n_tok = heads.shape[1]
    n_grid_rows = seq // GRID_W
    nrb = n_grid_rows // NA_ROWS
    blk = NA_ROWS * GRID_W

    def kv_spec(first_pair, shift):
        return pl.BlockSpec(
            (1, blk, LANES),
            lambda b, p, rb: (first_pair + p, b * nrb + jnp.clip(rb + shift, 0, nrb - 1), 0))

    body = functools.partial(_na_attn_kernel, n_grid_rows=n_grid_rows)
    return pl.pallas_call(
        body,
        out_shape=jax.ShapeDtypeStruct((N_MIX_PAIRS, n_tok, LANES), BF16),
        grid=(batch, N_MIX_PAIRS, nrb),
        in_specs=[
            pl.BlockSpec((1, blk, LANES), lambda b, p, rb: (p, b * nrb + rb, 0)),
            kv_spec(N_MIX_PAIRS, -1), kv_spec(N_MIX_PAIRS, 0), kv_spec(N_MIX_PAIRS, 1),
            kv_spec(2 * N_MIX_PAIRS, -1), kv_spec(2 * N_MIX_PAIRS, 0), kv_spec(2 * N_MIX_PAIRS, 1),
            pl.BlockSpec((2,) + bias.shape[1:], lambda b, p, rb: (p, 0, 0, 0)),
        ],
        out_specs=pl.BlockSpec((1, blk, LANES), lambda b, p, rb: (p, b * nrb + rb, 0)),
        scratch_shapes=[pltpu.VMEM((3 * blk, LANES), BF16), pltpu.VMEM((3 * blk, LANES), BF16)],
        compiler_params=_cparams("parallel", "parallel", "arbitrary"),
        name="neighbourhood_attention",
    )(heads, heads, heads, heads, heads, heads, heads, bias)


def _na_bias_table(rpb):
    col = jnp.arange(GRID_W)
    col_start = jnp.clip(col - NA_COLS // 2, 0, GRID_W - NA_COLS)
    col_mask = (col[None, :] >= col_start[:, None]) & (col[None, :] < col_start[:, None] + NA_COLS)
    col_bias_idx = jnp.clip(col[None, :] - col[:, None] + NA_COLS - 1, 0, 2 * NA_COLS - 2)
    tiles = rpb.astype(F32)[:, :, col_bias_idx]
    tiles = jnp.where(col_mask[None, None], tiles, NEG_BIG)
    return jnp.concatenate([tiles[:, :-1], tiles[:, 1:]], axis=-1)


def _outproj_kernel(mix_ref, qm_ref, kv_ref, x_ref, w_ref, g_ref, b_ref, o_ref, *, alpha):
    tm = x_ref.shape[0]
    lo = _lane_iota((tm, LANES)) < HEAD_DIM
    parts = [mix_ref[p] for p in range(mix_ref.shape[0])]
    kv = kv_ref[0]
    scores = []
    for t in range(N_MEM_PAIRS):
        q = qm_ref[t].astype(F32)
        qs = jnp.concatenate([jnp.where(lo, q, 0.0), jnp.where(lo, 0.0, q)], axis=0).astype(BF16)
        kt = kv[:, t * LANES:(t + 1) * LANES]
        scores.append(lax.dot_general(qs, kt, NT_DIMS, preferred_element_type=F32))
    probs = []
    for s in scores:
        p = jnp.exp(s - jnp.max(s, axis=-1, keepdims=True))
        probs.append((p.astype(BF16), jnp.sum(p, axis=-1, keepdims=True)))
    for t, (p, l) in enumerate(probs):
        vt = kv[:, MEM_WIDTH + t * LANES:MEM_WIDTH + (t + 1) * LANES]
        o = jnp.dot(p, vt, preferred_element_type=F32) / l
        parts.append(jnp.where(lo, o[0:tm], o[tm:]).astype(BF16))
    attn = jnp.concatenate(parts, axis=1)
    h = jnp.dot(attn, w_ref[...], preferred_element_type=F32)
    o_ref[...] = _layer_norm(alpha * x_ref[...] + h, g_ref[...], b_ref[...])


def _out_projection(mix, heads, kv, x, w_out, g, b, batch, seq, tm, alpha):
    n_tok, d = x.shape
    nt = seq // tm
    n_pairs = heads.shape[0]
    n_mem = kv.shape[1]
    body = functools.partial(_outproj_kernel, alpha=alpha)
    return pl.pallas_call(
        body,
        out_shape=jax.ShapeDtypeStruct((n_tok, d), F32),
        grid=(batch, nt),
        in_specs=[
            pl.BlockSpec((N_MIX_PAIRS, tm, LANES), lambda bi, i: (0, bi * nt + i, 0)),
            pl.BlockSpec((N_MEM_PAIRS, tm, LANES),
                         lambda bi, i: (n_pairs // N_MEM_PAIRS - 1, bi * nt + i, 0)),
            pl.BlockSpec((1, n_mem, 2 * MEM_WIDTH), lambda bi, i: (bi, 0, 0)),
            pl.BlockSpec((tm, d), lambda bi, i: (bi * nt + i, 0)),
            pl.BlockSpec(w_out.shape, lambda bi, i: (0, 0)),
            pl.BlockSpec((1, d), lambda bi, i: (0, 0)),
            pl.BlockSpec((1, d), lambda bi, i: (0, 0)),
        ],
        out_specs=pl.BlockSpec((tm, d), lambda bi, i: (bi * nt + i, 0)),
        compiler_params=_cparams("parallel", "parallel"),
        name="mem_attention_out_projection",
    )(mix, heads, kv, x, w_out, g, b)


ROW_TILE = 1024
SC_WINDOW = 128
HALF = D_MODEL // 2


def _pack_half(v):
    q = HALF // 2
    return pltpu.pack_elementwise([v[:, :q], v[:, q:]], packed_dtype=BF16)


def _unpack_half(w):
    return jnp.concatenate(
        [pltpu.unpack_elementwise(w, index=h, packed_dtype=BF16, unpacked_dtype=F32)
         for h in range(2)], axis=1)

def _first_index_of_max(vals, index, sentinel, axes):
    mx = vals
    for ax in axes:
        mx = jnp.max(mx, axis=ax, keepdims=True)
    idx = jnp.where(vals == mx, index, sentinel)
    for ax in axes:
        idx = jnp.min(idx, axis=ax, keepdims=True)
    return mx, idx


def _route_kernel(x_ref, rth_ref, rtl_ref, rbias_ref, tri_ref,
                  xp_ref, e8_ref, r8_ref, w8_ref, cnt_ref, run_ref):
    @pl.when(pl.program_id(0) == 0)
    def _():
        run_ref[...] = jnp.zeros(run_ref.shape, F32)

    x = x_ref[...]
    tm = x.shape[0]
    xp_ref[0] = _pack_half(x[:, :HALF])
    xp_ref[1] = _pack_half(x[:, HALF:])
    xh = x.astype(BF16)
    xl = (x - xh.astype(F32)).astype(BF16)
    rth = rth_ref[...]
    logits = (lax.dot_general(rth, xh, NT_DIMS, preferred_element_type=F32)
              + lax.dot_general(rth, xl, NT_DIMS, preferred_element_type=F32)
              + lax.dot_general(rtl_ref[...], xh, NT_DIMS, preferred_element_type=F32))
    scores = jax.nn.sigmoid(logits)
    choice = scores + rbias_ref[...]
    shape3 = (N_GROUPS, PER_GROUP, tm)
    c3 = choice.reshape(shape3)
    s3 = scores.reshape(shape3)
    e_in_g = lax.broadcasted_iota(jnp.int32, shape3, 1).astype(F32)
    g_idx = lax.broadcasted_iota(jnp.int32, (N_GROUPS, 1, tm), 0).astype(F32)
    e_idx = lax.broadcasted_iota(jnp.int32, shape3, 0).astype(F32) * PER_GROUP + e_in_g
    m1, i1 = _first_index_of_max(c3, e_in_g, float(PER_GROUP), (1,))
    m2 = jnp.max(jnp.where(e_in_g == i1, -jnp.inf, c3), axis=1, keepdims=True)
    cur = m1 + m2
    gsel = jnp.zeros(cur.shape, F32)
    for _ in range(TOPK_GROUPS):
        _, gi = _first_index_of_max(cur, g_idx, float(N_GROUPS), (0,))
        hit = g_idx == gi
        gsel = jnp.where(hit, 1.0, gsel)
        cur = jnp.where(hit, -jnp.inf, cur)
    cur = jnp.where(gsel > 0.0, c3, -jnp.inf)
    sel = jnp.zeros(shape3, F32)
    picks = []
    for _ in range(TOP_K):
        _, ei = _first_index_of_max(cur, e_idx, float(N_EXPERTS), (0, 1))
        hit = e_idx == ei
        sel = jnp.where(hit, 1.0, sel)
        cur = jnp.where(hit, -jnp.inf, cur)
        picks.append(ei)
    w = sel * s3
    denom = jnp.sum(jnp.sum(w, axis=0, keepdims=True), axis=1, keepdims=True)
    gates3 = w / denom * ROUTED_SCALE
    sel2 = sel.reshape(N_EXPERTS, tm)
    before = jnp.dot(sel2.astype(BF16), tri_ref[...], preferred_element_type=F32)
    rank3 = (before + run_ref[:, 0:1]).reshape(shape3)
    run_ref[...] = run_ref[...] + jnp.sum(sel2, axis=1, keepdims=True)
    cnt_ref[...] = run_ref[...]

    def pick(values, hit):
        return jnp.sum(jnp.sum(jnp.where(hit, values, 0.0), axis=0, keepdims=True), axis=1)

    for k, ei in enumerate(picks):
        hit = e_idx == ei
        e8_ref[k:k + 1, :] = ei.reshape(1, tm).astype(jnp.int32)
        r8_ref[k:k + 1, :] = pick(rank3, hit).astype(jnp.int32)
        w8_ref[k:k + 1, :] = pick(gates3, hit)


def _route(x, rth, rtl, rbias, tm):
    n_tok, d = x.shape
    tri = (jnp.arange(tm)[:, None] < jnp.arange(tm)[None, :]).astype(BF16)
    const2 = lambda i: (0, 0)
    return pl.pallas_call(
        _route_kernel,
        out_shape=(
            jax.ShapeDtypeStruct((2, n_tok, HALF // 2), jnp.int32),
            jax.ShapeDtypeStruct((TOP_K, n_tok), jnp.int32),
            jax.ShapeDtypeStruct((TOP_K, n_tok), jnp.int32),
            jax.ShapeDtypeStruct((TOP_K, n_tok), F32),
            jax.ShapeDtypeStruct((N_EXPERTS, LANES), F32),
        ),
        grid=(n_tok // tm,),
        in_specs=[
            pl.BlockSpec((tm, d), lambda i: (i, 0)),
            pl.BlockSpec(rth.shape, const2),
            pl.BlockSpec(rtl.shape, const2),
            pl.BlockSpec(rbias.shape, const2),
            pl.BlockSpec((tm, tm), const2),
        ],
        out_specs=(
            pl.BlockSpec((2, tm, HALF // 2), lambda i: (0, i, 0)),
            pl.BlockSpec((TOP_K, tm), lambda i: (0, i)),
            pl.BlockSpec((TOP_K, tm), lambda i: (0, i)),
            pl.BlockSpec((TOP_K, tm), lambda i: (0, i)),
            pl.BlockSpec((N_EXPERTS, LANES), const2),
        ),
        scratch_shapes=[pltpu.VMEM((N_EXPERTS, LANES), F32)],
        compiler_params=_cparams("arbitrary"),
        name="moe_router",
    )(x, rth, rtl, rbias, tri)


def _positions_kernel(start_ref, e8_ref, r8_ref, idx_ref, *, n_rows):
    e8 = e8_ref[...]
    pos = r8_ref[...]
    for e in range(N_EXPERTS):
        pos = pos + jnp.where(e8 == e, start_ref[e], 0)
    idx_ref[0] = pos
    idx_ref[1] = pos + n_rows


def _positions(start, e8, r8, n_rows, tn):
    n_tok = e8.shape[1]
    grid_spec = pltpu.PrefetchScalarGridSpec(
        num_scalar_prefetch=1,
        grid=(n_tok // tn,),
        in_specs=[pl.BlockSpec((TOP_K, tn), lambda i, st: (0, i)),
                  pl.BlockSpec((TOP_K, tn), lambda i, st: (0, i))],
        out_specs=pl.BlockSpec((2, TOP_K, tn), lambda i, st: (0, 0, i)),
    )
    return pl.pallas_call(
        functools.partial(_positions_kernel, n_rows=n_rows),
        out_shape=jax.ShapeDtypeStruct((2, TOP_K, n_tok), jnp.int32),
        grid_spec=grid_spec,
        compiler_params=_cparams("parallel"),
        name="moe_positions",
    )(start, e8, r8)


def _experts_kernel(te_ref, used_ref, xs_ref, wg_ref, wu_ref, wd_ref, ys_ref):
    @pl.when(pl.program_id(0) < used_ref[0])
    def _():
        x = jnp.concatenate([_unpack_half(xs_ref[0]), _unpack_half(xs_ref[1])], axis=1).astype(BF16)
        h = (jax.nn.silu(jnp.dot(x, wg_ref[0], preferred_element_type=F32))
             * jnp.dot(x, wu_ref[0], preferred_element_type=F32))
        y = jnp.dot(h.astype(BF16), wd_ref[0], preferred_element_type=F32)
        ys_ref[0] = _pack_half(y[:, :HALF])
        ys_ref[1] = _pack_half(y[:, HALF:])


def _experts(tile_expert, n_used, xs, eg, eu, ed):
    n_rows = xs.shape[1]
    _, d, f = eg.shape
    grid_spec = pltpu.PrefetchScalarGridSpec(
        num_scalar_prefetch=2,
        grid=(n_rows // ROW_TILE,),
        in_specs=[
            pl.BlockSpec((2, ROW_TILE, HALF // 2), lambda i, te, nu: (0, i, 0)),
            pl.BlockSpec((1, d, f), lambda i, te, nu: (te[i], 0, 0)),
            pl.BlockSpec((1, d, f), lambda i, te, nu: (te[i], 0, 0)),
            pl.BlockSpec((1, f, d), lambda i, te, nu: (te[i], 0, 0)),
        ],
        out_specs=pl.BlockSpec((2, ROW_TILE, HALF // 2), lambda i, te, nu: (0, i, 0)),
    )
    return pl.pallas_call(
        _experts_kernel,
        out_shape=jax.ShapeDtypeStruct(xs.shape, jnp.int32),
        grid_spec=grid_spec,
        compiler_params=_cparams("parallel"),
        name="moe_experts",
    )(tile_expert, n_used, xs, eg, eu, ed)


def _combine_kernel(x_ref, yg_ref, w8_ref, sg_ref, su_ref, sd_ref, g_ref, b_ref, o_ref, *, alpha):
    x = x_ref[...]
    tm = x.shape[0]
    xh = x.astype(BF16)
    hs = (jax.nn.silu(jnp.dot(xh, sg_ref[...], preferred_element_type=F32))
          * jnp.dot(xh, su_ref[...], preferred_element_type=F32))
    acc = jnp.dot(hs.astype(BF16), sd_ref[...], preferred_element_type=F32)
    w8 = jnp.concatenate([w8_ref[...], jnp.zeros((LANES - TOP_K, tm), F32)], axis=0).T
    for k in range(TOP_K):
        yk = jnp.concatenate([_unpack_half(yg_ref[0, k]), _unpack_half(yg_ref[1, k])], axis=1)
        acc = acc + w8[:, k:k + 1] * yk
    o_ref[...] = _layer_norm(alpha * x + acc, g_ref[...], b_ref[...])


def _combine(x, yg, w8, sg, su, sd, g, b, tm, alpha):
    n_tok, d = x.shape
    const2 = lambda i: (0, 0)
    return pl.pallas_call(
        functools.partial(_combine_kernel, alpha=alpha),
        out_shape=jax.ShapeDtypeStruct((n_tok, d), F32),
        grid=(n_tok // tm,),
        in_specs=[
            pl.BlockSpec((tm, d), lambda i: (i, 0)),
            pl.BlockSpec((2, TOP_K, tm, HALF // 2), lambda i: (0, 0, i, 0)),
            pl.BlockSpec((TOP_K, tm), lambda i: (0, i)),
            pl.BlockSpec(sg.shape, const2),
            pl.BlockSpec(su.shape, const2),
            pl.BlockSpec(sd.shape, const2),
            pl.BlockSpec((1, d), const2),
            pl.BlockSpec((1, d), const2),
        ],
        out_specs=pl.BlockSpec((tm, d), lambda i: (i, 0)),
        compiler_params=_cparams("parallel"),
        name="moe_combine",
    )(x, yg, w8, sg, su, sd, g, b)


def _sc_mesh():
    return plsc.VectorSubcoreMesh(core_axis_name="c", subcore_axis_name="s")


def _sc_gather_rows(table, idx):
    n = idx.shape[0]
    d = table.shape[1]

    @functools.partial(pl.kernel, out_type=jax.ShapeDtypeStruct((n, d), table.dtype),
                       mesh=_sc_mesh())
    def gather(table_hbm, idx_hbm, out_hbm):
        def body(idx_vmem, out_vmem):
            pltpu.sync_copy(table_hbm.at[idx_vmem.at[0]], out_vmem)

        pltpu.emit_pipeline(
            body, grid=(n // SC_WINDOW,),
            in_specs=[pl.BlockSpec((1, SC_WINDOW), index_map=lambda i: (0, i))],
            out_specs=[pl.BlockSpec((SC_WINDOW, d), index_map=lambda i: (i, 0))],
            core_axis_name=("c", "s"), dimension_semantics=(pltpu.PARALLEL,), trace_scopes=False,
        )(idx_hbm, out_hbm)

    return gather(table, idx.reshape(1, n))


def _sc_scatter_rows(rows, idx, n_out, src_block):
    n = idx.shape[0]
    d = rows.shape[1]

    @functools.partial(pl.kernel, out_type=jax.ShapeDtypeStruct((n_out, d), rows.dtype),
                       mesh=_sc_mesh())
    def scatter(rows_hbm, idx_hbm, out_hbm):
        def body(rows_vmem, idx_vmem):
            pltpu.sync_copy(rows_vmem, out_hbm.at[idx_vmem.at[0]])

        pltpu.emit_pipeline(
            body, grid=(n // SC_WINDOW,),
            in_specs=[pl.BlockSpec((SC_WINDOW, d), index_map=lambda i: (src_block(i), 0)),
                      pl.BlockSpec((1, SC_WINDOW), index_map=lambda i: (0, i))],
            out_specs=[],
            core_axis_name=("c", "s"), dimension_semantics=(pltpu.PARALLEL,), trace_scopes=False,
        )(rows_hbm, idx_hbm)

    return scatter(rows, idx.reshape(1, n))


def _moe_dispatch(x, rth, rtl, rbias, tm):
    n_tok, d = x.shape
    n_rows = TOP_K * n_tok + N_EXPERTS * ROW_TILE
    n_tiles = n_rows // ROW_TILE
    xp, e8, r8, w8, cnt = _route(x, rth, rtl, rbias, tm)
    counts = cnt[:, 0].astype(jnp.int32)
    padded = (counts + ROW_TILE - 1) // ROW_TILE * ROW_TILE
    ends = jnp.cumsum(padded)
    start = ends - padded
    n_used = (ends[-1:] // ROW_TILE).astype(jnp.int32)
    tile_ids = jnp.arange(n_tiles, dtype=jnp.int32)
    tile_expert = jnp.minimum(
        jnp.sum((tile_ids[:, None] >= (ends // ROW_TILE)[None, :]).astype(jnp.int32), axis=1),
        N_EXPERTS - 1)
    idx = _positions(start, e8, r8, n_rows, _pick_tile(n_tok, 2048)).reshape(-1)
    blocks_per_half = n_tok // SC_WINDOW
    xs = _sc_scatter_rows(
        xp.reshape(2 * n_tok, HALF // 2), idx, 2 * n_rows,
        lambda i: (i // (TOP_K * blocks_per_half)) * blocks_per_half + i % blocks_per_half)
    return xs.reshape(2, n_rows, HALF // 2), idx, tile_expert, n_used, w8


def _moe_experts(xs, idx, tile_expert, n_used, eg, eu, ed):
    n_rows = xs.shape[1]
    ys = _experts(tile_expert, n_used, xs, eg, eu, ed)
    yg = _sc_gather_rows(ys.reshape(2 * n_rows, HALF // 2), idx)
    return yg.reshape(2, TOP_K, idx.shape[0] // (2 * TOP_K), HALF // 2)


def _rope_tables(seq):
    t = jnp.arange(seq)
    row = (t // GRID_W).astype(F32)
    colp = (t % GRID_W).astype(F32)
    inv = ROPE_THETA ** (-jnp.arange(0, ROPE_AXIS_DIM, 2, dtype=F32) / ROPE_AXIS_DIM)
    ang = jnp.concatenate([row[:, None] * inv[None], colp[:, None] * inv[None]], axis=-1)
    ang = jnp.repeat(ang, 2, axis=-1)
    sign = jnp.where(jnp.arange(HEAD_DIM) % 2 == 0, -1.0, 1.0).astype(F32)
    cos = jnp.tile(jnp.cos(ang), (1, 2))
    sin = jnp.tile(jnp.sin(ang) * sign[None], (1, 2))
    return cos, sin


def _gqa_head_order():
    order = []
    for kvp in range(N_KV_HEADS // 2):
        base = 2 * GQA_GROUP * kvp
        for g in range(GQA_GROUP):
            order += [base + g, base + GQA_GROUP + g]
    return np.array(order)


def _pick_tile(n, target):
    t = min(n, target)
    while n % t:
        t //= 2
    return t


def kernel(x_prompt, x_sample, mem_prompt, mem_sample, a_w_in, a_lambda, a_subln, b_w_in, b_rpb,
           c_w_in, c_q_norm, c_k_norm, w_mem_kv, w_out, ln1_g, ln1_b, router, router_bias,
           e_gate, e_up, e_down, s_gate, s_up, s_down, ln2_g, ln2_b):
    depth = w_out.shape[0]
    alpha = (2.0 * depth) ** 0.25
    seq = x_prompt.shape[1]
    d = x_prompt.shape[2]
    assert x_sample.shape[1] == seq and seq % (NA_ROWS * GRID_W) == 0
    n_mem = mem_prompt.shape[1]
    batches = [x_prompt.shape[0], x_sample.shape[0]]
    xs_ = [x_prompt.reshape(-1, d), x_sample.reshape(-1, d)]
    mems = [mem_prompt.reshape(-1, d), mem_sample.reshape(-1, d)]

    tm = _pick_tile(seq, 512)
    tk = _pick_tile(seq, 512)
    head_order = _gqa_head_order()
    col_order = (head_order[:, None] * HEAD_DIM + np.arange(HEAD_DIM)[None]).reshape(-1)
    cos, sin = _rope_tables(seq)
    slopes = _alibi_slopes(N_MIX_HEADS).reshape(N_MIX_PAIRS, 2)
    slope_tab = np.zeros((N_MIX_PAIRS, 8, LANES), np.float32)
    slope_tab[:, 0:2, :] = slopes[:, :, None]
    slope_tab = jnp.asarray(slope_tab) * LOG2E
    ones_row = functools.partial(jnp.ones, dtype=F32)
    mem_q_scale = jnp.full((MEM_WIDTH,), HEAD_DIM ** -0.5, F32)

    for i in range(depth):
        mixer, occ = i % N_MIXERS, i // N_MIXERS
        w_o = w_out[i]
        rope = None
        if mixer == 0:
            w_in = a_w_in[occ]
            q_scale = DIFF_QK_DIM ** -0.5 * LOG2E
            gain = jnp.tile(a_subln[occ].astype(F32), 2)[None]
            lam = a_lambda[occ].astype(F32)
        elif mixer == 1:
            w_in = b_w_in[occ]
            q_scale = HEAD_DIM ** -0.5
            na_bias = _na_bias_table(b_rpb[occ])
        else:
            w_in = c_w_in[occ]
            w_in = jnp.concatenate([w_in[:, :MIX_WIDTH][:, col_order], w_in[:, MIX_WIDTH:]], axis=1)
            w_o = jnp.concatenate([w_o[:MIX_WIDTH][col_order], w_o[MIX_WIDTH:]], axis=0)
            q_scale = HEAD_DIM ** -0.5 * LOG2E
            qg = jnp.tile(c_q_norm[occ].astype(F32), 2)[None, None]
            kg = jnp.tile(c_k_norm[occ].astype(F32), 2)[None, None]
            gains = jnp.concatenate([jnp.tile(qg, (N_MIX_PAIRS, 1, 1)),
                                     jnp.tile(kg, (N_KV_HEADS // 2, 1, 1))], axis=0)
            rope = (gains, cos, sin)
        colscale = jnp.concatenate([
            jnp.full((MIX_WIDTH,), q_scale, F32),
            ones_row((w_in.shape[1] - MIX_WIDTH - MEM_WIDTH,)), mem_q_scale])[None]
        w_in = w_in.astype(BF16)
        w_o = w_o.astype(BF16)
        w_kv = w_mem_kv[i].astype(BF16)
        rt = router[i].T.astype(F32)
        rth = rt.astype(BF16)
        rtl = (rt - rth.astype(F32)).astype(BF16)
        rbias = router_bias[i].astype(F32)[:, None]
        eg, eu, ed = e_gate[i].astype(BF16), e_up[i].astype(BF16), e_down[i].astype(BF16)
        sg, su, sd = s_gate[i].astype(BF16), s_up[i].astype(BF16), s_down[i].astype(BF16)

        dispatched = []
        for s in range(len(xs_)):
            x, batch = xs_[s], batches[s]
            heads = _in_projection(x, w_in, colscale, tm, seq, rope=rope)
            if mixer == 0:
                mix = _diff_attention(heads, slope_tab, lam, gain, batch, seq,
                                      _pick_tile(seq, 256), _pick_tile(seq, 1024),
                                      _diff_lambda_init(i))
            elif mixer == 1:
                mix = _na_attention(heads, na_bias, batch, seq)
            else:
                mix = _gqa_attention(heads, batch, seq, _pick_tile(seq, 512), tk)
            kv = _matmul(mems[s], w_kv, _pick_tile(batch * n_mem, 512))
            kv = kv.reshape(batch, n_mem, 2 * MEM_WIDTH)
            x = _out_projection(mix, heads, kv, x, w_o, ln1_g[i][None], ln1_b[i][None],
                                batch, seq, tm, alpha)
            xs_[s] = x
            dispatched.append(_moe_dispatch(x, rth, rtl, rbias, tm))
        gathered = [_moe_experts(*disp[:4], eg, eu, ed) for disp in dispatched]
        for s in range(len(xs_)):
            xs_[s] = _combine(xs_[s], gathered[s], dispatched[s][4], sg, su, sd,
                              ln2_g[i][None], ln2_b[i][None], tm, alpha)

    return tuple(x.reshape(b, seq, d) for x, b in zip(xs_, batches))
```

```python
import functools
import math

import numpy as np
import jax
import jax.numpy as jnp
from jax import lax
from jax.experimental import pallas as pl
from jax.experimental.pallas import tpu as pltpu
from jax.experimental.pallas import tpu_sc as plsc

F32 = jnp.float32
BF16 = jnp.bfloat16

D_MODEL = 1024
HEAD_DIM = 64
LANES = 128
N_MIX_HEADS = 12
N_MIX_PAIRS = N_MIX_HEADS // 2
N_MEM_HEADS = 4
N_MEM_PAIRS = N_MEM_HEADS // 2
MIX_WIDTH = N_MIX_HEADS * HEAD_DIM
MEM_WIDTH = N_MEM_HEADS * HEAD_DIM
N_MIXERS = 3
DIFF_QK_DIM = HEAD_DIM // 2
GRID_W = 64
NA_ROWS = 8
NA_COLS = 16
N_KV_HEADS = 4
GQA_GROUP = N_MIX_HEADS // N_KV_HEADS
ROPE_THETA = 10000.0
ROPE_AXIS_DIM = HEAD_DIM // 2
N_EXPERTS = 64
TOP_K = 8
N_GROUPS = 8
TOPK_GROUPS = 4
PER_GROUP = N_EXPERTS // N_GROUPS
D_EXPERT = 256
ROUTED_SCALE = 2.5
LN_EPS = 1e-5
RMS_EPS = 1e-6
NEG_BIG = -1e30
LOG2E = math.log2(math.e)

VMEM_LIMIT = 48 * 1024 * 1024

NT_DIMS = (((1,), (1,)), ((), ()))
TN_DIMS = (((0,), (0,)), ((), ()))


def _cparams(*sem):
    return pltpu.CompilerParams(dimension_semantics=sem, vmem_limit_bytes=VMEM_LIMIT)


def _alibi_slopes(n):
    def pow2_slopes(m):
        start = 2.0 ** (-8.0 / m)
        return [start ** (i + 1) for i in range(m)]
    if math.log2(n).is_integer():
        s = pow2_slopes(n)
    else:
        c = 2 ** math.floor(math.log2(n))
        s = pow2_slopes(c) + pow2_slopes(2 * c)[0::2][: n - c]
    return np.array(s, np.float32)


def _diff_lambda_init(layer_idx):
    return 0.8 - 0.6 * math.exp(-0.3 * layer_idx)


def _lane_iota(shape):
    return lax.broadcasted_iota(jnp.int32, shape, len(shape) - 1)


def _layer_norm(y, g, b):
    mu = jnp.mean(y, axis=-1, keepdims=True)
    yc = y - mu
    var = jnp.mean(yc * yc, axis=-1, keepdims=True)
    return yc * lax.rsqrt(var + LN_EPS) * g + b


def _head_rms_scale(x):
    lo = _lane_iota(x.shape) < HEAD_DIM
    xx = x * x
    ss_lo = jnp.sum(jnp.where(lo, xx, 0.0), axis=-1, keepdims=True)
    ss_hi = jnp.sum(jnp.where(lo, 0.0, xx), axis=-1, keepdims=True)
    inv = 1.0 / HEAD_DIM
    return jnp.where(lo, lax.rsqrt(ss_lo * inv + RMS_EPS), lax.rsqrt(ss_hi * inv + RMS_EPS))


def _inproj_kernel(x_ref, w_ref, cs_ref, o_ref):
    r = jnp.dot(x_ref[...].astype(BF16), w_ref[...], preferred_element_type=F32)
    r = r * cs_ref[...]
    for p in range(o_ref.shape[0]):
        o_ref[p] = r[:, p * LANES:(p + 1) * LANES].astype(BF16)


def _inproj_rope_kernel(x_ref, w_ref, cs_ref, g_ref, cos_ref, sin_ref, o_ref, *, n_rope_pairs):
    r = jnp.dot(x_ref[...].astype(BF16), w_ref[...], preferred_element_type=F32)
    cs = cs_ref[...]
    cos = cos_ref[...]
    sin = sin_ref[...]
    even = (_lane_iota(cos.shape) % 2) == 0
    for p in range(o_ref.shape[0]):
        xp = r[:, p * LANES:(p + 1) * LANES]
        if p < n_rope_pairs:
            xn = xp * _head_rms_scale(xp) * g_ref[p]
            partner = jnp.where(even, pltpu.roll(xn, LANES - 1, 1), pltpu.roll(xn, 1, 1))
            xp = xn * cos + partner * sin
        o_ref[p] = (xp * cs[:, p * LANES:(p + 1) * LANES]).astype(BF16)


def _in_projection(x, w, colscale, tm, seq, rope=None):
    n_tok, d = x.shape
    width = w.shape[1]
    n_pairs = width // LANES
    in_specs = [
        pl.BlockSpec((tm, d), lambda i: (i, 0)),
        pl.BlockSpec((d, width), lambda i: (0, 0)),
        pl.BlockSpec((1, width), lambda i: (0, 0)),
    ]
    args = [x, w, colscale]
    if rope is None:
        body = _inproj_kernel
    else:
        gains, cos, sin = rope
        tiles_per_seq = seq // tm
        body = functools.partial(_inproj_rope_kernel, n_rope_pairs=gains.shape[0])
        in_specs += [
            pl.BlockSpec(gains.shape, lambda i: (0, 0, 0)),
            pl.BlockSpec((tm, LANES), lambda i: (i % tiles_per_seq, 0)),
            pl.BlockSpec((tm, LANES), lambda i: (i % tiles_per_seq, 0)),
        ]
        args += [gains, cos, sin]
    return pl.pallas_call(
        body,
        out_shape=jax.ShapeDtypeStruct((n_pairs, n_tok, LANES), BF16),
        grid=(n_tok // tm,),
        in_specs=in_specs,
        out_specs=pl.BlockSpec((n_pairs, tm, LANES), lambda i: (0, i, 0)),
        compiler_params=_cparams("parallel"),
        name="in_projection",
    )(*args)


def _matmul_kernel(x_ref, w_ref, o_ref):
    o_ref[...] = jnp.dot(x_ref[...].astype(BF16), w_ref[...],
                         preferred_element_type=F32).astype(o_ref.dtype)


def _matmul(x, w, tm):
    m, k = x.shape
    n = w.shape[1]
    return pl.pallas_call(
        _matmul_kernel,
        out_shape=jax.ShapeDtypeStruct((m, n), BF16),
        grid=(m // tm,),
        in_specs=[pl.BlockSpec((tm, k), lambda i: (i, 0)), pl.BlockSpec((k, n), lambda i: (0, 0))],
        out_specs=pl.BlockSpec((tm, n), lambda i: (i, 0)),
        compiler_params=_cparams("parallel"),
        name="mem_kv_projection",
    )(x, w)


FLASH_UNROLL = 2


def _stack_masked_q(q_ref, qs_ref, tq):
    n_qpairs = q_ref.shape[0]
    per_pair = qs_ref.shape[0] // tq // n_qpairs
    lanes_per_map = LANES // per_pair
    for t in range(n_qpairs):
        q = q_ref[t].astype(F32)
        part = _lane_iota(q.shape) // lanes_per_map
        for a in range(per_pair):
            r0 = (t * per_pair + a) * tq
            qs_ref[r0:r0 + tq] = jnp.where(part == a, q, 0.0).astype(BF16)


def _softmax_accumulate(s, v, m, l, acc_ref):
    m_new = jnp.maximum(m, jnp.max(s, axis=0, keepdims=True))
    alpha = jnp.exp2(m - m_new)
    p = jnp.exp2(s - m_new)
    l_new = alpha * l + jnp.sum(p, axis=0, keepdims=True)
    pv = lax.dot_general(v, p.astype(BF16), TN_DIMS, preferred_element_type=F32)
    acc_ref[...] = alpha * acc_ref[...] + pv
    return m_new, l_new


def _diff_attn_kernel(q_ref, k_ref, v_ref, slope_ref, lam_ref, g_ref, o_ref, qs_ref, acc_ref,
                      *, tq, tk, seq, lambda_init):
    i = pl.program_id(2)
    n_maps = 4
    w = n_maps * tq
    _stack_masked_q(q_ref, qs_ref, tq)
    acc_ref[...] = jnp.zeros(acc_ref.shape, F32)
    slopes = slope_ref[0]
    rel0 = (lax.broadcasted_iota(jnp.int32, (tk, tq), 0)
            - lax.broadcasted_iota(jnp.int32, (tk, tq), 1) - i * tq)

    def chunk(j, carry):
        m, l = carry
        off = pl.multiple_of(j * tk, tk)
        k = k_ref[0, pl.ds(off, tk), :]
        v = v_ref[0, pl.ds(off, tk), :]
        s = lax.dot_general(k, qs_ref[...], NT_DIMS, preferred_element_type=F32)
        dist = jnp.abs(rel0 + j * tk).astype(F32)
        s = jnp.concatenate(
            [s[:, a * tq:(a + 1) * tq] - slopes[a // 2:a // 2 + 1, 0:1] * dist
             for a in range(n_maps)], axis=1)
        return _softmax_accumulate(s, v, m, l, acc_ref)

    carry = (jnp.full((1, w), -jnp.inf, F32), jnp.zeros((1, w), F32))
    m, l = lax.fori_loop(0, seq // tk, chunk, carry, unroll=FLASH_UNROLL)

    o = acc_ref[...] / l
    lam = lam_ref[...]
    lam_full = (jnp.exp(jnp.sum(lam[0:1] * lam[1:2], axis=-1, keepdims=True))
                - jnp.exp(jnp.sum(lam[2:3] * lam[3:4], axis=-1, keepdims=True)) + lambda_init)
    h0 = o[:, 0:tq] - lam_full * o[:, tq:2 * tq]
    h1 = o[:, 2 * tq:3 * tq] - lam_full * o[:, 3 * tq:4 * tq]
    row = lax.broadcasted_iota(jnp.int32, (LANES, tq), 0)
    o = jnp.where(row < HEAD_DIM, h0, h1).T
    o = o * _head_rms_scale(o) * g_ref[...] * (1.0 - lambda_init)
    o_ref[0] = o.astype(BF16)


def _diff_attention(heads, slopes, lam, gain, batch, seq, tq, tk, lambda_init):
    n_tok = heads.shape[1]
    nq = seq // tq
    body = functools.partial(_diff_attn_kernel, tq=tq, tk=tk, seq=seq, lambda_init=lambda_init)
    return pl.pallas_call(
        body,
        out_shape=jax.ShapeDtypeStruct((N_MIX_PAIRS, n_tok, LANES), BF16),
        grid=(batch, N_MIX_PAIRS, nq),
        in_specs=[
            pl.BlockSpec((1, tq, LANES), lambda b, p, i: (p, b * nq + i, 0)),
            pl.BlockSpec((1, seq, LANES), lambda b, p, i: (N_MIX_PAIRS + p, b, 0)),
            pl.BlockSpec((1, seq, LANES), lambda b, p, i: (2 * N_MIX_PAIRS + p, b, 0)),
            pl.BlockSpec((1, 8, LANES), lambda b, p, i: (p, 0, 0)),
            pl.BlockSpec(lam.shape, lambda b, p, i: (0, 0)),
            pl.BlockSpec((1, LANES), lambda b, p, i: (0, 0)),
        ],
        out_specs=pl.BlockSpec((1, tq, LANES), lambda b, p, i: (p, b * nq + i, 0)),
        scratch_shapes=[
            pltpu.VMEM((4 * tq, LANES), BF16),
            pltpu.VMEM((LANES, 4 * tq), F32),
        ],
        compiler_params=_cparams("parallel", "parallel", "parallel"),
        name="diff_attention",
    )(heads, heads, heads, slopes, lam, gain)


def _gqa_attn_kernel(q_ref, k_ref, v_ref, o_ref, qs_ref, acc_ref, *, tq, tk, seq):
    n_qpairs = q_ref.shape[0]
    w = 2 * n_qpairs * tq
    _stack_masked_q(q_ref, qs_ref, tq)
    acc_ref[...] = jnp.zeros(acc_ref.shape, F32)

    def chunk(j, carry):
        m, l = carry
        off = pl.multiple_of(j * tk, tk)
        k = k_ref[0, pl.ds(off, tk), :]
        v = v_ref[0, pl.ds(off, tk), :]
        s = lax.dot_general(k, qs_ref[...], NT_DIMS, preferred_element_type=F32)
        return _softmax_accumulate(s, v, m, l, acc_ref)

    carry = (jnp.full((1, w), -jnp.inf, F32), jnp.zeros((1, w), F32))
    m, l = lax.fori_loop(0, seq // tk, chunk, carry, unroll=FLASH_UNROLL)
    o = acc_ref[...] / l
    row = lax.broadcasted_iota(jnp.int32, (LANES, tq), 0)
    for t in range(n_qpairs):
        lo = o[:, (2 * t) * tq:(2 * t + 1) * tq]
        hi = o[:, (2 * t + 1) * tq:(2 * t + 2) * tq]
        o_ref[t] = jnp.where(row < HEAD_DIM, lo, hi).T.astype(BF16)


def _gqa_attention(heads, batch, seq, tq, tk):
    n_tok = heads.shape[1]
    nq = seq // tq
    n_kv_pairs = N_KV_HEADS // 2
    qp = N_MIX_PAIRS // n_kv_pairs
    body = functools.partial(_gqa_attn_kernel, tq=tq, tk=tk, seq=seq)
    return pl.pallas_call(
        body,
        out_shape=jax.ShapeDtypeStruct((N_MIX_PAIRS, n_tok, LANES), BF16),
        grid=(batch, n_kv_pairs, nq),
        in_specs=[
            pl.BlockSpec((qp, tq, LANES), lambda b, p, i: (p, b * nq + i, 0)),
            pl.BlockSpec((1, seq, LANES), lambda b, p, i: (N_MIX_PAIRS + p, b, 0)),
            pl.BlockSpec((1, seq, LANES), lambda b, p, i: (N_MIX_PAIRS + n_kv_pairs + p, b, 0)),
        ],
        out_specs=pl.BlockSpec((qp, tq, LANES), lambda b, p, i: (p, b * nq + i, 0)),
        scratch_shapes=[
            pltpu.VMEM((2 * qp * tq, LANES), BF16),
            pltpu.VMEM((LANES, 2 * qp * tq), F32),
        ],
        compiler_params=_cparams("parallel", "parallel", "parallel"),
        name="gqa_attention",
    )(heads, heads, heads)


def _na_attn_kernel(q_ref, kp_ref, kc_ref, kn_ref, vp_ref, vc_ref, vn_ref, bias_ref, o_ref,
                    kbuf_ref, vbuf_ref, *, n_grid_rows):
    rb = pl.program_id(2)
    blk = NA_ROWS * GRID_W
    win = NA_ROWS * GRID_W
    kbuf_ref[0:blk] = kp_ref[0]
    kbuf_ref[blk:2 * blk] = kc_ref[0]
    kbuf_ref[2 * blk:3 * blk] = kn_ref[0]
    vbuf_ref[0:blk] = vp_ref[0]
    vbuf_ref[blk:2 * blk] = vc_ref[0]
    vbuf_ref[2 * blk:3 * blk] = vn_ref[0]
    lo = _lane_iota((GRID_W, LANES)) < HEAD_DIM
    offs, scores = [], []
    for u in range(NA_ROWS):
        r = rb * NA_ROWS + u
        rs = jnp.clip(r - NA_ROWS // 2, 0, n_grid_rows - NA_ROWS)
        off = pl.multiple_of((rs - (rb - 1) * NA_ROWS) * GRID_W, GRID_W)
        ro0 = rs - r + NA_ROWS - 1
        q = q_ref[0, u * GRID_W:(u + 1) * GRID_W].astype(F32)
        qs = jnp.concatenate([jnp.where(lo, q, 0.0), jnp.where(lo, 0.0, q)], axis=0).astype(BF16)
        s = lax.dot_general(qs, kbuf_ref[pl.ds(off, win)], NT_DIMS,
                            preferred_element_type=F32)
        bias = jnp.concatenate(
            [jnp.concatenate([bias_ref[h, ro0 + 2 * t] for t in range(NA_ROWS // 2)], axis=1)
             for h in range(2)], axis=0)
        offs.append(off)
        scores.append(s + bias)
    probs = []
    for s in scores:
        p = jnp.exp(s - jnp.max(s, axis=-1, keepdims=True))
        probs.append((p.astype(BF16), jnp.sum(p, axis=-1, keepdims=True)))
    for u, (p, l) in enumerate(probs):
        o = jnp.dot(p, vbuf_ref[pl.ds(offs[u], win)], preferred_element_type=F32) / l
        o_ref[0, u * GRID_W:(u + 1) * GRID_W] = jnp.where(lo, o[0:GRID_W], o[GRID_W:]).astype(BF16)


def _na_attention(heads, bias, batch, seq):
    n_tok = heads.shape[1]
    n_grid_rows = seq // GRID_W
    nrb = n_grid_rows // NA_ROWS
    blk = NA_ROWS * GRID_W

    def kv_spec(first_pair, shift):
        return pl.BlockSpec(
            (1, blk, LANES),
            lambda b, p, rb: (first_pair + p, b * nrb + jnp.clip(rb + shift, 0, nrb - 1), 0))

    body = functools.partial(_na_attn_kernel, n_grid_rows=n_grid_rows)
    return pl.pallas_call(
        body,
        out_shape=jax.ShapeDtypeStruct((N_MIX_PAIRS, n_tok, LANES), BF16),
        grid=(batch, N_MIX_PAIRS, nrb),
        in_specs=[
            pl.BlockSpec((1, blk, LANES), lambda b, p, rb: (p, b * nrb + rb, 0)),
            kv_spec(N_MIX_PAIRS, -1), kv_spec(N_MIX_PAIRS, 0), kv_spec(N_MIX_PAIRS, 1),
            kv_spec(2 * N_MIX_PAIRS, -1), kv_spec(2 * N_MIX_PAIRS, 0), kv_spec(2 * N_MIX_PAIRS, 1),
            pl.BlockSpec((2,) + bias.shape[1:], lambda b, p, rb: (p, 0, 0, 0)),
        ],
        out_specs=pl.BlockSpec((1, blk, LANES), lambda b, p, rb: (p, b * nrb + rb, 0)),
        scratch_shapes=[pltpu.VMEM((3 * blk, LANES), BF16), pltpu.VMEM((3 * blk, LANES), BF16)],
        compiler_params=_cparams("parallel", "parallel", "arbitrary"),
        name="neighbourhood_attention",
    )(heads, heads, heads, heads, heads, heads, heads, bias)


def _na_bias_table(rpb):
    col = jnp.arange(GRID_W)
    col_start = jnp.clip(col - NA_COLS // 2, 0, GRID_W - NA_COLS)
    col_mask = (col[None, :] >= col_start[:, None]) & (col[None, :] < col_start[:, None] + NA_COLS)
    col_bias_idx = jnp.clip(col[None, :] - col[:, None] + NA_COLS - 1, 0, 2 * NA_COLS - 2)
    tiles = rpb.astype(F32)[:, :, col_bias_idx]
    tiles = jnp.where(col_mask[None, None], tiles, NEG_BIG)
    return jnp.concatenate([tiles[:, :-1], tiles[:, 1:]], axis=-1)


def _outproj_kernel(mix_ref, qm_ref, kv_ref, x_ref, w_ref, g_ref, b_ref, o_ref, *, alpha):
    tm = x_ref.shape[0]
    lo = _lane_iota((tm, LANES)) < HEAD_DIM
    parts = [mix_ref[p] for p in range(mix_ref.shape[0])]
    kv = kv_ref[0]
    scores = []
    for t in range(N_MEM_PAIRS):
        q = qm_ref[t].astype(F32)
        qs = jnp.concatenate([jnp.where(lo, q, 0.0), jnp.where(lo, 0.0, q)], axis=0).astype(BF16)
        kt = kv[:, t * LANES:(t + 1) * LANES]
        scores.append(lax.dot_general(qs, kt, NT_DIMS, preferred_element_type=F32))
    probs = []
    for s in scores:
        p = jnp.exp(s - jnp.max(s, axis=-1, keepdims=True))
        probs.append((p.astype(BF16), jnp.sum(p, axis=-1, keepdims=True)))
    for t, (p, l) in enumerate(probs):
        vt = kv[:, MEM_WIDTH + t * LANES:MEM_WIDTH + (t + 1) * LANES]
        o = jnp.dot(p, vt, preferred_element_type=F32) / l
        parts.append(jnp.where(lo, o[0:tm], o[tm:]).astype(BF16))
    attn = jnp.concatenate(parts, axis=1)
    h = jnp.dot(attn, w_ref[...], preferred_element_type=F32)
    o_ref[...] = _layer_norm(alpha * x_ref[...] + h, g_ref[...], b_ref[...])


def _out_projection(mix, heads, kv, x, w_out, g, b, batch, seq, tm, alpha):
    n_tok, d = x.shape
    nt = seq // tm
    n_pairs = heads.shape[0]
    n_mem = kv.shape[1]
    body = functools.partial(_outproj_kernel, alpha=alpha)
    return pl.pallas_call(
        body,
        out_shape=jax.ShapeDtypeStruct((n_tok, d), F32),
        grid=(batch, nt),
        in_specs=[
            pl.BlockSpec((N_MIX_PAIRS, tm, LANES), lambda bi, i: (0, bi * nt + i, 0)),
            pl.BlockSpec((N_MEM_PAIRS, tm, LANES),
                         lambda bi, i: (n_pairs // N_MEM_PAIRS - 1, bi * nt + i, 0)),
            pl.BlockSpec((1, n_mem, 2 * MEM_WIDTH), lambda bi, i: (bi, 0, 0)),
            pl.BlockSpec((tm, d), lambda bi, i: (bi * nt + i, 0)),
            pl.BlockSpec(w_out.shape, lambda bi, i: (0, 0)),
            pl.BlockSpec((1, d), lambda bi, i: (0, 0)),
            pl.BlockSpec((1, d), lambda bi, i: (0, 0)),
        ],
        out_specs=pl.BlockSpec((tm, d), lambda bi, i: (bi * nt + i, 0)),
        compiler_params=_cparams("parallel", "parallel"),
        name="mem_attention_out_projection",
    )(mix, heads, kv, x, w_out, g, b)


ROW_TILE = 1024
SC_WINDOW = 128
HALF = D_MODEL // 2


def _pack_half(v):
    q = HALF // 2
    return pltpu.pack_elementwise([v[:, :q], v[:, q:]], packed_dtype=BF16)


def _unpack_half(w):
    return jnp.concatenate(
        [pltpu.unpack_elementwise(w, index=h, packed_dtype=BF16, unpacked_dtype=F32)
         for h in range(2)], axis=1)

def _first_index_of_max(vals, index, sentinel, axes):
    mx = vals
    for ax in axes:
        mx = jnp.max(mx, axis=ax, keepdims=True)
    idx = jnp.where(vals == mx, index, sentinel)
    for ax in axes:
        idx = jnp.min(idx, axis=ax, keepdims=True)
    return mx, idx


def _route_kernel(x_ref, rth_ref, rtl_ref, rbias_ref, tri_ref,
                  xp_ref, e8_ref, r8_ref, w8_ref, cnt_ref, run_ref):
    @pl.when(pl.program_id(0) == 0)
    def _():
        run_ref[...] = jnp.zeros(run_ref.shape, F32)

    x = x_ref[...]
    tm = x.shape[0]
    xp_ref[0] = _pack_half(x[:, :HALF])
    xp_ref[1] = _pack_half(x[:, HALF:])
    xh = x.astype(BF16)
    xl = (x - xh.astype(F32)).astype(BF16)
    rth = rth_ref[...]
    logits = (lax.dot_general(rth, xh, NT_DIMS, preferred_element_type=F32)
              + lax.dot_general(rth, xl, NT_DIMS, preferred_element_type=F32)
              + lax.dot_general(rtl_ref[...], xh, NT_DIMS, preferred_element_type=F32))
    scores = jax.nn.sigmoid(logits)
    choice = scores + rbias_ref[...]
    shape3 = (N_GROUPS, PER_GROUP, tm)
    c3 = choice.reshape(shape3)
    s3 = scores.reshape(shape3)
    e_in_g = lax.broadcasted_iota(jnp.int32, shape3, 1).astype(F32)
    g_idx = lax.broadcasted_iota(jnp.int32, (N_GROUPS, 1, tm), 0).astype(F32)
    e_idx = lax.broadcasted_iota(jnp.int32, shape3, 0).astype(F32) * PER_GROUP + e_in_g
    m1, i1 = _first_index_of_max(c3, e_in_g, float(PER_GROUP), (1,))
    m2 = jnp.max(jnp.where(e_in_g == i1, -jnp.inf, c3), axis=1, keepdims=True)
    cur = m1 + m2
    gsel = jnp.zeros(cur.shape, F32)
    for _ in range(TOPK_GROUPS):
        _, gi = _first_index_of_max(cur, g_idx, float(N_GROUPS), (0,))
        hit = g_idx == gi
        gsel = jnp.where(hit, 1.0, gsel)
        cur = jnp.where(hit, -jnp.inf, cur)
    cur = jnp.where(gsel > 0.0, c3, -jnp.inf)
    sel = jnp.zeros(shape3, F32)
    picks = []
    for _ in range(TOP_K):
        _, ei = _first_index_of_max(cur, e_idx, float(N_EXPERTS), (0, 1))
        hit = e_idx == ei
        sel = jnp.where(hit, 1.0, sel)
        cur = jnp.where(hit, -jnp.inf, cur)
        picks.append(ei)
    w = sel * s3
    denom = jnp.sum(jnp.sum(w, axis=0, keepdims=True), axis=1, keepdims=True)
    gates3 = w / denom * ROUTED_SCALE
    sel2 = sel.reshape(N_EXPERTS, tm)
    before = jnp.dot(sel2.astype(BF16), tri_ref[...], preferred_element_type=F32)
    rank3 = (before + run_ref[:, 0:1]).reshape(shape3)
    run_ref[...] = run_ref[...] + jnp.sum(sel2, axis=1, keepdims=True)
    cnt_ref[...] = run_ref[...]

    def pick(values, hit):
        return jnp.sum(jnp.sum(jnp.where(hit, values, 0.0), axis=0, keepdims=True), axis=1)

    for k, ei in enumerate(picks):
        hit = e_idx == ei
        e8_ref[k:k + 1, :] = ei.reshape(1, tm).astype(jnp.int32)
        r8_ref[k:k + 1, :] = pick(rank3, hit).astype(jnp.int32)
        w8_ref[k:k + 1, :] = pick(gates3, hit)


def _route(x, rth, rtl, rbias, tm):
    n_tok, d = x.shape
    tri = (jnp.arange(tm)[:, None] < jnp.arange(tm)[None, :]).astype(BF16)
    const2 = lambda i: (0, 0)
    return pl.pallas_call(
        _route_kernel,
        out_shape=(
            jax.ShapeDtypeStruct((2, n_tok, HALF // 2), jnp.int32),
            jax.ShapeDtypeStruct((TOP_K, n_tok), jnp.int32),
            jax.ShapeDtypeStruct((TOP_K, n_tok), jnp.int32),
            jax.ShapeDtypeStruct((TOP_K, n_tok), F32),
            jax.ShapeDtypeStruct((N_EXPERTS, LANES), F32),
        ),
        grid=(n_tok // tm,),
        in_specs=[
            pl.BlockSpec((tm, d), lambda i: (i, 0)),
            pl.BlockSpec(rth.shape, const2),
            pl.BlockSpec(rtl.shape, const2),
            pl.BlockSpec(rbias.shape, const2),
            pl.BlockSpec((tm, tm), const2),
        ],
        out_specs=(
            pl.BlockSpec((2, tm, HALF // 2), lambda i: (0, i, 0)),
            pl.BlockSpec((TOP_K, tm), lambda i: (0, i)),
            pl.BlockSpec((TOP_K, tm), lambda i: (0, i)),
            pl.BlockSpec((TOP_K, tm), lambda i: (0, i)),
            pl.BlockSpec((N_EXPERTS, LANES), const2),
        ),
        scratch_shapes=[pltpu.VMEM((N_EXPERTS, LANES), F32)],
        compiler_params=_cparams("arbitrary"),
        name="moe_router",
    )(x, rth, rtl, rbias, tri)


def _positions_kernel(start_ref, e8_ref, r8_ref, idx_ref, *, n_rows):
    e8 = e8_ref[...]
    pos = r8_ref[...]
    for e in range(N_EXPERTS):
        pos = pos + jnp.where(e8 == e, start_ref[e], 0)
    idx_ref[0] = pos
    idx_ref[1] = pos + n_rows


def _positions(start, e8, r8, n_rows, tn):
    n_tok = e8.shape[1]
    grid_spec = pltpu.PrefetchScalarGridSpec(
        num_scalar_prefetch=1,
        grid=(n_tok // tn,),
        in_specs=[pl.BlockSpec((TOP_K, tn), lambda i, st: (0, i)),
                  pl.BlockSpec((TOP_K, tn), lambda i, st: (0, i))],
        out_specs=pl.BlockSpec((2, TOP_K, tn), lambda i, st: (0, 0, i)),
    )
    return pl.pallas_call(
        functools.partial(_positions_kernel, n_rows=n_rows),
        out_shape=jax.ShapeDtypeStruct((2, TOP_K, n_tok), jnp.int32),
        grid_spec=grid_spec,
        compiler_params=_cparams("parallel"),
        name="moe_positions",
    )(start, e8, r8)


def _experts_kernel(te_ref, used_ref, xs_ref, wg_ref, wu_ref, wd_ref, ys_ref):
    @pl.when(pl.program_id(0) < used_ref[0])
    def _():
        x = jnp.concatenate([_unpack_half(xs_ref[0]), _unpack_half(xs_ref[1])], axis=1).astype(BF16)
        h = (jax.nn.silu(jnp.dot(x, wg_ref[0], preferred_element_type=F32))
             * jnp.dot(x, wu_ref[0], preferred_element_type=F32))
        y = jnp.dot(h.astype(BF16), wd_ref[0], preferred_element_type=F32)
        ys_ref[0] = _pack_half(y[:, :HALF])
        ys_ref[1] = _pack_half(y[:, HALF:])


def _experts(tile_expert, n_used, xs, eg, eu, ed):
    n_rows = xs.shape[1]
    _, d, f = eg.shape
    row_block = lambda i, te, nu: (0, jnp.minimum(i, nu[0] - 1), 0)
    grid_spec = pltpu.PrefetchScalarGridSpec(
        num_scalar_prefetch=2,
        grid=(n_rows // ROW_TILE,),
        in_specs=[
            pl.BlockSpec((2, ROW_TILE, HALF // 2), row_block),
            pl.BlockSpec((1, d, f), lambda i, te, nu: (te[i], 0, 0)),
            pl.BlockSpec((1, d, f), lambda i, te, nu: (te[i], 0, 0)),
            pl.BlockSpec((1, f, d), lambda i, te, nu: (te[i], 0, 0)),
        ],
        out_specs=pl.BlockSpec((2, ROW_TILE, HALF // 2), row_block),
    )
    return pl.pallas_call(
        _experts_kernel,
        out_shape=jax.ShapeDtypeStruct(xs.shape, jnp.int32),
        grid_spec=grid_spec,
        compiler_params=_cparams("arbitrary"),
        name="moe_experts",
    )(tile_expert, n_used, xs, eg, eu, ed)


def _combine_kernel(x_ref, yg_ref, w8_ref, sg_ref, su_ref, sd_ref, g_ref, b_ref, o_ref, *, alpha):
    x = x_ref[...]
    tm = x.shape[0]
    xh = x.astype(BF16)
    hs = (jax.nn.silu(jnp.dot(xh, sg_ref[...], preferred_element_type=F32))
          * jnp.dot(xh, su_ref[...], preferred_element_type=F32))
    acc = jnp.dot(hs.astype(BF16), sd_ref[...], preferred_element_type=F32)
    w8 = jnp.concatenate([w8_ref[...], jnp.zeros((LANES - TOP_K, tm), F32)], axis=0).T
    for k in range(TOP_K):
        yk = jnp.concatenate([_unpack_half(yg_ref[0, k]), _unpack_half(yg_ref[1, k])], axis=1)
        acc = acc + w8[:, k:k + 1] * yk
    o_ref[...] = _layer_norm(alpha * x + acc, g_ref[...], b_ref[...])


def _combine(x, yg, w8, sg, su, sd, g, b, tm, alpha):
    n_tok, d = x.shape
    const2 = lambda i: (0, 0)
    return pl.pallas_call(
        functools.partial(_combine_kernel, alpha=alpha),
        out_shape=jax.ShapeDtypeStruct((n_tok, d), F32),
        grid=(n_tok // tm,),
        in_specs=[
            pl.BlockSpec((tm, d), lambda i: (i, 0)),
            pl.BlockSpec((2, TOP_K, tm, HALF // 2), lambda i: (0, 0, i, 0)),
            pl.BlockSpec((TOP_K, tm), lambda i: (0, i)),
            pl.BlockSpec(sg.shape, const2),
            pl.BlockSpec(su.shape, const2),
            pl.BlockSpec(sd.shape, const2),
            pl.BlockSpec((1, d), const2),
            pl.BlockSpec((1, d), const2),
        ],
        out_specs=pl.BlockSpec((tm, d), lambda i: (i, 0)),
        compiler_params=_cparams("parallel"),
        name="moe_combine",
    )(x, yg, w8, sg, su, sd, g, b)


def _sc_mesh():
    return plsc.VectorSubcoreMesh(core_axis_name="c", subcore_axis_name="s")


def _sc_gather_rows(table, idx):
    n = idx.shape[0]
    d = table.shape[1]

    @functools.partial(pl.kernel, out_type=jax.ShapeDtypeStruct((n, d), table.dtype),
                       mesh=_sc_mesh())
    def gather(table_hbm, idx_hbm, out_hbm):
        def body(idx_vmem, out_vmem):
            pltpu.sync_copy(table_hbm.at[idx_vmem.at[0]], out_vmem)

        pltpu.emit_pipeline(
            body, grid=(n // SC_WINDOW,),
            in_specs=[pl.BlockSpec((1, SC_WINDOW), index_map=lambda i: (0, i))],
            out_specs=[pl.BlockSpec((SC_WINDOW, d), index_map=lambda i: (i, 0))],
            core_axis_name=("c", "s"), dimension_semantics=(pltpu.PARALLEL,), trace_scopes=False,
        )(idx_hbm, out_hbm)

    return gather(table, idx.reshape(1, n))


def _sc_scatter_rows(rows, idx, n_out, src_block):
    n = idx.shape[0]
    d = rows.shape[1]

    @functools.partial(pl.kernel, out_type=jax.ShapeDtypeStruct((n_out, d), rows.dtype),
                       mesh=_sc_mesh())
    def scatter(rows_hbm, idx_hbm, out_hbm):
        def body(rows_vmem, idx_vmem):
            pltpu.sync_copy(rows_vmem, out_hbm.at[idx_vmem.at[0]])

        pltpu.emit_pipeline(
            body, grid=(n // SC_WINDOW,),
            in_specs=[pl.BlockSpec((SC_WINDOW, d), index_map=lambda i: (src_block(i), 0)),
                      pl.BlockSpec((1, SC_WINDOW), index_map=lambda i: (0, i))],
            out_specs=[],
            core_axis_name=("c", "s"), dimension_semantics=(pltpu.PARALLEL,), trace_scopes=False,
        )(rows_hbm, idx_hbm)

    return scatter(rows, idx.reshape(1, n))


def _moe_dispatch(x, rth, rtl, rbias, tm):
    n_tok, d = x.shape
    n_rows = TOP_K * n_tok + N_EXPERTS * ROW_TILE
    n_tiles = n_rows // ROW_TILE
    xp, e8, r8, w8, cnt = _route(x, rth, rtl, rbias, tm)
    counts = cnt[:, 0].astype(jnp.int32)
    padded = (counts + ROW_TILE - 1) // ROW_TILE * ROW_TILE
    ends = jnp.cumsum(padded)
    start = ends - padded
    n_used = (ends[-1:] // ROW_TILE).astype(jnp.int32)
    tile_ids = jnp.arange(n_tiles, dtype=jnp.int32)
    tile_expert = jnp.minimum(
        jnp.sum((tile_ids[:, None] >= (ends // ROW_TILE)[None, :]).astype(jnp.int32), axis=1),
        N_EXPERTS - 1)
    idx = _positions(start, e8, r8, n_rows, _pick_tile(n_tok, 2048)).reshape(-1)
    blocks_per_half = n_tok // SC_WINDOW
    xs = _sc_scatter_rows(
        xp.reshape(2 * n_tok, HALF // 2), idx, 2 * n_rows,
        lambda i: (i // (TOP_K * blocks_per_half)) * blocks_per_half + i % blocks_per_half)
    return xs.reshape(2, n_rows, HALF // 2), idx, tile_expert, n_used, w8


def _moe_experts(xs, idx, tile_expert, n_used, eg, eu, ed):
    n_rows = xs.shape[1]
    ys = _experts(tile_expert, n_used, xs, eg, eu, ed)
    yg = _sc_gather_rows(ys.reshape(2 * n_rows, HALF // 2), idx)
    return yg.reshape(2, TOP_K, idx.shape[0] // (2 * TOP_K), HALF // 2)


def _rope_tables(seq):
    t = jnp.arange(seq)
    row = (t // GRID_W).astype(F32)
    colp = (t % GRID_W).astype(F32)
    inv = ROPE_THETA ** (-jnp.arange(0, ROPE_AXIS_DIM, 2, dtype=F32) / ROPE_AXIS_DIM)
    ang = jnp.concatenate([row[:, None] * inv[None], colp[:, None] * inv[None]], axis=-1)
    ang = jnp.repeat(ang, 2, axis=-1)
    sign = jnp.where(jnp.arange(HEAD_DIM) % 2 == 0, -1.0, 1.0).astype(F32)
    cos = jnp.tile(jnp.cos(ang), (1, 2))
    sin = jnp.tile(jnp.sin(ang) * sign[None], (1, 2))
    return cos, sin


def _gqa_head_order():
    order = []
    for kvp in range(N_KV_HEADS // 2):
        base = 2 * GQA_GROUP * kvp
        for g in range(GQA_GROUP):
            order += [base + g, base + GQA_GROUP + g]
    return np.array(order)


def _pick_tile(n, target):
    t = min(n, target)
    while n % t:
        t //= 2
    return t


def kernel(x_prompt, x_sample, mem_prompt, mem_sample, a_w_in, a_lambda, a_subln, b_w_in, b_rpb,
           c_w_in, c_q_norm, c_k_norm, w_mem_kv, w_out, ln1_g, ln1_b, router, router_bias,
           e_gate, e_up, e_down, s_gate, s_up, s_down, ln2_g, ln2_b):
    depth = w_out.shape[0]
    alpha = (2.0 * depth) ** 0.25
    seq = x_prompt.shape[1]
    d = x_prompt.shape[2]
    assert x_sample.shape[1] == seq and seq % (NA_ROWS * GRID_W) == 0
    n_mem = mem_prompt.shape[1]
    batches = [x_prompt.shape[0], x_sample.shape[0]]
    xs_ = [x_prompt.reshape(-1, d), x_sample.reshape(-1, d)]
    mems = [mem_prompt.reshape(-1, d), mem_sample.reshape(-1, d)]

    tm = _pick_tile(seq, 512)
    tk = _pick_tile(seq, 512)
    head_order = _gqa_head_order()
    col_order = (head_order[:, None] * HEAD_DIM + np.arange(HEAD_DIM)[None]).reshape(-1)
    cos, sin = _rope_tables(seq)
    slopes = _alibi_slopes(N_MIX_HEADS).reshape(N_MIX_PAIRS, 2)
    slope_tab = np.zeros((N_MIX_PAIRS, 8, LANES), np.float32)
    slope_tab[:, 0:2, :] = slopes[:, :, None]
    slope_tab = jnp.asarray(slope_tab) * LOG2E
    ones_row = functools.partial(jnp.ones, dtype=F32)
    mem_q_scale = jnp.full((MEM_WIDTH,), HEAD_DIM ** -0.5, F32)

    for i in range(depth):
        mixer, occ = i % N_MIXERS, i // N_MIXERS
        w_o = w_out[i]
        rope = None
        if mixer == 0:
            w_in = a_w_in[occ]
            q_scale = DIFF_QK_DIM ** -0.5 * LOG2E
            gain = jnp.tile(a_subln[occ].astype(F32), 2)[None]
            lam = a_lambda[occ].astype(F32)
        elif mixer == 1:
            w_in = b_w_in[occ]
            q_scale = HEAD_DIM ** -0.5
            na_bias = _na_bias_table(b_rpb[occ])
        else:
            w_in = c_w_in[occ]
            w_in = jnp.concatenate([w_in[:, :MIX_WIDTH][:, col_order], w_in[:, MIX_WIDTH:]], axis=1)
            w_o = jnp.concatenate([w_o[:MIX_WIDTH][col_order], w_o[MIX_WIDTH:]], axis=0)
            q_scale = HEAD_DIM ** -0.5 * LOG2E
            qg = jnp.tile(c_q_norm[occ].astype(F32), 2)[None, None]
            kg = jnp.tile(c_k_norm[occ].astype(F32), 2)[None, None]
            gains = jnp.concatenate([jnp.tile(qg, (N_MIX_PAIRS, 1, 1)),
                                     jnp.tile(kg, (N_KV_HEADS // 2, 1, 1))], axis=0)
            rope = (gains, cos, sin)
        colscale = jnp.concatenate([
            jnp.full((MIX_WIDTH,), q_scale, F32),
            ones_row((w_in.shape[1] - MIX_WIDTH - MEM_WIDTH,)), mem_q_scale])[None]
        w_in = w_in.astype(BF16)
        w_o = w_o.astype(BF16)
        w_kv = w_mem_kv[i].astype(BF16)
        rt = router[i].T.astype(F32)
        rth = rt.astype(BF16)
        rtl = (rt - rth.astype(F32)).astype(BF16)
        rbias = router_bias[i].astype(F32)[:, None]
        eg, eu, ed = e_gate[i].astype(BF16), e_up[i].astype(BF16), e_down[i].astype(BF16)
        sg, su, sd = s_gate[i].astype(BF16), s_up[i].astype(BF16), s_down[i].astype(BF16)

        dispatched = []
        for s in range(len(xs_)):
            x, batch = xs_[s], batches[s]
            heads = _in_projection(x, w_in, colscale, tm, seq, rope=rope)
            if mixer == 0:
                mix = _diff_attention(heads, slope_tab, lam, gain, batch, seq,
                                      _pick_tile(seq, 256), _pick_tile(seq, 1024),
                                      _diff_lambda_init(i))
            elif mixer == 1:
                mix = _na_attention(heads, na_bias, batch, seq)
            else:
                mix = _gqa_attention(heads, batch, seq, _pick_tile(seq, 512), tk)
            kv = _matmul(mems[s], w_kv, _pick_tile(batch * n_mem, 512))
            kv = kv.reshape(batch, n_mem, 2 * MEM_WIDTH)
            x = _out_projection(mix, heads, kv, x, w_o, ln1_g[i][None], ln1_b[i][None],
                                batch, seq, tm, alpha)
            xs_[s] = x
            dispatched.append(_moe_dispatch(x, rth, rtl, rbias, tm))
        gathered = [_moe_experts(*disp[:4], eg, eu, ed) for disp in dispatched]
        for s in range(len(xs_)):
            xs_[s] = _combine(xs_[s], gathered[s], dispatched[s][4], sg, su, sd,
                              ln2_g[i][None], ln2_b[i][None], tm, alpha)

    return tuple(x.reshape(b, seq, d) for x, b in zip(xs_, batches))
```
